```python
import math
import jax, jax.numpy as jnp
from jax import lax
import numpy as np

D_MODEL = 2048
BATCH = 2
SEQ = 4096
DEPTH = 2
DEC_BATCH = 16
DEC_SEQ = 32
PAST_LEN = 1024

CHUNK = 64
N_MIXERS = 2
N_CONV_LAYERS = (DEPTH + N_MIXERS - 1) // N_MIXERS
N_ATTN_LAYERS = DEPTH // N_MIXERS
CONV_K = 31
N_HEADS = 32
N_KV_HEADS = 4
HEAD_DIM = 64
GROUP = N_HEADS // N_KV_HEADS
WINDOW = 128
N_BUCKETS = 32
MAX_DISTANCE = 128
D_FF = 5632
FFN_K = 3
EPS = 1e-6
NEG_INF = -1e30
ATTN_SCALE = HEAD_DIM ** -0.5

kernel_name = "streaming_conformer_swa_hybrid_step"


def rmsnorm(x, g):
    xf = x.astype(jnp.float32)
    y = xf * lax.rsqrt(jnp.mean(xf * xf, axis=-1, keepdims=True) + EPS)
    return (y * g.astype(jnp.float32)).astype(x.dtype)


def layernorm(x, g, b):
    xf = x.astype(jnp.float32)
    mu = jnp.mean(xf, axis=-1, keepdims=True)
    var = jnp.mean(jnp.square(xf - mu), axis=-1, keepdims=True)
    y = (xf - mu) * lax.rsqrt(var + EPS) * g.astype(jnp.float32) + b.astype(jnp.float32)
    return y.astype(x.dtype)


def modulate(h, shift, scale):
    return h * (1 + scale[:, None, :]) + shift[:, None, :]


def causal_dwconv(u, past, w, b):
    width = w.shape[0]
    full = jnp.concatenate([past.astype(u.dtype), u], axis=1)
    y = lax.conv_general_dilated(full, w[:, None, :].astype(u.dtype), (1,), 'VALID',
                                 dimension_numbers=('NWC', 'WIO', 'NWC'),
                                 feature_group_count=u.shape[-1])
    return y + b, full[:, -(width - 1):]


def t5_bucket(rel):
    half = N_BUCKETS // 2
    max_exact = half // 2
    n = jnp.abs(rel)
    ret = jnp.where(rel > 0, half, 0)
    nf = jnp.maximum(n, 1).astype(jnp.float32)
    large = max_exact + (jnp.log(nf / max_exact) / math.log(MAX_DISTANCE / max_exact)
                         * (half - max_exact)).astype(jnp.int32)
    large = jnp.minimum(large, half - 1)
    return ret + jnp.where(n < max_exact, n, large)


def conformer_conv(h, past, w_in, b_in, w_dw, b_dw, ln_g, ln_b, w_out, b_out):
    a, g = jnp.split(h @ w_in + b_in, 2, axis=-1)
    u = a * jax.nn.sigmoid(g)
    y, new_past = causal_dwconv(u, past, w_dw, b_dw)
    y = jax.nn.silu(layernorm(y, ln_g, ln_b))
    return y @ w_out + b_out, new_past


def conv_glu_ffn(h, past, w_up, w_dw, b_dw, w_down):
    g, v = jnp.split(h @ w_up, 2, axis=-1)
    g, new_past = causal_dwconv(g, past, w_dw, b_dw)
    return (jax.nn.gelu(g) * v) @ w_down, new_past


def swa_attention(h, past_k, past_v, past_valid, keep, w_q, w_k, w_v, w_o, sinks, rel_bias):
    B, T, _ = h.shape
    P = past_k.shape[1]
    q = (h @ w_q).reshape(B, T, N_KV_HEADS, GROUP, HEAD_DIM)
    k = (h @ w_k).reshape(B, T, N_KV_HEADS, HEAD_DIM)
    v = (h @ w_v).reshape(B, T, N_KV_HEADS, HEAD_DIM)
    k_full = jnp.concatenate([past_k.astype(k.dtype), k], axis=1)
    v_full = jnp.concatenate([past_v.astype(v.dtype), v], axis=1)
    Q = min(T, CHUNK)
    N = T // Q
    K = P + Q
    qb = q.reshape(B, N, Q, N_KV_HEADS, GROUP, HEAD_DIM)
    key_idx = jnp.arange(N)[:, None] * Q + jnp.arange(K)[None, :]
    kb = k_full[:, key_idx]
    vb = v_full[:, key_idx]
    rel = jnp.arange(K)[None, :] - P - jnp.arange(Q)[:, None]
    bias = rel_bias[t5_bucket(rel)].astype(jnp.float32)
    bias = jnp.transpose(bias, (2, 0, 1)).reshape(N_KV_HEADS, GROUP, Q, K)
    valid = (key_idx - P) >= -past_valid
    s = jnp.einsum('bnqhgd,bnkhd->bnhgqk', qb, kb,
                   preferred_element_type=jnp.float32) * ATTN_SCALE + bias
    s = jnp.where(valid[None, :, None, None, None, :], s, NEG_INF)
    sink = sinks.astype(jnp.float32).reshape(N_KV_HEADS, GROUP)[None, None, :, :, None, None]
    mx = jnp.maximum(jnp.max(s, axis=-1, keepdims=True), sink)
    p = jnp.exp(s - mx)
    p = p / (jnp.sum(p, axis=-1, keepdims=True) + jnp.exp(sink - mx))
    o = jnp.einsum('bnhgqk,bnkhd->bnqhgd', p, vb.astype(jnp.float32))
    o = o.astype(h.dtype).reshape(B, T, N_HEADS * HEAD_DIM)
    return o @ w_o, k_full[:, -keep:], v_full[:, -keep:]


def setup_inputs(seed: int = 0) -> dict:
    key = jax.random.key(seed)
    ks = iter(jax.random.split(key, 40))
    D = D_MODEL
    win_cache = min(WINDOW, PAST_LEN)

    def nrm(shape, scale):
        return jax.random.normal(next(ks), shape, jnp.float32) * scale

    return {
        "x_prompt": nrm((BATCH, SEQ, D), 1.0),
        "x_sample": nrm((DEC_BATCH, DEC_SEQ, D), 1.0),
        "c_prompt": nrm((BATCH, D), 1.0),
        "c_sample": nrm((DEC_BATCH, D), 1.0),
        "cache_conv": nrm((N_CONV_LAYERS, DEC_BATCH, CONV_K - 1, D), 0.5),
        "cache_k": nrm((N_ATTN_LAYERS, DEC_BATCH, win_cache, N_KV_HEADS, HEAD_DIM), 1.0),
        "cache_v": nrm((N_ATTN_LAYERS, DEC_BATCH, win_cache, N_KV_HEADS, HEAD_DIM), 1.0),
        "cache_ffn": nrm((DEPTH, DEC_BATCH, FFN_K - 1, D_FF), 1.0),
        "w_mod": nrm((DEPTH, D, 6 * D), 0.5 * D ** -0.5),
        "b_mod": nrm((DEPTH, 6 * D), 0.02),
        "g_norm": 1.0 + nrm((DEPTH, 4, D), 0.05),
        "conv_w_in": nrm((N_CONV_LAYERS, D, 2 * D), D ** -0.5),
        "conv_b_in": nrm((N_CONV_LAYERS, 2 * D), 0.02),
        "conv_w_dw": nrm((N_CONV_LAYERS, CONV_K, D), CONV_K ** -0.5),
        "conv_b_dw": nrm((N_CONV_LAYERS, D), 0.02),
        "conv_ln_g": 1.0 + nrm((N_CONV_LAYERS, D), 0.05),
        "conv_ln_b": nrm((N_CONV_LAYERS, D), 0.02),
        "conv_w_out": nrm((N_CONV_LAYERS, D, D), D ** -0.5),
        "conv_b_out": nrm((N_CONV_LAYERS, D), 0.02),
        "attn_w_q": nrm((N_ATTN_LAYERS, D, N_HEADS * HEAD_DIM), D ** -0.5),
        "attn_w_k": nrm((N_ATTN_LAYERS, D, N_KV_HEADS * HEAD_DIM), D ** -0.5),
        "attn_w_v": nrm((N_ATTN_LAYERS, D, N_KV_HEADS * HEAD_DIM), D ** -0.5),
        "attn_w_o": nrm((N_ATTN_LAYERS, N_HEADS * HEAD_DIM, D), (N_HEADS * HEAD_DIM) ** -0.5),
        "attn_sinks": nrm((N_ATTN_LAYERS, N_HEADS), 0.5),
        "rel_bias": nrm((N_BUCKETS, N_HEADS), 0.3),
        "ffn_w_up": nrm((DEPTH, D, 2 * D_FF), D ** -0.5),
        "ffn_w_dw": nrm((DEPTH, FFN_K, D_FF), FFN_K ** -0.5),
        "ffn_b_dw": nrm((DEPTH, D_FF), 0.02),
        "ffn_w_down": nrm((DEPTH, D_FF, D), D_FF ** -0.5),
    }


def reference(x_prompt, x_sample, c_prompt, c_sample, cache_conv, cache_k, cache_v, cache_ffn,
              w_mod, b_mod, g_norm,
              conv_w_in, conv_b_in, conv_w_dw, conv_b_dw, conv_ln_g, conv_ln_b, conv_w_out, conv_b_out,
              attn_w_q, attn_w_k, attn_w_v, attn_w_o, attn_sinks, rel_bias,
              ffn_w_up, ffn_w_dw, ffn_b_dw, ffn_w_down):
    xs = [x_prompt, x_sample]
    cs = [c_prompt, c_sample]
    conv_new = [[], []]
    k_new = [[], []]
    v_new = [[], []]
    ffn_new = [[], []]
    for i in range(DEPTH):
        j = i // N_MIXERS
        for grp in range(2):
            x = xs[grp]
            Bg, T = x.shape[0], x.shape[1]
            sh1, sc1, ga1, sh2, sc2, ga2 = jnp.split(
                jax.nn.silu(cs[grp]) @ w_mod[i] + b_mod[i], 6, axis=-1)
            h = modulate(rmsnorm(x, g_norm[i, 0]), sh1, sc1)
            if i % N_MIXERS == 0:
                past = jnp.zeros((Bg, CONV_K - 1, D_MODEL), x.dtype) if grp == 0 else cache_conv[j]
                m, st = conformer_conv(h, past, conv_w_in[j], conv_b_in[j], conv_w_dw[j], conv_b_dw[j],
                                       conv_ln_g[j], conv_ln_b[j], conv_w_out[j], conv_b_out[j])
                conv_new[grp].append(st)
            else:
                if grp == 0:
                    past_k = jnp.zeros((Bg, WINDOW, N_KV_HEADS, HEAD_DIM), x.dtype)
                    past_v = past_k
                    past_valid = 0
                    keep = min(WINDOW, T)
                else:
                    past_k = cache_k[j]
                    past_v = cache_v[j]
                    past_valid = past_k.shape[1]
                    keep = past_k.shape[1]
                m, ks_, vs_ = swa_attention(h, past_k, past_v, past_valid, keep,
                                            attn_w_q[j], attn_w_k[j], attn_w_v[j], attn_w_o[j],
                                            attn_sinks[j], rel_bias)
                k_new[grp].append(ks_)
                v_new[grp].append(vs_)
            x = x + ga1[:, None, :] * rmsnorm(m, g_norm[i, 1])
            h = modulate(rmsnorm(x, g_norm[i, 2]), sh2, sc2)
            past_f = jnp.zeros((Bg, FFN_K - 1, D_FF), x.dtype) if grp == 0 else cache_ffn[i]
            f, fst = conv_glu_ffn(h, past_f, ffn_w_up[i], ffn_w_dw[i], ffn_b_dw[i], ffn_w_down[i])
            ffn_new[grp].append(fst)
            xs[grp] = x + ga2[:, None, :] * rmsnorm(f, g_norm[i, 3])
    conv_state_prompt = jnp.stack(conv_new[0])
    conv_state_sample = jnp.stack(conv_new[1])
    k_state_prompt = jnp.stack(k_new[0])
    v_state_prompt = jnp.stack(v_new[0])
    k_state_sample = jnp.stack(k_new[1])
    v_state_sample = jnp.stack(v_new[1])
    ffn_state_prompt = jnp.stack(ffn_new[0])
    ffn_state_sample = jnp.stack(ffn_new[1])
    return (xs[0], xs[1], conv_state_prompt, conv_state_sample, k_state_prompt, v_state_prompt,
            k_state_sample, v_state_sample, ffn_state_prompt, ffn_state_sample)
```

```python
import functools
import math

import jax
import jax.numpy as jnp
from jax import lax
from jax.experimental import pallas as pl
from jax.experimental.pallas import tpu as pltpu

F32 = jnp.float32
BF16 = jnp.bfloat16

D_MODEL = 2048
D_FF = 5632
DEPTH = 2
CONV_K = 31
FFN_K = 3
N_HEADS = 32
N_KV_HEADS = 4
GROUP = N_HEADS // N_KV_HEADS
HEAD_DIM = 64
KV_DIM = N_KV_HEADS * HEAD_DIM
CHUNK = 64
WINDOW = 128
N_BUCKETS = 32
MAX_DISTANCE = 128
EPS = 1e-6
NEG_INF = -1e30
ATTN_SCALE = HEAD_DIM ** -0.5

SUBLANES = 8
LANES = 128
CONV_HIST = 32
VMEM_LIMIT = 58 * 1024 * 1024


def _cparams(n_grid):
    return pltpu.CompilerParams(dimension_semantics=("arbitrary",) * n_grid,
                                vmem_limit_bytes=VMEM_LIMIT)


def _dot(a, b):
    return jnp.dot(a, b, preferred_element_type=F32)


def _norm_mod(x_ref, dst_ref, g_ref, sc_ref, sh_ref, S, L):
    rc = 16
    for s in range(S):
        mul = g_ref[...] * (1.0 + sc_ref[s:s + 1, :])
        add = sh_ref[s:s + 1, :]

        def body(r, c, s=s, mul=mul, add=add):
            rows = pl.ds(pl.multiple_of(s * L + r * rc, rc), rc)
            x = x_ref[rows, :]
            inv = lax.rsqrt(jnp.mean(x * x, axis=-1, keepdims=True) + EPS)
            dst_ref[rows, :] = (x * inv * mul + add).astype(dst_ref.dtype)
            return c

        lax.fori_loop(0, L // rc, body, 0)


def _residual(m_ref, x_ref, g_ref, ga_ref, out_ref, S, L):
    rc = 16
    for s in range(S):
        mul = g_ref[...] * ga_ref[s:s + 1, :]

        def body(r, c, s=s, mul=mul):
            rows = pl.ds(pl.multiple_of(s * L + r * rc, rc), rc)
            m = m_ref[rows, :]
            inv = lax.rsqrt(jnp.mean(m * m, axis=-1, keepdims=True) + EPS)
            out_ref[rows, :] = x_ref[rows, :] + m * inv * mul
            return c

        lax.fori_loop(0, L // rc, body, 0)


def _mod_kernel(c_ref, w_ref, b_ref, o_ref):
    c = c_ref[...]
    o_ref[...] = _dot(c * jax.nn.sigmoid(c), w_ref[...]) + b_ref[...]


def _modulation(c_all, w_mod, b_mod):
    rows = c_all.shape[0]
    tn = 1024
    return pl.pallas_call(
        _mod_kernel,
        grid=(DEPTH, 6 * D_MODEL // tn),
        in_specs=[pl.BlockSpec((rows, D_MODEL), lambda i, j: (0, 0)),
                  pl.BlockSpec((None, D_MODEL, tn), lambda i, j: (i, 0, j)),
                  pl.BlockSpec((None, 1, tn), lambda i, j: (i, 0, j))],
        out_specs=pl.BlockSpec((None, rows, tn), lambda i, j: (i, 0, j)),
        out_shape=jax.ShapeDtypeStruct((DEPTH, rows, 6 * D_MODEL), F32),
        compiler_params=_cparams(2),
        name="modulation",
    )(c_all, w_mod, b_mod.reshape(DEPTH, 1, 6 * D_MODEL))


def _mod_spec(S, which):
    if S == 1:
        return pl.BlockSpec((None, None, 1, D_MODEL), lambda b, t, *_: (b, which, 0, 0))
    return pl.BlockSpec((None, S, D_MODEL), lambda b, t, *_: (which, 0, 0))


def _vec_spec(n):
    return pl.BlockSpec((1, n), lambda *_: (0, 0))


def _conv_in_kernel(x_ref, g_ref, sc_ref, sh_ref, wa_ref, wg_ref, ba_ref, bg_ref, u_ref, h_ref, *, S, L):
    @pl.when(pl.program_id(2) == 0)
    def _():
        _norm_mod(x_ref, h_ref, g_ref, sc_ref, sh_ref, S, L)

    h = h_ref[...]
    a = _dot(h, wa_ref[...]) + ba_ref[...]
    g = _dot(h, wg_ref[...]) + bg_ref[...]
    u_ref[...] = a * jax.nn.sigmoid(g)


def _conv_in(x, mods, g0, w_in, b_in, *, S, L, tm):
    Bp, T, _ = x.shape
    tn = 512
    J = D_MODEL // tn
    b2 = b_in.reshape(1, 2 * D_MODEL)
    return pl.pallas_call(
        functools.partial(_conv_in_kernel, S=S, L=L),
        grid=(Bp, T // tm, J),
        in_specs=[pl.BlockSpec((None, tm, D_MODEL), lambda b, t, j: (b, t, 0)),
                  _vec_spec(D_MODEL), _mod_spec(S, 1), _mod_spec(S, 0),
                  pl.BlockSpec((D_MODEL, tn), lambda b, t, j: (0, j)),
                  pl.BlockSpec((D_MODEL, tn), lambda b, t, j: (0, j + J)),
                  pl.BlockSpec((1, tn), lambda b, t, j: (0, j)),
                  pl.BlockSpec((1, tn), lambda b, t, j: (0, j + J))],
        out_specs=pl.BlockSpec((None, tm, tn), lambda b, t, j: (b, t, j)),
        out_shape=jax.ShapeDtypeStruct((Bp, T, D_MODEL), F32),
        scratch_shapes=[pltpu.VMEM((tm, D_MODEL), BF16)],
        compiler_params=_cparams(3),
        name="conv_in",
    )(x, g0, mods, mods, w_in, w_in, b2, b2)


def _dwconv_ln_silu(full_ref, wb_ref, bdw_ref, lng_ref, lnb_ref, y_ref, a_ref, S, L, R):
    n_a = (CONV_K - 1 + 2 + SUBLANES - 1) // SUBLANES + 1
    win_rows = R + CONV_HIST
    for s in range(S):

        def body(r, c, s=s):
            base = pl.multiple_of(r * R, R)
            for cb in range(D_MODEL // LANES):
                cols = slice(cb * LANES, (cb + 1) * LANES)
                win = full_ref[s, pl.ds(base, win_rows), cols]
                acc = jnp.broadcast_to(bdw_ref[:, cols], (R // SUBLANES, SUBLANES, LANES))
                for sh in range(SUBLANES):
                    wsh = win if sh == 0 else pltpu.roll(win, win_rows - sh, axis=0)
                    for a in range(n_a):
                        k = SUBLANES * a + sh - 2
                        if 0 <= k < CONV_K:
                            tap = wsh[SUBLANES * a:SUBLANES * a + R, :]
                            acc = acc + tap.reshape(R // SUBLANES, SUBLANES, LANES) * wb_ref[k, :, cols]
                y_ref[:, cols] = acc.reshape(R, LANES)
            for q in range(R // 16):
                y = y_ref[q * 16:(q + 1) * 16, :]
                mu = jnp.mean(y, axis=-1, keepdims=True)
                yc = y - mu
                var = jnp.mean(yc * yc, axis=-1, keepdims=True)
                z = yc * lax.rsqrt(var + EPS) * lng_ref[...] + lnb_ref[...]
                rows = pl.ds(pl.multiple_of(s * L + base + q * 16, 16), 16)
                a_ref[rows, :] = z * jax.nn.sigmoid(z)
            return c

        lax.fori_loop(0, L // R, body, 0)


def _conv_out_kernel(*refs, S, L, R, prompt):
    if prompt:
        (u_ref, halo_ref, wdw_ref, bdw_ref, lng_ref, lnb_ref, wout_ref, bout_ref, x_ref, g_ref, ga_ref,
         x1_ref, full_ref, wb_ref, y_ref, a_ref) = refs
    else:
        (u_ref, hist_ref, wdw_ref, bdw_ref, lng_ref, lnb_ref, wout_ref, bout_ref, x_ref, g_ref, ga_ref,
         x1_ref, full_ref, wb_ref, y_ref, a_ref) = refs
    t = pl.program_id(1)

    @pl.when((pl.program_id(0) == 0) & (t == 0))
    def _():
        for k in range(CONV_K):
            wb_ref[k] = jnp.broadcast_to(wdw_ref[k:k + 1, :], (SUBLANES, D_MODEL))

    if prompt:
        @pl.when(t == 0)
        def _():
            full_ref[0, 0:CONV_HIST, :] = jnp.zeros((CONV_HIST, D_MODEL), F32)

        @pl.when(t > 0)
        def _():
            full_ref[0, 0:CONV_HIST, :] = halo_ref[...]

        full_ref[0, CONV_HIST:CONV_HIST + L, :] = u_ref[...]
    else:
        pad = CONV_HIST - (CONV_K - 1)
        full_ref[:, 0:pad, :] = jnp.zeros((S, pad, D_MODEL), F32)
        full_ref[:, pad:CONV_HIST, :] = hist_ref[...]
        full_ref[:, CONV_HIST:CONV_HIST + L, :] = u_ref[...].reshape(S, L, D_MODEL)

    _dwconv_ln_silu(full_ref, wb_ref, bdw_ref, lng_ref, lnb_ref, y_ref, a_ref, S, L, R)
    x1_ref[...] = _dot(a_ref[...], wout_ref[...]) + bout_ref[...]
    _residual(x1_ref, x_ref, g_ref, ga_ref, x1_ref, S, L)


def _conv_out(u, hist, x, mods, g1, w_dw, b_dw, ln_g, ln_b, w_out, b_out, *, S, L, tm, R):
    Bp, T, _ = x.shape
    prompt = hist is None
    row_spec = pl.BlockSpec((None, tm, D_MODEL), lambda b, t: (b, t, 0))
    if prompt:
        per = tm // CONV_HIST
        second = pl.BlockSpec((None, CONV_HIST, D_MODEL), lambda b, t: (b, jnp.maximum(t * per - 1, 0), 0))
        second_arg = u
    else:
        second = pl.BlockSpec((S, CONV_K - 1, D_MODEL), lambda b, t: (0, 0, 0))
        second_arg = hist
    return pl.pallas_call(
        functools.partial(_conv_out_kernel, S=S, L=L, R=R, prompt=prompt),
        grid=(Bp, T // tm),
        in_specs=[row_spec, second,
                  pl.BlockSpec((CONV_K, D_MODEL), lambda b, t: (0, 0)),
                  _vec_spec(D_MODEL), _vec_spec(D_MODEL), _vec_spec(D_MODEL),
                  pl.BlockSpec((D_MODEL, D_MODEL), lambda b, t: (0, 0), pipeline_mode=pl.Buffered(1)),
                  _vec_spec(D_MODEL), row_spec, _vec_spec(D_MODEL), _mod_spec(S, 2)],
        out_specs=row_spec,
        out_shape=jax.ShapeDtypeStruct((Bp, T, D_MODEL), F32),
        scratch_shapes=[pltpu.VMEM((S, CONV_HIST + L, D_MODEL), F32),
                        pltpu.VMEM((CONV_K, SUBLANES, D_MODEL), F32),
                        pltpu.VMEM((R, D_MODEL), F32),
                        pltpu.VMEM((tm, D_MODEL), F32)],
        compiler_params=_cparams(2),
        name="conv_out",
    )(u, second_arg, w_dw, b_dw.reshape(1, -1), ln_g.reshape(1, -1), ln_b.reshape(1, -1),
      w_out, b_out.reshape(1, -1), x, g1, mods)


def _ffn_kernel(*refs, S, L, prompt):
    if prompt:
        (x_ref, g2_ref, sc_ref, sh_ref, wg_ref, wv_ref, wdw_ref, bdw_ref, wd_ref, g3_ref, ga_ref,
         x2_ref, st_ref, h_ref, gs_ref, vs_ref, act_ref, carry_ref) = refs
    else:
        (x_ref, g2_ref, sc_ref, sh_ref, wg_ref, wv_ref, wdw_ref, bdw_ref, wd_ref, g3_ref, ga_ref, hist_ref,
         x2_ref, st_ref, h_ref, gs_ref, vs_ref, act_ref) = refs
    t = pl.program_id(1)
    j = pl.program_id(2)
    tf = wg_ref.shape[1]
    pad = SUBLANES

    @pl.when(j == 0)
    def _():
        _norm_mod(x_ref, h_ref, g2_ref, sc_ref, sh_ref, S, L)
        x2_ref[...] = jnp.zeros(x2_ref.shape, F32)

    h = h_ref[...]
    g = _dot(h, wg_ref[...])
    vs_ref[...] = _dot(h, wv_ref[...])
    gs_ref[:, pad:pad + L, :] = g.reshape(S, L, tf)
    if prompt:
        @pl.when(t == 0)
        def _():
            gs_ref[0, 0:pad, :] = jnp.zeros((pad, tf), F32)

        @pl.when(t > 0)
        def _():
            gs_ref[0, 0:pad, :] = carry_ref[j]

        carry_ref[j] = gs_ref[0, L:L + pad, :]
    else:
        gs_ref[:, pad - (FFN_K - 1):pad, :] = hist_ref[...]
    st_ref[...] = gs_ref[:, pad + L - (FFN_K - 1):pad + L, :]

    rc = min(L, 64)
    w0 = wdw_ref[0:1, :]
    w1 = wdw_ref[1:2, :]
    w2 = wdw_ref[2:3, :]
    bd = bdw_ref[...]
    for s in range(S):

        def body(r, c, s=s):
            base = pl.multiple_of(r * rc, rc)
            win = gs_ref[s, pl.ds(base, rc + pad), :]
            cur = win[pad:, :]
            p1 = pltpu.roll(win, 1, axis=0)[pad:, :]
            p2 = pltpu.roll(win, 2, axis=0)[pad:, :]
            gc = cur * w2 + p1 * w1 + p2 * w0 + bd
            rows = pl.ds(pl.multiple_of(s * L + base, rc), rc)
            act_ref[rows, :] = (jax.nn.gelu(gc) * vs_ref[rows, :]).astype(act_ref.dtype)
            return c

        lax.fori_loop(0, L // rc, body, 0)

    x2_ref[...] += _dot(act_ref[...], wd_ref[...])

    @pl.when(j == pl.num_programs(2) - 1)
    def _():
        _residual(x2_ref, x_ref, g3_ref, ga_ref, x2_ref, S, L)


def _ffn(x, hist, mods, g2, g3, w_up, w_dw, b_dw, w_down, *, S, L, tm, tf):
    Bp, T, _ = x.shape
    J = D_FF // tf
    nT = T // tm
    prompt = hist is None
    row_spec = pl.BlockSpec((None, tm, D_MODEL), lambda b, t, j: (b, t, 0))
    in_specs = [row_spec, _vec_spec(D_MODEL), _mod_spec(S, 4), _mod_spec(S, 3),
                pl.BlockSpec((D_MODEL, tf), lambda b, t, j: (0, j)),
                pl.BlockSpec((D_MODEL, tf), lambda b, t, j: (0, j + J)),
                pl.BlockSpec((FFN_K, tf), lambda b, t, j: (0, j)),
                pl.BlockSpec((1, tf), lambda b, t, j: (0, j)),
                pl.BlockSpec((tf, D_MODEL), lambda b, t, j: (j, 0)),
                _vec_spec(D_MODEL), _mod_spec(S, 5)]
    args = [x, g2, mods, mods, w_up, w_up, w_dw, b_dw.reshape(1, -1), w_down, g3, mods]
    scratch = [pltpu.VMEM((tm, D_MODEL), BF16),
               pltpu.VMEM((S, SUBLANES + L, tf), F32),
               pltpu.VMEM((tm, tf), F32),
               pltpu.VMEM((tm, tf), BF16)]
    if prompt:
        scratch.append(pltpu.VMEM((J, SUBLANES, tf), F32))
    else:
        in_specs.append(pl.BlockSpec((S, FFN_K - 1, tf), lambda b, t, j: (0, 0, j)))
        args.append(hist)
    return pl.pallas_call(
        functools.partial(_ffn_kernel, S=S, L=L, prompt=prompt),
        grid=(Bp, nT, J),
        in_specs=in_specs,
        out_specs=[row_spec,
                   pl.BlockSpec((None, None, S, FFN_K - 1, tf), lambda b, t, j: (b, t, 0, 0, j))],
        out_shape=[jax.ShapeDtypeStruct((Bp, T, D_MODEL), F32),
                   jax.ShapeDtypeStruct((Bp, nT, S, FFN_K - 1, D_FF), F32)],
        scratch_shapes=scratch,
        compiler_params=_cparams(3),
        name="ffn",
    )(*args)


def _qkv_kernel(x_ref, g_ref, sc_ref, sh_ref, wq_ref, wk_ref, wv_ref, q_ref, k_ref, v_ref, h_ref, *, S, L):
    @pl.when(pl.program_id(2) == 0)
    def _():
        _norm_mod(x_ref, h_ref, g_ref, sc_ref, sh_ref, S, L)
        k_ref[...] = _dot(h_ref[...], wk_ref[...])
        v_ref[...] = _dot(h_ref[...], wv_ref[...])

    q_ref[...] = _dot(h_ref[...], wq_ref[...])


def _qkv(x, mods, g0, w_q, w_k, w_v, *, S, L, tm):
    Bp, T, _ = x.shape
    tn = 512
    kv_spec = pl.BlockSpec((None, tm, KV_DIM), lambda b, t, j: (b, t, 0))
    return pl.pallas_call(
        functools.partial(_qkv_kernel, S=S, L=L),
        grid=(Bp, T // tm, D_MODEL // tn),
        in_specs=[pl.BlockSpec((None, tm, D_MODEL), lambda b, t, j: (b, t, 0)),
                  _vec_spec(D_MODEL), _mod_spec(S, 1), _mod_spec(S, 0),
                  pl.BlockSpec((D_MODEL, tn), lambda b, t, j: (0, j)),
                  pl.BlockSpec((D_MODEL, KV_DIM), lambda b, t, j: (0, 0)),
                  pl.BlockSpec((D_MODEL, KV_DIM), lambda b, t, j: (0, 0))],
        out_specs=[pl.BlockSpec((None, tm, tn), lambda b, t, j: (b, t, j)), kv_spec, kv_spec],
        out_shape=[jax.ShapeDtypeStruct((Bp, T, D_MODEL), F32),
                   jax.ShapeDtypeStruct((Bp, T, KV_DIM), F32),
                   jax.ShapeDtypeStruct((Bp, T, KV_DIM), F32)],
        scratch_shapes=[pltpu.VMEM((tm, D_MODEL), BF16)],
        compiler_params=_cparams(3),
        name="qkv",
    )(x, g0, mods, mods, w_q, w_k, w_v)


def _bias_kernel(rb_ref, bucket_ref, o_ref):
    bk = bucket_ref[...]
    for h in range(N_HEADS):
        acc = jnp.zeros(bk.shape, F32)
        for b in range(N_BUCKETS):
            acc = jnp.where(bk == b, rb_ref[b, h], acc)
        o_ref[h] = acc


def _t5_bucket(rel):
    half = N_BUCKETS // 2
    max_exact = half // 2
    n = jnp.abs(rel)
    ret = jnp.where(rel > 0, half, 0)
    nf = jnp.maximum(n, 1).astype(F32)
    large = max_exact + (jnp.log(nf / max_exact) / math.log(MAX_DISTANCE / max_exact)
                         * (half - max_exact)).astype(jnp.int32)
    large = jnp.minimum(large, half - 1)
    return ret + jnp.where(n < max_exact, n, large)


def _bias_table(rel_bias):
    K = WINDOW + CHUNK
    rel = jnp.arange(K)[None, :] - WINDOW - jnp.arange(CHUNK)[:, None]
    bucket = _t5_bucket(rel).astype(jnp.int32)
    return pl.pallas_call(
        _bias_kernel,
        in_specs=[pl.BlockSpec(memory_space=pltpu.SMEM),
                  pl.BlockSpec((CHUNK, K), lambda: (0, 0))],
        out_specs=pl.BlockSpec((N_HEADS, CHUNK, K), lambda: (0, 0, 0)),
        out_shape=jax.ShapeDtypeStruct((N_HEADS, CHUNK, K), F32),
        name="rel_bias_table",
    )(rel_bias, bucket)


def _attend(q, kw, vw, bias_ref, sink_ref, valid):
    Q = q.shape[0]
    outs = []
    for h in range(N_KV_HEADS):
        qh = q[:, h * GROUP * HEAD_DIM:(h + 1) * GROUP * HEAD_DIM]
        qs = jnp.concatenate([qh[:, g * HEAD_DIM:(g + 1) * HEAD_DIM] for g in range(GROUP)], axis=0)
        kh = kw[:, h * HEAD_DIM:(h + 1) * HEAD_DIM]
        vh = vw[:, h * HEAD_DIM:(h + 1) * HEAD_DIM]
        s = lax.dot_general(qs, kh, (((1,), (1,)), ((), ())), preferred_element_type=F32)
        s = s * ATTN_SCALE + bias_ref[h]
        if valid is not None:
            s = jnp.where(valid, s, NEG_INF)
        sk = sink_ref[h]
        mx = jnp.maximum(jnp.max(s, axis=-1, keepdims=True), sk)
        p = jnp.exp(s - mx)
        den = jnp.sum(p, axis=-1, keepdims=True) + jnp.exp(sk - mx)
        oh = _dot(p, vh) * (1.0 / den)
        outs.append(jnp.concatenate([oh[g * Q:(g + 1) * Q, :] for g in range(GROUP)], axis=1))
    return jnp.concatenate(outs, axis=1)


def _attn_prompt_kernel(q_ref, kp_ref, kc_ref, vp_ref, vc_ref, bias_ref, sink_ref, o_ref, kw_ref, vw_ref, *, tq):
    t = pl.program_id(1)
    kw_ref[0:WINDOW, :] = kp_ref[...]
    kw_ref[WINDOW:WINDOW + tq, :] = kc_ref[...]
    vw_ref[0:WINDOW, :] = vp_ref[...]
    vw_ref[WINDOW:WINDOW + tq, :] = vc_ref[...]
    K = WINDOW + CHUNK

    def body(c, carry):
        r0 = pl.multiple_of(c * CHUNK, CHUNK)
        q = q_ref[pl.ds(r0, CHUNK), :]
        kw = kw_ref[pl.ds(r0, K), :]
        vw = vw_ref[pl.ds(r0, K), :]
        pos = t * tq + r0 - WINDOW + lax.broadcasted_iota(jnp.int32, (1, K), 1)
        o_ref[pl.ds(r0, CHUNK), :] = _attend(q, kw, vw, bias_ref, sink_ref, pos >= 0)
        return carry

    lax.fori_loop(0, tq // CHUNK, body, 0)


def _attn_sample_kernel(q_ref, ck_ref, kn_ref, cv_ref, vn_ref, bias_ref, sink_ref, o_ref, kw_ref, vw_ref, *, S, L):
    P = ck_ref.shape[1]
    kw_ref[:, 0:P, :] = ck_ref[...]
    kw_ref[:, P:P + L, :] = kn_ref[...].reshape(S, L, KV_DIM)
    vw_ref[:, 0:P, :] = cv_ref[...]
    vw_ref[:, P:P + L, :] = vn_ref[...].reshape(S, L, KV_DIM)

    def body(s, carry):
        r0 = pl.multiple_of(s * L, L)
        q = q_ref[pl.ds(r0, L), :]
        o_ref[pl.ds(r0, L), :] = _attend(q, kw_ref[s], vw_ref[s], bias_ref, sink_ref, None)
        return carry

    lax.fori_loop(0, S, body, 0)


def _attn_prompt(q, k, v, bias, sink, *, tq):
    Bp, T, _ = q.shape
    K = WINDOW + CHUNK
    per = tq // WINDOW
    row_spec = pl.BlockSpec((None, tq, D_MODEL), lambda b, t: (b, t, 0))
    prev_spec = pl.BlockSpec((None, WINDOW, KV_DIM), lambda b, t: (b, jnp.maximum(t * per - 1, 0), 0))
    cur_spec = pl.BlockSpec((None, tq, KV_DIM), lambda b, t: (b, t, 0))
    return pl.pallas_call(
        functools.partial(_attn_prompt_kernel, tq=tq),
        grid=(Bp, T // tq),
        in_specs=[row_spec, prev_spec, cur_spec, prev_spec, cur_spec,
                  pl.BlockSpec((N_KV_HEADS, GROUP * CHUNK, K), lambda b, t: (0, 0, 0)),
                  pl.BlockSpec((N_KV_HEADS, GROUP * CHUNK, 1), lambda b, t: (0, 0, 0))],
        out_specs=row_spec,
        out_shape=jax.ShapeDtypeStruct((Bp, T, D_MODEL), F32),
        scratch_shapes=[pltpu.VMEM((WINDOW + tq, KV_DIM), F32), pltpu.VMEM((WINDOW + tq, KV_DIM), F32)],
        compiler_params=_cparams(2),
        name="attn_prompt",
    )(q, k, k, v, v, bias, sink)


def _attn_sample(q, k, v, cache_k, cache_v, bias, sink, *, S, L):
    P = cache_k.shape[1]
    K = P + L
    full = lambda shape: pl.BlockSpec(shape, lambda: (0,) * len(shape))
    return pl.pallas_call(
        functools.partial(_attn_sample_kernel, S=S, L=L),
        in_specs=[full((S * L, D_MODEL)), full((S, P, KV_DIM)), full((S * L, KV_DIM)),
                  full((S, P, KV_DIM)), full((S * L, KV_DIM)),
                  full((N_KV_HEADS, GROUP * L, K)), full((N_KV_HEADS, GROUP * L, 1))],
        out_specs=full((S * L, D_MODEL)),
        out_shape=jax.ShapeDtypeStruct((S * L, D_MODEL), F32),
        scratch_shapes=[pltpu.VMEM((S, K, KV_DIM), F32), pltpu.VMEM((S, K, KV_DIM), F32)],
        compiler_params=pltpu.CompilerParams(vmem_limit_bytes=VMEM_LIMIT),
        name="attn_sample",
    )(q, cache_k, k, cache_v, v, bias, sink)


def _attn_out_kernel(o_ref, w_ref, x_ref, g_ref, ga_ref, x1_ref, *, S, L):
    x1_ref[...] = _dot(o_ref[...], w_ref[...])
    _residual(x1_ref, x_ref, g_ref, ga_ref, x1_ref, S, L)


def _attn_out(o, x, mods, g1, w_o, *, S, L, tm):
    Bp, T, _ = x.shape
    row_spec = pl.BlockSpec((None, tm, D_MODEL), lambda b, t: (b, t, 0))
    return pl.pallas_call(
        functools.partial(_attn_out_kernel, S=S, L=L),
        grid=(Bp, T // tm),
        in_specs=[row_spec,
                  pl.BlockSpec((D_MODEL, D_MODEL), lambda b, t: (0, 0), pipeline_mode=pl.Buffered(1)),
                  row_spec, _vec_spec(D_MODEL), _mod_spec(S, 2)],
        out_specs=row_spec,
        out_shape=jax.ShapeDtypeStruct((Bp, T, D_MODEL), F32),
        compiler_params=_cparams(2),
        name="attn_out",
    )(o, w_o, x, g1, mods)


def kernel(x_prompt, x_sample, c_prompt, c_sample, cache_conv, cache_k, cache_v, cache_ffn, w_mod, b_mod, g_norm, conv_w_in, conv_b_in, conv_w_dw, conv_b_dw, conv_ln_g, conv_ln_b, conv_w_out, conv_b_out, attn_w_q, attn_w_k, attn_w_v, attn_w_o, attn_sinks, rel_bias, ffn_w_up, ffn_w_dw, ffn_b_dw, ffn_w_down):
    B, T, D = x_prompt.shape
    SB, SL, _ = x_sample.shape
    n_c = B + SB
    c_rows = -(-n_c // SUBLANES) * SUBLANES
    c_all = jnp.concatenate([c_prompt, c_sample, jnp.zeros((c_rows - n_c, D), F32)], axis=0)
    mod = _modulation(c_all, w_mod, b_mod)
    mods_p = [mod[i, :B].reshape(B, 6, 1, D) for i in range(DEPTH)]
    mods_s = [mod[i, B:n_c].reshape(SB, 6, D).transpose(1, 0, 2) for i in range(DEPTH)]

    geo_p = dict(S=1, L=1024, tm=1024)
    geo_p_small = dict(S=1, L=256, tm=256)
    geo_s = dict(S=SB, L=SL, tm=SB * SL)
    gn = lambda i, k: g_norm[i, k].reshape(1, D)

    xp = x_prompt
    xs = x_sample.reshape(1, SB * SL, D)

    up = _conv_in(xp, mods_p[0], gn(0, 0), conv_w_in[0], conv_b_in[0], **geo_p)
    us = _conv_in(xs, mods_s[0], gn(0, 0), conv_w_in[0], conv_b_in[0], **geo_s)
    conv_args = (conv_w_dw[0], conv_b_dw[0], conv_ln_g[0], conv_ln_b[0], conv_w_out[0], conv_b_out[0])
    xp = _conv_out(up, None, xp, mods_p[0], gn(0, 1), *conv_args, R=64, **geo_p_small)
    xs = _conv_out(us, cache_conv[0], xs, mods_s[0], gn(0, 1), *conv_args, R=SL, **geo_s)
    conv_state_p = up[:, T - (CONV_K - 1):, :][None]
    conv_state_s = us.reshape(SB, SL, D)[:, SL - (CONV_K - 1):, :][None]

    ffn_p, ffn_s = [], []

    def run_ffn(i, xp, xs):
        w = (ffn_w_up[i], ffn_w_dw[i], ffn_b_dw[i], ffn_w_down[i])
        xp, st_p = _ffn(xp, None, mods_p[i], gn(i, 2), gn(i, 3), *w, tf=256, **geo_p)
        xs, st_s = _ffn(xs, cache_ffn[i], mods_s[i], gn(i, 2), gn(i, 3), *w, tf=256, **geo_s)
        ffn_p.append(st_p[:, -1, 0])
        ffn_s.append(st_s[0, 0])
        return xp, xs

    xp, xs = run_ffn(0, xp, xs)

    qp, kp, vp = _qkv(xp, mods_p[1], gn(1, 0), attn_w_q[0], attn_w_k[0], attn_w_v[0], **geo_p)
    qs, ks, vs = _qkv(xs, mods_s[1], gn(1, 0), attn_w_q[0], attn_w_k[0], attn_w_v[0], **geo_s)
    P = cache_k.shape[2]
    table = _bias_table(rel_bias)
    bias_p = table.reshape(N_KV_HEADS, GROUP * CHUNK, WINDOW + CHUNK)
    bias_s = table[:, :SL, :P + SL].reshape(N_KV_HEADS, GROUP * SL, P + SL)
    sink_p = jnp.repeat(attn_sinks[0], CHUNK).reshape(N_KV_HEADS, GROUP * CHUNK, 1)
    sink_s = jnp.repeat(attn_sinks[0], SL).reshape(N_KV_HEADS, GROUP * SL, 1)
    op = _attn_prompt(qp, kp, vp, bias_p, sink_p, tq=256)
    ck = cache_k[0].reshape(SB, P, KV_DIM)
    cv = cache_v[0].reshape(SB, P, KV_DIM)
    os_ = _attn_sample(qs[0], ks[0], vs[0], ck, cv, bias_s, sink_s, S=SB, L=SL)[None]
    xp = _attn_out(op, xp, mods_p[1], gn(1, 1), attn_w_o[0], **geo_p_small)
    xs = _attn_out(os_, xs, mods_s[1], gn(1, 1), attn_w_o[0], **geo_s)
    keep = min(WINDOW, T)
    k_state_p = kp[:, T - keep:].reshape(B, keep, N_KV_HEADS, HEAD_DIM)[None]
    v_state_p = vp[:, T - keep:].reshape(B, keep, N_KV_HEADS, HEAD_DIM)[None]
    k_state_s = jnp.concatenate([ck, ks.reshape(SB, SL, KV_DIM)], axis=1)[:, SL:]
    v_state_s = jnp.concatenate([cv, vs.reshape(SB, SL, KV_DIM)], axis=1)[:, SL:]
    k_state_s = k_state_s.reshape(SB, P, N_KV_HEADS, HEAD_DIM)[None]
    v_state_s = v_state_s.reshape(SB, P, N_KV_HEADS, HEAD_DIM)[None]

    xp, xs = run_ffn(1, xp, xs)

    return (xp, xs.reshape(SB, SL, D), conv_state_p, conv_state_s,
            k_state_p, v_state_p, k_state_s, v_state_s,
            jnp.stack(ffn_p), jnp.stack(ffn_s))
```

```python
import functools
import math

import jax
import jax.numpy as jnp
from jax import lax
from jax.experimental import pallas as pl
from jax.experimental.pallas import tpu as pltpu

F32 = jnp.float32
BF16 = jnp.bfloat16

D_MODEL = 2048
D_FF = 5632
DEPTH = 2
CONV_K = 31
FFN_K = 3
N_HEADS = 32
N_KV_HEADS = 4
GROUP = N_HEADS // N_KV_HEADS
HEAD_DIM = 64
KV_DIM = N_KV_HEADS * HEAD_DIM
CHUNK = 64
WINDOW = 128
N_BUCKETS = 32
MAX_DISTANCE = 128
EPS = 1e-6
NEG_INF = -1e30
ATTN_SCALE = HEAD_DIM ** -0.5

SUBLANES = 8
LANES = 128
CONV_HIST = 32
KEY_WIN = 2 * WINDOW
VMEM_LIMIT = 58 * 1024 * 1024
MOD_ROWS_SAMPLE = 16


def _cparams(n_grid):
    return pltpu.CompilerParams(dimension_semantics=("arbitrary",) * n_grid,
                                vmem_limit_bytes=VMEM_LIMIT)


def _dot(a, b):
    return jnp.dot(a, b, preferred_element_type=F32)


ROW_SLAB = 64


def _row_loop(S, L, fn):
    slab = min(L, ROW_SLAB)
    for s in range(S):

        def body(r, c, s=s):
            fn(s, pl.ds(pl.multiple_of(s * L + r * slab, slab), slab))
            return c

        lax.fori_loop(0, L // slab, body, 0)


def _norm_mod(x_ref, dst_ref, g_ref, sc_ref, sh_ref, S, L):
    def fn(s, rows):
        x = x_ref[rows, :]
        inv = lax.rsqrt(jnp.mean(x * x, axis=-1, keepdims=True) + EPS)
        mul = g_ref[...] * (1.0 + sc_ref[s:s + 1, :])
        dst_ref[rows, :] = (x * inv * mul + sh_ref[s:s + 1, :]).astype(dst_ref.dtype)

    _row_loop(S, L, fn)


def _residual(m_ref, x_ref, g_ref, ga_ref, out_ref, S, L):
    def fn(s, rows):
        m = m_ref[rows, :]
        inv = lax.rsqrt(jnp.mean(m * m, axis=-1, keepdims=True) + EPS)
        out_ref[rows, :] = x_ref[rows, :] + m * inv * (g_ref[...] * ga_ref[s:s + 1, :])

    _row_loop(S, L, fn)


def _row_spec(tm):
    return pl.BlockSpec((None, tm, D_MODEL), lambda b, t, *_: (b, t, 0))


def _vec_spec(index, n=D_MODEL):
    return pl.BlockSpec((None, 1, n), lambda *_: (index, 0, 0))


def _mod_spec(S, layer, which):
    if S == 1:
        first = MOD_ROWS_SAMPLE // SUBLANES
        return pl.BlockSpec((None, None, SUBLANES, D_MODEL), lambda b, t, *_: (layer, which, first + b, 0))
    return pl.BlockSpec((None, None, S, D_MODEL), lambda b, t, *_: (layer, which, 0, 0))


def _mod_kernel(c_ref, w_ref, b_ref, o_ref):
    c = c_ref[...]
    o_ref[...] = _dot(c * jax.nn.sigmoid(c), w_ref[...]) + b_ref[...]


def _modulation(c_all, w_mod, b_mod):
    rows = c_all.shape[0]
    tn = 1024
    per = D_MODEL // tn
    return pl.pallas_call(
        _mod_kernel,
        grid=(DEPTH, 6 * per),
        in_specs=[pl.BlockSpec((rows, D_MODEL), lambda i, j: (0, 0)),
                  pl.BlockSpec((None, D_MODEL, tn), lambda i, j: (i, 0, j)),
                  pl.BlockSpec((None, 1, tn), lambda i, j: (i, 0, j))],
        out_specs=pl.BlockSpec((None, None, rows, tn), lambda i, j: (i, j // per, 0, j % per)),
        out_shape=jax.ShapeDtypeStruct((DEPTH, 6, rows, D_MODEL), F32),
        compiler_params=_cparams(2),
        name="modulation",
    )(c_all, w_mod, b_mod.reshape(DEPTH, 1, 6 * D_MODEL))


def _conv_in_kernel(x_ref, g_ref, sc_ref, sh_ref, wa_ref, wg_ref, ba_ref, bg_ref, u_ref, h_ref, *, S, L):
    @pl.when(pl.program_id(2) == 0)
    def _():
        _norm_mod(x_ref, h_ref, g_ref, sc_ref, sh_ref, S, L)

    h = h_ref[...]
    a = _dot(h, wa_ref[...]) + ba_ref[...]
    g = _dot(h, wg_ref[...]) + bg_ref[...]
    u_ref[...] = a * jax.nn.sigmoid(g)


def _conv_in(x, mod, g_norm, w_in, b_in, *, layer, S, L, tm):
    Bp, T, _ = x.shape
    tn = 512
    J = D_MODEL // tn
    b2 = b_in.reshape(-1, 1, 2 * D_MODEL)
    return pl.pallas_call(
        functools.partial(_conv_in_kernel, S=S, L=L),
        grid=(Bp, T // tm, J),
        in_specs=[_row_spec(tm), _vec_spec(4 * layer + 0), _mod_spec(S, layer, 1), _mod_spec(S, layer, 0),
                  pl.BlockSpec((None, D_MODEL, tn), lambda b, t, j: (0, 0, j)),
                  pl.BlockSpec((None, D_MODEL, tn), lambda b, t, j: (0, 0, j + J)),
                  pl.BlockSpec((None, 1, tn), lambda b, t, j: (0, 0, j)),
                  pl.BlockSpec((None, 1, tn), lambda b, t, j: (0, 0, j + J))],
        out_specs=pl.BlockSpec((None, tm, tn), lambda b, t, j: (b, t, j)),
        out_shape=jax.ShapeDtypeStruct((Bp, T, D_MODEL), F32),
        scratch_shapes=[pltpu.VMEM((tm, D_MODEL), BF16)],
        compiler_params=_cparams(3),
        name="conv_in",
    )(x, g_norm, mod, mod, w_in, w_in, b2, b2)


def _dwconv(full_ref, wb_ref, bdw_ref, y_ref, S, L, R):
    n_groups = (CONV_K + 1 + SUBLANES) // SUBLANES
    win_rows = R + CONV_HIST

    def body(cb, c):
        cols = pl.ds(pl.multiple_of(cb * LANES, LANES), LANES)
        for s in range(S):
            for r in range(L // R):
                win = full_ref[s, r * R:r * R + win_rows, cols]
                acc = jnp.broadcast_to(bdw_ref[:, cols], (R // SUBLANES, SUBLANES, LANES))
                for sh in range(SUBLANES):
                    wsh = win if sh == 0 else pltpu.roll(win, win_rows - sh, axis=0)
                    for a in range(n_groups):
                        k = SUBLANES * a + sh - 2
                        if 0 <= k < CONV_K:
                            tap = wsh[SUBLANES * a:SUBLANES * a + R, :]
                            acc = acc + tap.reshape(R // SUBLANES, SUBLANES, LANES) * wb_ref[k, :, cols]
                y_ref[s * L + r * R:s * L + (r + 1) * R, cols] = acc.reshape(R, LANES)
        return c

    lax.fori_loop(0, D_MODEL // LANES, body, 0)


def _ln_silu(y_ref, lng_ref, lnb_ref, S, L):
    def fn(s, rows):
        y = y_ref[rows, :]
        mu = jnp.mean(y, axis=-1, keepdims=True)
        yc = y - mu
        var = jnp.mean(yc * yc, axis=-1, keepdims=True)
        z = yc * lax.rsqrt(var + EPS) * lng_ref[...] + lnb_ref[...]
        y_ref[rows, :] = z * jax.nn.sigmoid(z)

    _row_loop(S, L, fn)


def _conv_out_kernel(u_ref, hist_ref, wdw_ref, bdw_ref, lng_ref, lnb_ref, wout_ref, bout_ref, x_ref, g_ref, ga_ref,
                     x1_ref, full_ref, wb_ref, y_ref, *, S, L, R, prompt):
    t = pl.program_id(1)

    @pl.when((pl.program_id(0) == 0) & (t == 0))
    def _():
        for k in range(CONV_K):
            wb_ref[k] = jnp.broadcast_to(wdw_ref[k:k + 1, :], (SUBLANES, D_MODEL))

    if prompt:
        @pl.when(t == 0)
        def _():
            full_ref[0, 0:CONV_HIST, :] = jnp.zeros((CONV_HIST, D_MODEL), F32)

        @pl.when(t > 0)
        def _():
            full_ref[0, 0:CONV_HIST, :] = hist_ref[...]

        full_ref[0, CONV_HIST:CONV_HIST + L, :] = u_ref[...]
    else:
        pad = CONV_HIST - (CONV_K - 1)
        full_ref[:, 0:pad, :] = jnp.zeros((S, pad, D_MODEL), F32)
        full_ref[:, pad:CONV_HIST, :] = hist_ref[...]
        full_ref[:, CONV_HIST:CONV_HIST + L, :] = u_ref[...].reshape(S, L, D_MODEL)

    _dwconv(full_ref, wb_ref, bdw_ref, y_ref, S, L, R)
    _ln_silu(y_ref, lng_ref, lnb_ref, S, L)
    x1_ref[...] = _dot(y_ref[...], wout_ref[...]) + bout_ref[...]
    _residual(x1_ref, x_ref, g_ref, ga_ref, x1_ref, S, L)


def _conv_out(u, hist, x, mod, g_norm, w_dw, b_dw, ln_g, ln_b, w_out, b_out, *, layer, S, L, tm, R):
    Bp, T, _ = x.shape
    prompt = hist is None
    if prompt:
        per = tm // CONV_HIST
        second = pl.BlockSpec((None, CONV_HIST, D_MODEL), lambda b, t: (b, jnp.maximum(t * per - 1, 0), 0))
        second_arg = u
    else:
        second = pl.BlockSpec((S, CONV_K - 1, D_MODEL), lambda b, t: (0, 0, 0))
        second_arg = hist
    vec = lambda a: a.reshape(-1, 1, D_MODEL)
    return pl.pallas_call(
        functools.partial(_conv_out_kernel, S=S, L=L, R=R, prompt=prompt),
        grid=(Bp, T // tm),
        in_specs=[_row_spec(tm), second,
                  pl.BlockSpec((None, CONV_K, D_MODEL), lambda b, t: (0, 0, 0)),
                  _vec_spec(0), _vec_spec(0), _vec_spec(0),
                  pl.BlockSpec((None, D_MODEL, D_MODEL), lambda b, t: (0, 0, 0), pipeline_mode=pl.Buffered(1)),
                  _vec_spec(0), _row_spec(tm), _vec_spec(4 * layer + 1), _mod_spec(S, layer, 2)],
        out_specs=_row_spec(tm),
        out_shape=jax.ShapeDtypeStruct((Bp, T, D_MODEL), F32),
        scratch_shapes=[pltpu.VMEM((S, CONV_HIST + L, D_MODEL), F32),
                        pltpu.VMEM((CONV_K, SUBLANES, D_MODEL), F32),
                        pltpu.VMEM((tm, D_MODEL), F32)],
        compiler_params=_cparams(2),
        name="conv_out",
    )(u, second_arg, w_dw, vec(b_dw), vec(ln_g), vec(ln_b), w_out, vec(b_out), x, g_norm, mod)


def _ffn_kernel(*refs, S, L, nb, prompt):
    if prompt:
        (x_ref, g2_ref, sc_ref, sh_ref, wg_ref, wv_ref, wdw_ref, bdw_ref, wd_ref, g3_ref, ga_ref,
         x2_ref, st_ref, h_ref, carry_ref) = refs
    else:
        (x_ref, g2_ref, sc_ref, sh_ref, wg_ref, wv_ref, wdw_ref, bdw_ref, wd_ref, g3_ref, ga_ref, hist_ref,
         x2_ref, st_ref, h_ref) = refs
    t = pl.program_id(1)
    j = pl.program_id(2)
    tm = x_ref.shape[0]
    rb = tm // nb
    seg = min(L, rb)
    pad = SUBLANES

    @pl.when(j == 0)
    def _():
        _norm_mod(x_ref, h_ref, g2_ref, sc_ref, sh_ref, S, L)
        x2_ref[...] = jnp.zeros(x2_ref.shape, F32)

    wg = wg_ref[...].astype(BF16)
    wv = wv_ref[...].astype(BF16)
    wd = wd_ref[...].astype(BF16)
    w0 = wdw_ref[0:1, :]
    w1 = wdw_ref[1:2, :]
    w2 = wdw_ref[2:3, :]
    bd = bdw_ref[...]
    tail = None
    for blk in range(nb):
        rows = slice(blk * rb, (blk + 1) * rb)
        hb = h_ref[rows, :]
        g = _dot(hb, wg)
        v = _dot(hb, wv)
        p1s, p2s = [], []
        for q in range(rb // seg):
            gq = g[q * seg:(q + 1) * seg, :]
            if not prompt:
                s = blk * (rb // seg) + q
                hist = hist_ref[s]
                st_ref[s] = gq[seg - (FFN_K - 1):, :]
            elif blk == 0:
                hist = jnp.where(t == 0, 0.0, carry_ref[j])
            else:
                hist = tail
            gfull = jnp.concatenate([hist, gq], axis=0)
            p1s.append(pltpu.roll(gfull, 1, axis=0)[pad:, :])
            p2s.append(pltpu.roll(gfull, 2, axis=0)[pad:, :])
            tail = gq[seg - pad:, :]
        p1 = p1s[0] if len(p1s) == 1 else jnp.concatenate(p1s, axis=0)
        p2 = p2s[0] if len(p2s) == 1 else jnp.concatenate(p2s, axis=0)
        gc = g * w2 + p1 * w1 + p2 * w0 + bd
        act = (jax.nn.gelu(gc) * v).astype(BF16)
        x2_ref[rows, :] += _dot(act, wd)
    if prompt:
        carry_ref[j] = tail
        st_ref[0] = tail[pad - (FFN_K - 1):, :]

    @pl.when(j == pl.num_programs(2) - 1)
    def _():
        _residual(x2_ref, x_ref, g3_ref, ga_ref, x2_ref, S, L)


def _ffn(x, hist, mod, g_norm, w_up, w_dw, b_dw, w_down, *, layer, S, L, tm, tf, nb):
    Bp, T, _ = x.shape
    J = D_FF // tf
    nT = T // tm
    prompt = hist is None
    in_specs = [_row_spec(tm), _vec_spec(4 * layer + 2), _mod_spec(S, layer, 4), _mod_spec(S, layer, 3),
                pl.BlockSpec((None, D_MODEL, tf), lambda b, t, j: (layer, 0, j)),
                pl.BlockSpec((None, D_MODEL, tf), lambda b, t, j: (layer, 0, j + J)),
                pl.BlockSpec((None, FFN_K, tf), lambda b, t, j: (layer, 0, j)),
                pl.BlockSpec((None, 1, tf), lambda b, t, j: (layer, 0, j)),
                pl.BlockSpec((None, tf, D_MODEL), lambda b, t, j: (layer, j, 0)),
                _vec_spec(4 * layer + 3), _mod_spec(S, layer, 5)]
    args = [x, g_norm, mod, mod, w_up, w_up, w_dw, b_dw.reshape(DEPTH, 1, D_FF), w_down, g_norm, mod]
    scratch = [pltpu.VMEM((tm, D_MODEL), BF16)]
    if prompt:
        scratch.append(pltpu.VMEM((J, SUBLANES, tf), F32))
    else:
        in_specs.append(pl.BlockSpec((S, SUBLANES, tf), lambda b, t, j: (0, 0, j)))
        args.append(hist)
    return pl.pallas_call(
        functools.partial(_ffn_kernel, S=S, L=L, nb=nb, prompt=prompt),
        grid=(Bp, nT, J),
        in_specs=in_specs,
        out_specs=[_row_spec(tm),
                   pl.BlockSpec((None, None, S, FFN_K - 1, tf), lambda b, t, j: (b, t, 0, 0, j))],
        out_shape=[jax.ShapeDtypeStruct((Bp, T, D_MODEL), F32),
                   jax.ShapeDtypeStruct((Bp, nT, S, FFN_K - 1, D_FF), F32)],
        scratch_shapes=scratch,
        compiler_params=_cparams(3),
        name="ffn",
    )(*args)


def _qkv_kernel(x_ref, g_ref, sc_ref, sh_ref, wq_ref, wk_ref, wv_ref, q_ref, k_ref, v_ref, h_ref, *, S, L):
    @pl.when(pl.program_id(2) == 0)
    def _():
        _norm_mod(x_ref, h_ref, g_ref, sc_ref, sh_ref, S, L)
        k_ref[...] = _dot(h_ref[...], wk_ref[...])
        v_ref[...] = _dot(h_ref[...], wv_ref[...])

    q_ref[...] = _dot(h_ref[...], wq_ref[...])


def _qkv(x, mod, g_norm, w_q, w_k, w_v, *, layer, S, L, tm):
    Bp, T, _ = x.shape
    tn = 512
    kv_spec = pl.BlockSpec((None, tm, KV_DIM), lambda b, t, j: (b, t, 0))
    return pl.pallas_call(
        functools.partial(_qkv_kernel, S=S, L=L),
        grid=(Bp, T // tm, D_MODEL // tn),
        in_specs=[_row_spec(tm), _vec_spec(4 * layer + 0), _mod_spec(S, layer, 1), _mod_spec(S, layer, 0),
                  pl.BlockSpec((None, D_MODEL, tn), lambda b, t, j: (0, 0, j)),
                  pl.BlockSpec((None, D_MODEL, KV_DIM), lambda b, t, j: (0, 0, 0)),
                  pl.BlockSpec((None, D_MODEL, KV_DIM), lambda b, t, j: (0, 0, 0))],
        out_specs=[pl.BlockSpec((None, tm, tn), lambda b, t, j: (b, t, j)), kv_spec, kv_spec],
        out_shape=[jax.ShapeDtypeStruct((Bp, T, D_MODEL), F32),
                   jax.ShapeDtypeStruct((Bp, T, KV_DIM), F32),
                   jax.ShapeDtypeStruct((Bp, T, KV_DIM), F32)],
        scratch_shapes=[pltpu.VMEM((tm, D_MODEL), BF16)],
        compiler_params=_cparams(3),
        name="qkv",
    )(x, g_norm, mod, mod, w_q, w_k, w_v)


def _bias_kernel(rb_ref, bucket_ref, o_ref):
    bk = bucket_ref[...]
    for h in range(N_HEADS):
        acc = jnp.full(bk.shape, NEG_INF, F32)
        for b in range(N_BUCKETS):
            acc = jnp.where(bk == b, rb_ref[b, h], acc)
        o_ref[h] = acc


def _t5_bucket(rel):
    half = N_BUCKETS // 2
    max_exact = half // 2
    n = jnp.abs(rel)
    ret = jnp.where(rel > 0, half, 0)
    nf = jnp.maximum(n, 1).astype(F32)
    large = max_exact + (jnp.log(nf / max_exact) / math.log(MAX_DISTANCE / max_exact)
                         * (half - max_exact)).astype(jnp.int32)
    large = jnp.minimum(large, half - 1)
    return ret + jnp.where(n < max_exact, n, large)


def _bias_table(rel_bias, Qn, lo, hi):
    k = jnp.arange(KEY_WIN)[:, None]
    q = jnp.arange(Qn)[None, :]
    bucket = jnp.where((k >= lo(q)) & (k < hi(q)), _t5_bucket(k - WINDOW - q), -1).astype(jnp.int32)
    table = pl.pallas_call(
        _bias_kernel,
        in_specs=[pl.BlockSpec(memory_space=pltpu.SMEM),
                  pl.BlockSpec((KEY_WIN, Qn), lambda: (0, 0))],
        out_specs=pl.BlockSpec((N_HEADS, KEY_WIN, Qn), lambda: (0, 0, 0)),
        out_shape=jax.ShapeDtypeStruct((N_HEADS, KEY_WIN, Qn), F32),
        name="rel_bias_table",
    )(rel_bias, bucket)
    table = table.reshape(N_KV_HEADS, GROUP, KEY_WIN, Qn).transpose(0, 2, 1, 3)
    return table.reshape(N_KV_HEADS, KEY_WIN, GROUP * Qn)


def _attend(q_ref, r0, Qn, kw, vw, bias_ref, sink_ref, valid, o_ref, qs_ref):
    rows = pl.ds(r0, Qn)
    for g in range(GROUP):
        for h in range(N_KV_HEADS):
            src = (h * GROUP + g) * HEAD_DIM
            qs_ref[g * Qn:(g + 1) * Qn, h * HEAD_DIM:(h + 1) * HEAD_DIM] = q_ref[rows, src:src + HEAD_DIM]
    qt = qs_ref[...].T
    vt = vw.T
    outs = []
    for h in range(N_KV_HEADS):
        hd = slice(h * HEAD_DIM, (h + 1) * HEAD_DIM)
        s = _dot(kw[:, hd], qt[hd, :]) * ATTN_SCALE + bias_ref[h]
        if valid is not None:
            s = jnp.where(valid, s, NEG_INF)
        sk = sink_ref[h]
        mx = jnp.maximum(jnp.max(s, axis=0, keepdims=True), sk)
        p = jnp.exp(s - mx)
        den = jnp.sum(p, axis=0, keepdims=True) + jnp.exp(sk - mx)
        outs.append(_dot(vt[hd, :], p) * (1.0 / den))
    o2 = jnp.concatenate(outs, axis=0).T
    for g in range(GROUP):
        for h in range(N_KV_HEADS):
            dst = (h * GROUP + g) * HEAD_DIM
            o_ref[rows, dst:dst + HEAD_DIM] = o2[g * Qn:(g + 1) * Qn, h * HEAD_DIM:(h + 1) * HEAD_DIM]


def _attn_prompt_kernel(q_ref, kp_ref, kc_ref, vp_ref, vc_ref, bias_ref, sink_ref, o_ref, kw_ref, vw_ref, qs_ref,
                        *, tq, Qn):
    t = pl.program_id(1)
    kw_ref[0:WINDOW, :] = kp_ref[...]
    kw_ref[WINDOW:WINDOW + tq, :] = kc_ref[...]
    vw_ref[0:WINDOW, :] = vp_ref[...]
    vw_ref[WINDOW:WINDOW + tq, :] = vc_ref[...]

    def body(m, carry):
        r0 = pl.multiple_of(m * Qn, Qn)
        kw = kw_ref[pl.ds(r0, KEY_WIN), :]
        vw = vw_ref[pl.ds(r0, KEY_WIN), :]
        pos = t * tq + r0 - WINDOW + lax.broadcasted_iota(jnp.int32, (KEY_WIN, 1), 0)
        _attend(q_ref, r0, Qn, kw, vw, bias_ref, sink_ref, pos >= 0, o_ref, qs_ref)
        return carry

    lax.fori_loop(0, tq // Qn, body, 0)


def _attn_sample_kernel(q_ref, ck_ref, kn_ref, cv_ref, vn_ref, bias_ref, sink_ref, o_ref, kw_ref, vw_ref, qs_ref,
                        *, S, L):
    P = ck_ref.shape[1]
    kw_ref[:, 0:P, :] = ck_ref[...]
    kw_ref[:, P:P + L, :] = kn_ref[...].reshape(S, L, KV_DIM)
    kw_ref[:, P + L:, :] = jnp.zeros((S, KEY_WIN - P - L, KV_DIM), F32)
    vw_ref[:, 0:P, :] = cv_ref[...]
    vw_ref[:, P:P + L, :] = vn_ref[...].reshape(S, L, KV_DIM)
    vw_ref[:, P + L:, :] = jnp.zeros((S, KEY_WIN - P - L, KV_DIM), F32)

    def body(s, carry):
        r0 = pl.multiple_of(s * L, L)
        _attend(q_ref, r0, L, kw_ref[s], vw_ref[s], bias_ref, sink_ref, None, o_ref, qs_ref)
        return carry

    lax.fori_loop(0, S, body, 0)


def _attn_prompt(q, k, v, bias, sink, *, tq):
    Bp, T, _ = q.shape
    Qn = 2 * CHUNK
    per = tq // WINDOW
    prev_spec = pl.BlockSpec((None, WINDOW, KV_DIM), lambda b, t: (b, jnp.maximum(t * per - 1, 0), 0))
    cur_spec = pl.BlockSpec((None, tq, KV_DIM), lambda b, t: (b, t, 0))
    return pl.pallas_call(
        functools.partial(_attn_prompt_kernel, tq=tq, Qn=Qn),
        grid=(Bp, T // tq),
        in_specs=[_row_spec(tq), prev_spec, cur_spec, prev_spec, cur_spec,
                  pl.BlockSpec((N_KV_HEADS, KEY_WIN, GROUP * Qn), lambda b, t: (0, 0, 0)),
                  pl.BlockSpec((N_KV_HEADS, 1, GROUP * Qn), lambda b, t: (0, 0, 0))],
        out_specs=_row_spec(tq),
        out_shape=jax.ShapeDtypeStruct((Bp, T, D_MODEL), F32),
        scratch_shapes=[pltpu.VMEM((WINDOW + tq, KV_DIM), F32), pltpu.VMEM((WINDOW + tq, KV_DIM), F32),
                        pltpu.VMEM((GROUP * Qn, KV_DIM), F32)],
        compiler_params=_cparams(2),
        name="attn_prompt",
    )(q, k, k, v, v, bias, sink)


def _attn_sample(q, k, v, cache_k, cache_v, bias, sink, *, S, L):
    return pl.pallas_call(
        functools.partial(_attn_sample_kernel, S=S, L=L),
        out_shape=jax.ShapeDtypeStruct((S * L, D_MODEL), F32),
        scratch_shapes=[pltpu.VMEM((S, KEY_WIN, KV_DIM), F32), pltpu.VMEM((S, KEY_WIN, KV_DIM), F32),
                        pltpu.VMEM((GROUP * L, KV_DIM), F32)],
        compiler_params=pltpu.CompilerParams(vmem_limit_bytes=VMEM_LIMIT),
        name="attn_sample",
    )(q, cache_k, k, cache_v, v, bias, sink)


def _attn_out_kernel(o_ref, w_ref, x_ref, g_ref, ga_ref, x1_ref, *, S, L):
    x1_ref[...] = _dot(o_ref[...], w_ref[...])
    _residual(x1_ref, x_ref, g_ref, ga_ref, x1_ref, S, L)


def _attn_out(o, x, mod, g_norm, w_o, *, layer, S, L, tm):
    Bp, T, _ = x.shape
    return pl.pallas_call(
        functools.partial(_attn_out_kernel, S=S, L=L),
        grid=(Bp, T // tm),
        in_specs=[_row_spec(tm),
                  pl.BlockSpec((None, D_MODEL, D_MODEL), lambda b, t: (0, 0, 0), pipeline_mode=pl.Buffered(1)),
                  _row_spec(tm), _vec_spec(4 * layer + 1), _mod_spec(S, layer, 2)],
        out_specs=_row_spec(tm),
        out_shape=jax.ShapeDtypeStruct((Bp, T, D_MODEL), F32),
        compiler_params=_cparams(2),
        name="attn_out",
    )(o, w_o, x, g_norm, mod)


def kernel(x_prompt, x_sample, c_prompt, c_sample, cache_conv, cache_k, cache_v, cache_ffn, w_mod, b_mod, g_norm, conv_w_in, conv_b_in, conv_w_dw, conv_b_dw, conv_ln_g, conv_ln_b, conv_w_out, conv_b_out, attn_w_q, attn_w_k, attn_w_v, attn_w_o, attn_sinks, rel_bias, ffn_w_up, ffn_w_dw, ffn_b_dw, ffn_w_down):
    B, T, D = x_prompt.shape
    SB, SL, _ = x_sample.shape
    assert SB == MOD_ROWS_SAMPLE
    c_all = jnp.concatenate(
        [c_sample, jnp.pad(c_prompt[:, None, :], ((0, 0), (0, SUBLANES - 1), (0, 0))).reshape(B * SUBLANES, D)], axis=0)
    mod = _modulation(c_all, w_mod, b_mod)
    gn = g_norm.reshape(DEPTH * 4, 1, D)

    geo_p = dict(S=1, L=1024, tm=1024)
    geo_p_small = dict(S=1, L=256, tm=256)
    geo_s = dict(S=SB, L=SL, tm=SB * SL)

    xp = x_prompt
    xs = x_sample.reshape(1, SB * SL, D)

    up = _conv_in(xp, mod, gn, conv_w_in, conv_b_in, layer=0, **geo_p)
    us = _conv_in(xs, mod, gn, conv_w_in, conv_b_in, layer=0, **geo_s)
    conv_args = (conv_w_dw, conv_b_dw, conv_ln_g, conv_ln_b, conv_w_out, conv_b_out)
    xp = _conv_out(up, None, xp, mod, gn, *conv_args, layer=0, R=64, **geo_p_small)
    xs = _conv_out(us, cache_conv[0], xs, mod, gn, *conv_args, layer=0, R=SL, **geo_s)
    conv_state_p = up[:, T - (CONV_K - 1):, :][None]
    conv_state_s = us.reshape(SB, SL, D)[:, SL - (CONV_K - 1):, :][None]

    ffn_p, ffn_s = [], []
    ffn_hist = jnp.pad(cache_ffn, ((0, 0), (0, 0), (SUBLANES - (FFN_K - 1), 0), (0, 0)))

    def run_ffn(i, xp, xs):
        w = (ffn_w_up, ffn_w_dw, ffn_b_dw, ffn_w_down)
        xp, st_p = _ffn(xp, None, mod, gn, *w, layer=i, tf=256, nb=2, **geo_p)
        xs, st_s = _ffn(xs, ffn_hist[i], mod, gn, *w, layer=i, tf=256, nb=2, **geo_s)
        ffn_p.append(st_p[:, -1, 0])
        ffn_s.append(st_s[0, 0])
        return xp, xs

    xp, xs = run_ffn(0, xp, xs)

    qp, kp, vp = _qkv(xp, mod, gn, attn_w_q, attn_w_k, attn_w_v, layer=1, **geo_p)
    qs, ks, vs = _qkv(xs, mod, gn, attn_w_q, attn_w_k, attn_w_v, layer=1, **geo_s)
    P = cache_k.shape[2]
    assert P == WINDOW and P + SL <= KEY_WIN
    bias_p = _bias_table(rel_bias, 2 * CHUNK,
                         lambda q: (q // CHUNK) * CHUNK, lambda q: (q // CHUNK) * CHUNK + WINDOW + CHUNK)
    bias_s = _bias_table(rel_bias, SL, lambda q: 0 * q, lambda q: 0 * q + P + SL)
    sink_p = jnp.repeat(attn_sinks[0], 2 * CHUNK).reshape(N_KV_HEADS, 1, GROUP * 2 * CHUNK)
    sink_s = jnp.repeat(attn_sinks[0], SL).reshape(N_KV_HEADS, 1, GROUP * SL)
    op = _attn_prompt(qp, kp, vp, bias_p, sink_p, tq=256)
    ck = cache_k[0].reshape(SB, P, KV_DIM)
    cv = cache_v[0].reshape(SB, P, KV_DIM)
    os_ = _attn_sample(qs[0], ks[0], vs[0], ck, cv, bias_s, sink_s, S=SB, L=SL)[None]
    xp = _attn_out(op, xp, mod, gn, attn_w_o, layer=1, **geo_p_small)
    xs = _attn_out(os_, xs, mod, gn, attn_w_o, layer=1, **geo_s)
    keep = min(WINDOW, T)
    k_state_p = kp[:, T - keep:].reshape(B, keep, N_KV_HEADS, HEAD_DIM)[None]
    v_state_p = vp[:, T - keep:].reshape(B, keep, N_KV_HEADS, HEAD_DIM)[None]
    k_state_s = jnp.concatenate([ck, ks.reshape(SB, SL, KV_DIM)], axis=1)[:, SL:]
    v_state_s = jnp.concatenate([cv, vs.reshape(SB, SL, KV_DIM)], axis=1)[:, SL:]
    k_state_s = k_state_s.reshape(SB, P, N_KV_HEADS, HEAD_DIM)[None]
    v_state_s = v_state_s.reshape(SB, P, N_KV_HEADS, HEAD_DIM)[None]

    xp, xs = run_ffn(1, xp, xs)

    return (xp, xs.reshape(SB, SL, D), conv_state_p, conv_state_s,
            k_state_p, v_state_p, k_state_s, v_state_s,
            jnp.stack(ffn_p), jnp.stack(ffn_s))
```

```python
import functools
import math

import jax
import jax.numpy as jnp
from jax import lax
from jax.experimental import pallas as pl
from jax.experimental.pallas import tpu as pltpu

F32 = jnp.float32
BF16 = jnp.bfloat16

D_MODEL = 2048
D_FF = 5632
DEPTH = 2
CONV_K = 31
FFN_K = 3
N_HEADS = 32
N_KV_HEADS = 4
GROUP = N_HEADS // N_KV_HEADS
HEAD_DIM = 64
KV_DIM = N_KV_HEADS * HEAD_DIM
CHUNK = 64
WINDOW = 128
N_BUCKETS = 32
MAX_DISTANCE = 128
EPS = 1e-6
NEG_INF = -1e30
ATTN_SCALE = HEAD_DIM ** -0.5

SUBLANES = 8
LANES = 128
CONV_HIST = 32
KEY_WIN = 2 * WINDOW
VMEM_LIMIT = 58 * 1024 * 1024
MOD_ROWS_SAMPLE = 16


def _cparams(n_grid):
    return pltpu.CompilerParams(dimension_semantics=("arbitrary",) * n_grid,
                                vmem_limit_bytes=VMEM_LIMIT)


def _dot(a, b):
    return jnp.dot(a, b, preferred_element_type=F32)


ROW_SLAB = 64
ROW_GROUP = 2


def _row_loop(S, L, srcs, dst, fn):
    slab = min(L, ROW_SLAB)
    n = L // slab
    group = ROW_GROUP if n % ROW_GROUP == 0 else 1
    for s in range(S):

        def body(r, c, s=s):
            rows = [pl.ds(pl.multiple_of(s * L + (r * group + i) * slab, slab), slab) for i in range(group)]
            vals = [[src[rw, :] for src in srcs] for rw in rows]
            outs = [fn(s, *v) for v in vals]
            for rw, out in zip(rows, outs):
                dst[rw, :] = out.astype(dst.dtype)
            return c

        lax.fori_loop(0, n // group, body, 0)


def _norm_mod(x_ref, dst_ref, g_ref, sc_ref, sh_ref, S, L):
    def fn(s, x):
        inv = lax.rsqrt(jnp.mean(x * x, axis=-1, keepdims=True) + EPS)
        mul = g_ref[...] * (1.0 + sc_ref[s:s + 1, :])
        return x * inv * mul + sh_ref[s:s + 1, :]

    _row_loop(S, L, [x_ref], dst_ref, fn)


def _residual(m_ref, x_ref, g_ref, ga_ref, out_ref, S, L):
    def fn(s, m, x):
        inv = lax.rsqrt(jnp.mean(m * m, axis=-1, keepdims=True) + EPS)
        return x + m * inv * (g_ref[...] * ga_ref[s:s + 1, :])

    _row_loop(S, L, [m_ref, x_ref], out_ref, fn)


def _row_spec(tm):
    return pl.BlockSpec((None, tm, D_MODEL), lambda b, t, *_: (b, t, 0))


def _vec_spec(index, n=D_MODEL):
    return pl.BlockSpec((None, 1, n), lambda *_: (index, 0, 0))


def _mod_spec(S, layer, which):
    if S == 1:
        first = MOD_ROWS_SAMPLE // SUBLANES
        return pl.BlockSpec((None, None, SUBLANES, D_MODEL), lambda b, t, *_: (layer, which, first + b, 0))
    return pl.BlockSpec((None, None, S, D_MODEL), lambda b, t, *_: (layer, which, 0, 0))


def _mod_kernel(c_ref, w_ref, b_ref, o_ref):
    c = c_ref[...]
    o_ref[...] = _dot(c * jax.nn.sigmoid(c), w_ref[...]) + b_ref[...]


def _modulation(c_all, w_mod, b_mod):
    rows = c_all.shape[0]
    tn = 1024
    per = D_MODEL // tn
    return pl.pallas_call(
        _mod_kernel,
        grid=(DEPTH, 6 * per),
        in_specs=[pl.BlockSpec((rows, D_MODEL), lambda i, j: (0, 0)),
                  pl.BlockSpec((None, D_MODEL, tn), lambda i, j: (i, 0, j)),
                  pl.BlockSpec((None, 1, tn), lambda i, j: (i, 0, j))],
        out_specs=pl.BlockSpec((None, None, rows, tn), lambda i, j: (i, j // per, 0, j % per)),
        out_shape=jax.ShapeDtypeStruct((DEPTH, 6, rows, D_MODEL), F32),
        compiler_params=_cparams(2),
        name="modulation",
    )(c_all, w_mod, b_mod.reshape(DEPTH, 1, 6 * D_MODEL))


def _conv_in_kernel(x_ref, g_ref, sc_ref, sh_ref, wa_ref, wg_ref, ba_ref, bg_ref, u_ref, h_ref, *, S, L):
    @pl.when(pl.program_id(2) == 0)
    def _():
        _norm_mod(x_ref, h_ref, g_ref, sc_ref, sh_ref, S, L)

    h = h_ref[...]
    a = _dot(h, wa_ref[...]) + ba_ref[...]
    g = _dot(h, wg_ref[...]) + bg_ref[...]
    u_ref[...] = a * jax.nn.sigmoid(g)


def _conv_in(x, mod, g_norm, w_in, b_in, *, layer, S, L, tm):
    Bp, T, _ = x.shape
    tn = 512
    J = D_MODEL // tn
    b2 = b_in.reshape(-1, 1, 2 * D_MODEL)
    return pl.pallas_call(
        functools.partial(_conv_in_kernel, S=S, L=L),
        grid=(Bp, T // tm, J),
        in_specs=[_row_spec(tm), _vec_spec(4 * layer + 0), _mod_spec(S, layer, 1), _mod_spec(S, layer, 0),
                  pl.BlockSpec((None, D_MODEL, tn), lambda b, t, j: (0, 0, j)),
                  pl.BlockSpec((None, D_MODEL, tn), lambda b, t, j: (0, 0, j + J)),
                  pl.BlockSpec((None, 1, tn), lambda b, t, j: (0, 0, j)),
                  pl.BlockSpec((None, 1, tn), lambda b, t, j: (0, 0, j + J))],
        out_specs=pl.BlockSpec((None, tm, tn), lambda b, t, j: (b, t, j)),
        out_shape=jax.ShapeDtypeStruct((Bp, T, D_MODEL), F32),
        scratch_shapes=[pltpu.VMEM((tm, D_MODEL), BF16)],
        compiler_params=_cparams(3),
        name="conv_in",
    )(x, g_norm, mod, mod, w_in, w_in, b2, b2)


def _dwconv_block(win, wb_ref, bias, cols, R):
    n_groups = (CONV_K + 1 + SUBLANES) // SUBLANES
    win_rows = R + CONV_HIST
    acc = jnp.broadcast_to(bias, (R // SUBLANES, SUBLANES, LANES))
    for sh in range(SUBLANES):
        wsh = win if sh == 0 else pltpu.roll(win, win_rows - sh, axis=0)
        for a in range(n_groups):
            k = SUBLANES * a + sh - 2
            if 0 <= k < CONV_K:
                tap = wsh[SUBLANES * a:SUBLANES * a + R, :]
                acc = acc + tap.reshape(R // SUBLANES, SUBLANES, LANES) * wb_ref[k, :, cols]
    return acc.reshape(R, LANES)


def _fill_tap_table(wdw_ref, wb_ref):
    for k in range(CONV_K):
        wb_ref[k] = jnp.broadcast_to(wdw_ref[k:k + 1, :], (SUBLANES, D_MODEL))


def _conv_mix_kernel(x_ref, g_ref, sc_ref, sh_ref, wa_ref, wg_ref, ba_ref, bg_ref, wdw_ref, bdw_ref,
                     y_ref, tail_ref, h_ref, wb_ref, carry_ref, *, L, R, nb):
    t = pl.program_id(1)
    j = pl.program_id(2)
    tn = wa_ref.shape[1]
    rb = L // nb

    @pl.when((pl.program_id(0) == 0) & (t == 0) & (j == 0))
    def _():
        _fill_tap_table(wdw_ref, wb_ref)
        carry_ref[...] = jnp.zeros(carry_ref.shape, F32)

    @pl.when(j == 0)
    def _():
        _norm_mod(x_ref, h_ref, g_ref, sc_ref, sh_ref, 1, L)

    wa = wa_ref[...].astype(BF16)
    wg = wg_ref[...].astype(BF16)
    tail = jnp.where(t == 0, 0.0, carry_ref[j])
    for blk in range(nb):
        hb = h_ref[blk * rb:(blk + 1) * rb, :]
        a = _dot(hb, wa) + ba_ref[...]
        g = _dot(hb, wg) + bg_ref[...]
        u = a * jax.nn.sigmoid(g)
        ufull = jnp.concatenate([tail, u], axis=0)
        for cb in range(tn // LANES):
            gcols = pl.ds(pl.multiple_of(j * tn + cb * LANES, LANES), LANES)
            lcols = slice(cb * LANES, (cb + 1) * LANES)
            bias = bdw_ref[:, gcols]
            for r in range(rb // R):
                win = ufull[r * R:r * R + R + CONV_HIST, lcols]
                y_ref[blk * rb + r * R:blk * rb + (r + 1) * R, lcols] = _dwconv_block(win, wb_ref, bias, gcols, R)
        tail = u[rb - CONV_HIST:, :]
    carry_ref[j] = tail
    tail_ref[...] = tail


def _conv_mix(x, mod, g_norm, w_in, b_in, w_dw, b_dw, *, layer, tm, R, nb):
    Bp, T, _ = x.shape
    tn = 256
    J = D_MODEL // tn
    b2 = b_in.reshape(-1, 1, 2 * D_MODEL)
    return pl.pallas_call(
        functools.partial(_conv_mix_kernel, L=tm, R=R, nb=nb),
        grid=(Bp, T // tm, J),
        in_specs=[_row_spec(tm), _vec_spec(4 * layer + 0), _mod_spec(1, layer, 1), _mod_spec(1, layer, 0),
                  pl.BlockSpec((None, D_MODEL, tn), lambda b, t, j: (0, 0, j)),
                  pl.BlockSpec((None, D_MODEL, tn), lambda b, t, j: (0, 0, j + J)),
                  pl.BlockSpec((None, 1, tn), lambda b, t, j: (0, 0, j)),
                  pl.BlockSpec((None, 1, tn), lambda b, t, j: (0, 0, j + J)),
                  pl.BlockSpec((None, CONV_K, D_MODEL), lambda b, t, j: (0, 0, 0)),
                  _vec_spec(0)],
        out_specs=[pl.BlockSpec((None, tm, tn), lambda b, t, j: (b, t, j)),
                   pl.BlockSpec((None, CONV_HIST, tn), lambda b, t, j: (b, 0, j))],
        out_shape=[jax.ShapeDtypeStruct((Bp, T, D_MODEL), F32),
                   jax.ShapeDtypeStruct((Bp, CONV_HIST, D_MODEL), F32)],
        scratch_shapes=[pltpu.VMEM((tm, D_MODEL), BF16),
                        pltpu.VMEM((CONV_K, SUBLANES, D_MODEL), F32),
                        pltpu.VMEM((J, CONV_HIST, tn), F32)],
        compiler_params=_cparams(3),
        name="conv_mix",
    )(x, g_norm, mod, mod, w_in, w_in, b2, b2, w_dw, b_dw.reshape(-1, 1, D_MODEL))


def _conv_proj_kernel(y_ref, lng_ref, lnb_ref, wout_ref, bout_ref, x_ref, g_ref, ga_ref, x1_ref, a_ref, *, L):
    a_ref[...] = y_ref[...]
    _ln_silu(a_ref, lng_ref, lnb_ref, 1, L)
    x1_ref[...] = _dot(a_ref[...], wout_ref[...]) + bout_ref[...]
    _residual(x1_ref, x_ref, g_ref, ga_ref, x1_ref, 1, L)


def _conv_proj(y, x, mod, g_norm, ln_g, ln_b, w_out, b_out, *, layer, tm):
    Bp, T, _ = x.shape
    vec = lambda a: a.reshape(-1, 1, D_MODEL)
    return pl.pallas_call(
        functools.partial(_conv_proj_kernel, L=tm),
        grid=(Bp, T // tm),
        in_specs=[_row_spec(tm), _vec_spec(0), _vec_spec(0),
                  pl.BlockSpec((None, D_MODEL, D_MODEL), lambda b, t: (0, 0, 0), pipeline_mode=pl.Buffered(1)),
                  _vec_spec(0), _row_spec(tm), _vec_spec(4 * layer + 1), _mod_spec(1, layer, 2)],
        out_specs=_row_spec(tm),
        out_shape=jax.ShapeDtypeStruct((Bp, T, D_MODEL), F32),
        scratch_shapes=[pltpu.VMEM((tm, D_MODEL), F32)],
        compiler_params=_cparams(2),
        name="conv_proj",
    )(y, vec(ln_g), vec(ln_b), w_out, vec(b_out), x, g_norm, mod)


def _dwconv(full_ref, wb_ref, bdw_ref, y_ref, S, L, R):
    def body(cb, c):
        cols = pl.ds(pl.multiple_of(cb * LANES, LANES), LANES)
        bias = bdw_ref[:, cols]
        for s in range(S):
            for r in range(L // R):
                win = full_ref[s, r * R:r * R + R + CONV_HIST, cols]
                y_ref[s * L + r * R:s * L + (r + 1) * R, cols] = _dwconv_block(win, wb_ref, bias, cols, R)
        return c

    lax.fori_loop(0, D_MODEL // LANES, body, 0)


def _ln_silu(y_ref, lng_ref, lnb_ref, S, L):
    def fn(s, y):
        mu = jnp.mean(y, axis=-1, keepdims=True)
        yc = y - mu
        var = jnp.mean(yc * yc, axis=-1, keepdims=True)
        z = yc * lax.rsqrt(var + EPS) * lng_ref[...] + lnb_ref[...]
        return z * jax.nn.sigmoid(z)

    _row_loop(S, L, [y_ref], y_ref, fn)


def _conv_out_kernel(u_ref, hist_ref, wdw_ref, bdw_ref, lng_ref, lnb_ref, wout_ref, bout_ref, x_ref, g_ref, ga_ref,
                     x1_ref, full_ref, wb_ref, y_ref, *, S, L, R):
    _fill_tap_table(wdw_ref, wb_ref)
    pad = CONV_HIST - (CONV_K - 1)
    full_ref[:, 0:pad, :] = jnp.zeros((S, pad, D_MODEL), F32)
    full_ref[:, pad:CONV_HIST, :] = hist_ref[...]
    full_ref[:, CONV_HIST:CONV_HIST + L, :] = u_ref[...].reshape(S, L, D_MODEL)
    _dwconv(full_ref, wb_ref, bdw_ref, y_ref, S, L, R)
    _ln_silu(y_ref, lng_ref, lnb_ref, S, L)
    x1_ref[...] = _dot(y_ref[...], wout_ref[...]) + bout_ref[...]
    _residual(x1_ref, x_ref, g_ref, ga_ref, x1_ref, S, L)


def _conv_out(u, hist, x, mod, g_norm, w_dw, b_dw, ln_g, ln_b, w_out, b_out, *, layer, S, L, tm, R):
    Bp, T, _ = x.shape
    assert Bp == 1 and T == tm
    vec = lambda a: a.reshape(-1, 1, D_MODEL)
    return pl.pallas_call(
        functools.partial(_conv_out_kernel, S=S, L=L, R=R),
        grid=(Bp, T // tm),
        in_specs=[_row_spec(tm), pl.BlockSpec((S, CONV_K - 1, D_MODEL), lambda b, t: (0, 0, 0)),
                  pl.BlockSpec((None, CONV_K, D_MODEL), lambda b, t: (0, 0, 0)),
                  _vec_spec(0), _vec_spec(0), _vec_spec(0),
                  pl.BlockSpec((None, D_MODEL, D_MODEL), lambda b, t: (0, 0, 0), pipeline_mode=pl.Buffered(1)),
                  _vec_spec(0), _row_spec(tm), _vec_spec(4 * layer + 1), _mod_spec(S, layer, 2)],
        out_specs=_row_spec(tm),
        out_shape=jax.ShapeDtypeStruct((Bp, T, D_MODEL), F32),
        scratch_shapes=[pltpu.VMEM((S, CONV_HIST + L, D_MODEL), F32),
                        pltpu.VMEM((CONV_K, SUBLANES, D_MODEL), F32),
                        pltpu.VMEM((tm, D_MODEL), F32)],
        compiler_params=_cparams(2),
        name="conv_out",
    )(u, hist, w_dw, vec(b_dw), vec(ln_g), vec(ln_b), w_out, vec(b_out), x, g_norm, mod)


def _ffn_kernel(*refs, S, L, nb, prompt):
    if prompt:
        (x_ref, g2_ref, sc_ref, sh_ref, wg_ref, wv_ref, wdw_ref, bdw_ref, wd_ref, g3_ref, ga_ref,
         x2_ref, st_ref, h_ref, carry_ref) = refs
    else:
        (x_ref, g2_ref, sc_ref, sh_ref, wg_ref, wv_ref, wdw_ref, bdw_ref, wd_ref, g3_ref, ga_ref, hist_ref,
         x2_ref, st_ref, h_ref) = refs
    t = pl.program_id(1)
    j = pl.program_id(2)
    tm = x_ref.shape[0]
    rb = tm // nb
    seg = min(L, rb)
    pad = SUBLANES

    @pl.when(j == 0)
    def _():
        _norm_mod(x_ref, h_ref, g2_ref, sc_ref, sh_ref, S, L)
        x2_ref[...] = jnp.zeros(x2_ref.shape, F32)

    wg = wg_ref[...].astype(BF16)
    wv = wv_ref[...].astype(BF16)
    wd = wd_ref[...].astype(BF16)
    w0 = wdw_ref[0:1, :]
    w1 = wdw_ref[1:2, :]
    w2 = wdw_ref[2:3, :]
    bd = bdw_ref[...]
    tail = None
    for blk in range(nb):
        rows = slice(blk * rb, (blk + 1) * rb)
        hb = h_ref[rows, :]
        g = _dot(hb, wg)
        v = _dot(hb, wv)
        p1s, p2s = [], []
        for q in range(rb // seg):
            gq = g[q * seg:(q + 1) * seg, :]
            if not prompt:
                s = blk * (rb // seg) + q
                hist = hist_ref[s]
                st_ref[s] = gq[seg - (FFN_K - 1):, :]
            elif blk == 0:
                hist = jnp.where(t == 0, 0.0, carry_ref[j])
            else:
                hist = tail
            gfull = jnp.concatenate([hist, gq], axis=0)
            p1s.append(pltpu.roll(gfull, 1, axis=0)[pad:, :])
            p2s.append(pltpu.roll(gfull, 2, axis=0)[pad:, :])
            tail = gq[seg - pad:, :]
        p1 = p1s[0] if len(p1s) == 1 else jnp.concatenate(p1s, axis=0)
        p2 = p2s[0] if len(p2s) == 1 else jnp.concatenate(p2s, axis=0)
        gc = g * w2 + p1 * w1 + p2 * w0 + bd
        act = (jax.nn.gelu(gc) * v).astype(BF16)
        x2_ref[rows, :] += _dot(act, wd)
    if prompt:
        carry_ref[j] = tail
        st_ref[0] = tail[pad - (FFN_K - 1):, :]

    @pl.when(j == pl.num_programs(2) - 1)
    def _():
        _residual(x2_ref, x_ref, g3_ref, ga_ref, x2_ref, S, L)


def _ffn(x, hist, mod, g_norm, w_up, w_dw, b_dw, w_down, *, layer, S, L, tm, tf, nb):
    Bp, T, _ = x.shape
    J = D_FF // tf
    nT = T // tm
    prompt = hist is None
    in_specs = [_row_spec(tm), _vec_spec(4 * layer + 2), _mod_spec(S, layer, 4), _mod_spec(S, layer, 3),
                pl.BlockSpec((None, D_MODEL, tf), lambda b, t, j: (layer, 0, j)),
                pl.BlockSpec((None, D_MODEL, tf), lambda b, t, j: (layer, 0, j + J)),
                pl.BlockSpec((None, FFN_K, tf), lambda b, t, j: (layer, 0, j)),
                pl.BlockSpec((None, 1, tf), lambda b, t, j: (layer, 0, j)),
                pl.BlockSpec((None, tf, D_MODEL), lambda b, t, j: (layer, j, 0)),
                _vec_spec(4 * layer + 3), _mod_spec(S, layer, 5)]
    args = [x, g_norm, mod, mod, w_up, w_up, w_dw, b_dw.reshape(DEPTH, 1, D_FF), w_down, g_norm, mod]
    scratch = [pltpu.VMEM((tm, D_MODEL), BF16)]
    if prompt:
        scratch.append(pltpu.VMEM((J, SUBLANES, tf), F32))
    else:
        in_specs.append(pl.BlockSpec((S, SUBLANES, tf), lambda b, t, j: (0, 0, j)))
        args.append(hist)
    return pl.pallas_call(
        functools.partial(_ffn_kernel, S=S, L=L, nb=nb, prompt=prompt),
        grid=(Bp, nT, J),
        in_specs=in_specs,
        out_specs=[_row_spec(tm),
                   pl.BlockSpec((None, None, S, FFN_K - 1, tf), lambda b, t, j: (b, t, 0, 0, j))],
        out_shape=[jax.ShapeDtypeStruct((Bp, T, D_MODEL), F32),
                   jax.ShapeDtypeStruct((Bp, nT, S, FFN_K - 1, D_FF), F32)],
        scratch_shapes=scratch,
        compiler_params=_cparams(3),
        name="ffn",
    )(*args)


def _qkv_kernel(x_ref, g_ref, sc_ref, sh_ref, wq_ref, wk_ref, wv_ref, q_ref, k_ref, v_ref, h_ref, *, S, L):
    _norm_mod(x_ref, h_ref, g_ref, sc_ref, sh_ref, S, L)
    h = h_ref[...]
    q_ref[...] = _dot(h, wq_ref[...])
    k_ref[...] = _dot(h, wk_ref[...])
    v_ref[...] = _dot(h, wv_ref[...])


def _qkv(x, mod, g_norm, w_q, w_k, w_v, *, layer, S, L, tm):
    Bp, T, _ = x.shape
    kv_spec = pl.BlockSpec((None, tm, KV_DIM), lambda b, t: (b, t, 0))
    resident = lambda n: pl.BlockSpec((None, D_MODEL, n), lambda b, t: (0, 0, 0), pipeline_mode=pl.Buffered(1))
    return pl.pallas_call(
        functools.partial(_qkv_kernel, S=S, L=L),
        grid=(Bp, T // tm),
        in_specs=[_row_spec(tm), _vec_spec(4 * layer + 0), _mod_spec(S, layer, 1), _mod_spec(S, layer, 0),
                  resident(D_MODEL), resident(KV_DIM), resident(KV_DIM)],
        out_specs=[_row_spec(tm), kv_spec, kv_spec],
        out_shape=[jax.ShapeDtypeStruct((Bp, T, D_MODEL), F32),
                   jax.ShapeDtypeStruct((Bp, T, KV_DIM), F32),
                   jax.ShapeDtypeStruct((Bp, T, KV_DIM), F32)],
        scratch_shapes=[pltpu.VMEM((tm, D_MODEL), F32)],
        compiler_params=_cparams(2),
        name="qkv",
    )(x, g_norm, mod, mod, w_q, w_k, w_v)


def _bias_kernel(rb_ref, bucket_ref, o_ref):
    bk = bucket_ref[...]
    for h in range(N_HEADS):
        acc = jnp.full(bk.shape, NEG_INF, F32)
        for b in range(N_BUCKETS):
            acc = jnp.where(bk == b, rb_ref[b, h], acc)
        o_ref[h] = acc


def _t5_bucket(rel):
    half = N_BUCKETS // 2
    max_exact = half // 2
    n = jnp.abs(rel)
    ret = jnp.where(rel > 0, half, 0)
    nf = jnp.maximum(n, 1).astype(F32)
    large = max_exact + (jnp.log(nf / max_exact) / math.log(MAX_DISTANCE / max_exact)
                         * (half - max_exact)).astype(jnp.int32)
    large = jnp.minimum(large, half - 1)
    return ret + jnp.where(n < max_exact, n, large)


def _bias_table(rel_bias, Qn, lo, hi):
    k = jnp.arange(KEY_WIN)[:, None]
    q = jnp.arange(Qn)[None, :]
    bucket = jnp.where((k >= lo(q)) & (k < hi(q)), _t5_bucket(k - WINDOW - q), -1).astype(jnp.int32)
    table = pl.pallas_call(
        _bias_kernel,
        in_specs=[pl.BlockSpec(memory_space=pltpu.SMEM),
                  pl.BlockSpec((KEY_WIN, Qn), lambda: (0, 0))],
        out_specs=pl.BlockSpec((N_HEADS, KEY_WIN, Qn), lambda: (0, 0, 0)),
        out_shape=jax.ShapeDtypeStruct((N_HEADS, KEY_WIN, Qn), F32),
        name="rel_bias_table",
    )(rel_bias, bucket)
    table = table.reshape(N_KV_HEADS, GROUP, KEY_WIN, Qn).transpose(0, 2, 1, 3)
    return table.reshape(N_KV_HEADS, KEY_WIN, GROUP * Qn)


def _attend(q_ref, r0, Qn, kw, vw, bias_ref, sink_ref, valid, o_ref, qs_ref):
    rows = pl.ds(r0, Qn)
    for g in range(GROUP):
        for h in range(N_KV_HEADS):
            src = (h * GROUP + g) * HEAD_DIM
            qs_ref[g * Qn:(g + 1) * Qn, h * HEAD_DIM:(h + 1) * HEAD_DIM] = q_ref[rows, src:src + HEAD_DIM]
    qt = qs_ref[...].T
    vt = vw.T
    outs = []
    for h in range(N_KV_HEADS):
        hd = slice(h * HEAD_DIM, (h + 1) * HEAD_DIM)
        s = _dot(kw[:, hd], qt[hd, :]) * ATTN_SCALE + bias_ref[h]
        if valid is not None:
            s = jnp.where(valid, s, NEG_INF)
        sk = sink_ref[h]
        mx = jnp.maximum(jnp.max(s, axis=0, keepdims=True), sk)
        p = jnp.exp(s - mx)
        den = jnp.sum(p, axis=0, keepdims=True) + jnp.exp(sk - mx)
        outs.append(_dot(vt[hd, :], p) * (1.0 / den))
    o2 = jnp.concatenate(outs, axis=0).T
    for g in range(GROUP):
        for h in range(N_KV_HEADS):
            dst = (h * GROUP + g) * HEAD_DIM
            o_ref[rows, dst:dst + HEAD_DIM] = o2[g * Qn:(g + 1) * Qn, h * HEAD_DIM:(h + 1) * HEAD_DIM]


def _attn_prompt_kernel(q_ref, kp_ref, kc_ref, vp_ref, vc_ref, bias_ref, sink_ref, o_ref, kw_ref, vw_ref, qs_ref,
                        *, tq, Qn):
    t = pl.program_id(1)
    kw_ref[0:WINDOW, :] = kp_ref[...]
    kw_ref[WINDOW:WINDOW + tq, :] = kc_ref[...]
    vw_ref[0:WINDOW, :] = vp_ref[...]
    vw_ref[WINDOW:WINDOW + tq, :] = vc_ref[...]

    def body(m, carry):
        r0 = pl.multiple_of(m * Qn, Qn)
        kw = kw_ref[pl.ds(r0, KEY_WIN), :]
        vw = vw_ref[pl.ds(r0, KEY_WIN), :]
        pos = t * tq + r0 - WINDOW + lax.broadcasted_iota(jnp.int32, (KEY_WIN, 1), 0)
        _attend(q_ref, r0, Qn, kw, vw, bias_ref, sink_ref, pos >= 0, o_ref, qs_ref)
        return carry

    lax.fori_loop(0, tq // Qn, body, 0)


def _attn_sample_kernel(q_ref, ck_ref, kn_ref, cv_ref, vn_ref, bias_ref, sink_ref, o_ref, kw_ref, vw_ref, qs_ref,
                        *, S, L):
    P = ck_ref.shape[1]
    kw_ref[:, 0:P, :] = ck_ref[...]
    kw_ref[:, P:P + L, :] = kn_ref[...].reshape(S, L, KV_DIM)
    kw_ref[:, P + L:, :] = jnp.zeros((S, KEY_WIN - P - L, KV_DIM), F32)
    vw_ref[:, 0:P, :] = cv_ref[...]
    vw_ref[:, P:P + L, :] = vn_ref[...].reshape(S, L, KV_DIM)
    vw_ref[:, P + L:, :] = jnp.zeros((S, KEY_WIN - P - L, KV_DIM), F32)

    def body(s, carry):
        r0 = pl.multiple_of(s * L, L)
        _attend(q_ref, r0, L, kw_ref[s], vw_ref[s], bias_ref, sink_ref, None, o_ref, qs_ref)
        return carry

    lax.fori_loop(0, S, body, 0)


def _attn_prompt(q, k, v, bias, sink, *, tq):
    Bp, T, _ = q.shape
    Qn = 2 * CHUNK
    per = tq // WINDOW
    prev_spec = pl.BlockSpec((None, WINDOW, KV_DIM), lambda b, t: (b, jnp.maximum(t * per - 1, 0), 0))
    cur_spec = pl.BlockSpec((None, tq, KV_DIM), lambda b, t: (b, t, 0))
    return pl.pallas_call(
        functools.partial(_attn_prompt_kernel, tq=tq, Qn=Qn),
        grid=(Bp, T // tq),
        in_specs=[_row_spec(tq), prev_spec, cur_spec, prev_spec, cur_spec,
                  pl.BlockSpec((N_KV_HEADS, KEY_WIN, GROUP * Qn), lambda b, t: (0, 0, 0)),
                  pl.BlockSpec((N_KV_HEADS, 1, GROUP * Qn), lambda b, t: (0, 0, 0))],
        out_specs=_row_spec(tq),
        out_shape=jax.ShapeDtypeStruct((Bp, T, D_MODEL), F32),
        scratch_shapes=[pltpu.VMEM((WINDOW + tq, KV_DIM), F32), pltpu.VMEM((WINDOW + tq, KV_DIM), F32),
                        pltpu.VMEM((GROUP * Qn, KV_DIM), F32)],
        compiler_params=_cparams(2),
        name="attn_prompt",
    )(q, k, k, v, v, bias, sink)


def _attn_sample(q, k, v, cache_k, cache_v, bias, sink, *, S, L):
    return pl.pallas_call(
        functools.partial(_attn_sample_kernel, S=S, L=L),
        out_shape=jax.ShapeDtypeStruct((S * L, D_MODEL), F32),
        scratch_shapes=[pltpu.VMEM((S, KEY_WIN, KV_DIM), F32), pltpu.VMEM((S, KEY_WIN, KV_DIM), F32),
                        pltpu.VMEM((GROUP * L, KV_DIM), F32)],
        compiler_params=pltpu.CompilerParams(vmem_limit_bytes=VMEM_LIMIT),
        name="attn_sample",
    )(q, cache_k, k, cache_v, v, bias, sink)


def _attn_out_kernel(o_ref, w_ref, x_ref, g_ref, ga_ref, x1_ref, *, S, L):
    x1_ref[...] = _dot(o_ref[...], w_ref[...])
    _residual(x1_ref, x_ref, g_ref, ga_ref, x1_ref, S, L)


def _attn_out(o, x, mod, g_norm, w_o, *, layer, S, L, tm):
    Bp, T, _ = x.shape
    return pl.pallas_call(
        functools.partial(_attn_out_kernel, S=S, L=L),
        grid=(Bp, T // tm),
        in_specs=[_row_spec(tm),
                  pl.BlockSpec((None, D_MODEL, D_MODEL), lambda b, t: (0, 0, 0), pipeline_mode=pl.Buffered(1)),
                  _row_spec(tm), _vec_spec(4 * layer + 1), _mod_spec(S, layer, 2)],
        out_specs=_row_spec(tm),
        out_shape=jax.ShapeDtypeStruct((Bp, T, D_MODEL), F32),
        compiler_params=_cparams(2),
        name="attn_out",
    )(o, w_o, x, g_norm, mod)


def kernel(x_prompt, x_sample, c_prompt, c_sample, cache_conv, cache_k, cache_v, cache_ffn, w_mod, b_mod, g_norm, conv_w_in, conv_b_in, conv_w_dw, conv_b_dw, conv_ln_g, conv_ln_b, conv_w_out, conv_b_out, attn_w_q, attn_w_k, attn_w_v, attn_w_o, attn_sinks, rel_bias, ffn_w_up, ffn_w_dw, ffn_b_dw, ffn_w_down):
    B, T, D = x_prompt.shape
    SB, SL, _ = x_sample.shape
    assert SB == MOD_ROWS_SAMPLE
    c_all = jnp.concatenate(
        [c_sample, jnp.pad(c_prompt[:, None, :], ((0, 0), (0, SUBLANES - 1), (0, 0))).reshape(B * SUBLANES, D)], axis=0)
    mod = _modulation(c_all, w_mod, b_mod)
    gn = g_norm.reshape(DEPTH * 4, 1, D)

    geo_p = dict(S=1, L=1024, tm=1024)
    geo_p_small = dict(S=1, L=512, tm=512)
    geo_s = dict(S=SB, L=SL, tm=SB * SL)

    xp = x_prompt
    xs = x_sample.reshape(1, SB * SL, D)

    yp, tail_p = _conv_mix(xp, mod, gn, conv_w_in, conv_b_in, conv_w_dw, conv_b_dw, layer=0, tm=1024, R=64, nb=2)
    xp = _conv_proj(yp, xp, mod, gn, conv_ln_g, conv_ln_b, conv_w_out, conv_b_out, layer=0, tm=512)
    us = _conv_in(xs, mod, gn, conv_w_in, conv_b_in, layer=0, **geo_s)
    conv_args = (conv_w_dw, conv_b_dw, conv_ln_g, conv_ln_b, conv_w_out, conv_b_out)
    xs = _conv_out(us, cache_conv[0], xs, mod, gn, *conv_args, layer=0, R=SL, **geo_s)
    conv_state_p = tail_p[:, CONV_HIST - (CONV_K - 1):, :][None]
    conv_state_s = us.reshape(SB, SL, D)[:, SL - (CONV_K - 1):, :][None]

    ffn_p, ffn_s = [], []
    ffn_hist = jnp.pad(cache_ffn, ((0, 0), (0, 0), (SUBLANES - (FFN_K - 1), 0), (0, 0)))

    def run_ffn(i, xp, xs):
        w = (ffn_w_up, ffn_w_dw, ffn_b_dw, ffn_w_down)
        xp, st_p = _ffn(xp, None, mod, gn, *w, layer=i, tf=256, nb=2, **geo_p)
        xs, st_s = _ffn(xs, ffn_hist[i], mod, gn, *w, layer=i, tf=256, nb=2, **geo_s)
        ffn_p.append(st_p[:, -1, 0])
        ffn_s.append(st_s[0, 0])
        return xp, xs

    xp, xs = run_ffn(0, xp, xs)

    qp, kp, vp = _qkv(xp, mod, gn, attn_w_q, attn_w_k, attn_w_v, layer=1, **geo_p_small)
    qs, ks, vs = _qkv(xs, mod, gn, attn_w_q, attn_w_k, attn_w_v, layer=1, **geo_s)
    P = cache_k.shape[2]
    assert P == WINDOW and P + SL <= KEY_WIN
    bias_p = _bias_table(rel_bias, 2 * CHUNK,
                         lambda q: (q // CHUNK) * CHUNK, lambda q: (q // CHUNK) * CHUNK + WINDOW + CHUNK)
    bias_s = _bias_table(rel_bias, SL, lambda q: 0 * q, lambda q: 0 * q + P + SL)
    sink_p = jnp.repeat(attn_sinks[0], 2 * CHUNK).reshape(N_KV_HEADS, 1, GROUP * 2 * CHUNK)
    sink_s = jnp.repeat(attn_sinks[0], SL).reshape(N_KV_HEADS, 1, GROUP * SL)
    op = _attn_prompt(qp, kp, vp, bias_p, sink_p, tq=256)
    ck = cache_k[0].reshape(SB, P, KV_DIM)
    cv = cache_v[0].reshape(SB, P, KV_DIM)
    os_ = _attn_sample(qs[0], ks[0], vs[0], ck, cv, bias_s, sink_s, S=SB, L=SL)[None]
    xp = _attn_out(op, xp, mod, gn, attn_w_o, layer=1, **geo_p_small)
    xs = _attn_out(os_, xs, mod, gn, attn_w_o, layer=1, **geo_s)
    keep = min(WINDOW, T)
    k_state_p = kp[:, T - keep:].reshape(B, keep, N_KV_HEADS, HEAD_DIM)[None]
    v_state_p = vp[:, T - keep:].reshape(B, keep, N_KV_HEADS, HEAD_DIM)[None]
    k_state_s = jnp.concatenate([ck, ks.reshape(SB, SL, KV_DIM)], axis=1)[:, SL:]
    v_state_s = jnp.concatenate([cv, vs.reshape(SB, SL, KV_DIM)], axis=1)[:, SL:]
    k_state_s = k_state_s.reshape(SB, P, N_KV_HEADS, HEAD_DIM)[None]
    v_state_s = v_state_s.reshape(SB, P, N_KV_HEADS, HEAD_DIM)[None]

    xp, xs = run_ffn(1, xp, xs)

    return (xp, xs.reshape(SB, SL, D), conv_state_p, conv_state_s,
            k_state_p, v_state_p, k_state_s, v_state_s,
            jnp.stack(ffn_p), jnp.stack(ffn_s))
```

```python
import functools
import math

import jax
import jax.numpy as jnp
from jax import lax
from jax.experimental import pallas as pl
from jax.experimental.pallas import tpu as pltpu

F32 = jnp.float32
BF16 = jnp.bfloat16

D_MODEL = 2048
D_FF = 5632
DEPTH = 2
CONV_K = 31
FFN_K = 3
N_HEADS = 32
N_KV_HEADS = 4
GROUP = N_HEADS // N_KV_HEADS
HEAD_DIM = 64
KV_DIM = N_KV_HEADS * HEAD_DIM
CHUNK = 64
WINDOW = 128
N_BUCKETS = 32
MAX_DISTANCE = 128
EPS = 1e-6
NEG_INF = -1e30
ATTN_SCALE = HEAD_DIM ** -0.5

SUBLANES = 8
LANES = 128
CONV_HIST = 32
KEY_WIN = 2 * WINDOW
VMEM_LIMIT = 58 * 1024 * 1024
VMEM_LIMIT_FFN = 62 * 1024 * 1024
MOD_ROWS_SAMPLE = 16


def _cparams(n_grid):
    return pltpu.CompilerParams(dimension_semantics=("arbitrary",) * n_grid,
                                vmem_limit_bytes=VMEM_LIMIT)


def _dot(a, b):
    return jnp.dot(a, b, preferred_element_type=F32)


ROW_SLAB = 64
ROW_GROUP = 2


def _row_loop(S, L, srcs, dst, fn):
    slab = min(L, ROW_SLAB)
    n = L // slab
    group = ROW_GROUP if n % ROW_GROUP == 0 else 1
    for s in range(S):

        def body(r, c, s=s):
            rows = [pl.ds(pl.multiple_of(s * L + (r * group + i) * slab, slab), slab) for i in range(group)]
            vals = [[src[rw, :] for src in srcs] for rw in rows]
            outs = [fn(s, *v) for v in vals]
            for rw, out in zip(rows, outs):
                dst[rw, :] = out.astype(dst.dtype)
            return c

        lax.fori_loop(0, n // group, body, 0)


def _norm_mod(x_ref, dst_ref, g_ref, sc_ref, sh_ref, S, L):
    def fn(s, x):
        inv = lax.rsqrt(jnp.mean(x * x, axis=-1, keepdims=True) + EPS)
        mul = g_ref[...] * (1.0 + sc_ref[s:s + 1, :])
        return x * inv * mul + sh_ref[s:s + 1, :]

    _row_loop(S, L, [x_ref], dst_ref, fn)


def _residual(m_ref, x_ref, g_ref, ga_ref, out_ref, S, L):
    def fn(s, m, x):
        inv = lax.rsqrt(jnp.mean(m * m, axis=-1, keepdims=True) + EPS)
        return x + m * inv * (g_ref[...] * ga_ref[s:s + 1, :])

    _row_loop(S, L, [m_ref, x_ref], out_ref, fn)


def _row_spec(tm):
    return pl.BlockSpec((None, tm, D_MODEL), lambda b, t, *_: (b, t, 0))


def _vec_spec(index, n=D_MODEL):
    return pl.BlockSpec((None, 1, n), lambda *_: (index, 0, 0))


def _mod_spec(S, layer, which):
    if S == 1:
        first = MOD_ROWS_SAMPLE // SUBLANES
        return pl.BlockSpec((None, None, SUBLANES, D_MODEL), lambda b, t, *_: (layer, which, first + b, 0))
    return pl.BlockSpec((None, None, S, D_MODEL), lambda b, t, *_: (layer, which, 0, 0))


def _mod_kernel(c_ref, w_ref, b_ref, o_ref):
    c = c_ref[...]
    o_ref[...] = _dot(c * jax.nn.sigmoid(c), w_ref[...]) + b_ref[...]


def _modulation(c_all, w_mod, b_mod):
    rows = c_all.shape[0]
    tn = 1024
    per = D_MODEL // tn
    return pl.pallas_call(
        _mod_kernel,
        grid=(DEPTH, 6 * per),
        in_specs=[pl.BlockSpec((rows, D_MODEL), lambda i, j: (0, 0)),
                  pl.BlockSpec((None, D_MODEL, tn), lambda i, j: (i, 0, j)),
                  pl.BlockSpec((None, 1, tn), lambda i, j: (i, 0, j))],
        out_specs=pl.BlockSpec((None, None, rows, tn), lambda i, j: (i, j // per, 0, j % per)),
        out_shape=jax.ShapeDtypeStruct((DEPTH, 6, rows, D_MODEL), F32),
        compiler_params=_cparams(2),
        name="modulation",
    )(c_all, w_mod, b_mod.reshape(DEPTH, 1, 6 * D_MODEL))


def _conv_in_kernel(x_ref, g_ref, sc_ref, sh_ref, wa_ref, wg_ref, ba_ref, bg_ref, u_ref, h_ref, *, S, L):
    @pl.when(pl.program_id(2) == 0)
    def _():
        _norm_mod(x_ref, h_ref, g_ref, sc_ref, sh_ref, S, L)

    h = h_ref[...]
    a = _dot(h, wa_ref[...]) + ba_ref[...]
    g = _dot(h, wg_ref[...]) + bg_ref[...]
    u_ref[...] = a * jax.nn.sigmoid(g)


def _conv_in(x, mod, g_norm, w_in, b_in, *, layer, S, L, tm):
    Bp, T, _ = x.shape
    tn = 512
    J = D_MODEL // tn
    b2 = b_in.reshape(-1, 1, 2 * D_MODEL)
    return pl.pallas_call(
        functools.partial(_conv_in_kernel, S=S, L=L),
        grid=(Bp, T // tm, J),
        in_specs=[_row_spec(tm), _vec_spec(4 * layer + 0), _mod_spec(S, layer, 1), _mod_spec(S, layer, 0),
                  pl.BlockSpec((None, D_MODEL, tn), lambda b, t, j: (0, 0, j)),
                  pl.BlockSpec((None, D_MODEL, tn), lambda b, t, j: (0, 0, j + J)),
                  pl.BlockSpec((None, 1, tn), lambda b, t, j: (0, 0, j)),
                  pl.BlockSpec((None, 1, tn), lambda b, t, j: (0, 0, j + J))],
        out_specs=pl.BlockSpec((None, tm, tn), lambda b, t, j: (b, t, j)),
        out_shape=jax.ShapeDtypeStruct((Bp, T, D_MODEL), F32),
        scratch_shapes=[pltpu.VMEM((tm, D_MODEL), BF16)],
        compiler_params=_cparams(3),
        name="conv_in",
    )(x, g_norm, mod, mod, w_in, w_in, b2, b2)


def _dwconv_block(win, wb_ref, bias, cols, R):
    n_groups = (CONV_K + 1 + SUBLANES) // SUBLANES
    win_rows = R + CONV_HIST
    acc = jnp.broadcast_to(bias, (R // SUBLANES, SUBLANES, LANES))
    for sh in range(SUBLANES):
        wsh = win if sh == 0 else pltpu.roll(win, win_rows - sh, axis=0)
        for a in range(n_groups):
            k = SUBLANES * a + sh - 2
            if 0 <= k < CONV_K:
                tap = wsh[SUBLANES * a:SUBLANES * a + R, :]
                acc = acc + tap.reshape(R // SUBLANES, SUBLANES, LANES) * wb_ref[k, :, cols]
    return acc.reshape(R, LANES)


def _fill_tap_table(wdw_ref, wb_ref):
    for k in range(CONV_K):
        wb_ref[k] = jnp.broadcast_to(wdw_ref[k:k + 1, :], (SUBLANES, D_MODEL))


def _conv_mix_kernel(x_ref, g_ref, sc_ref, sh_ref, wa_ref, wg_ref, ba_ref, bg_ref, wdw_ref, bdw_ref,
                     y_ref, tail_ref, h_ref, wb_ref, carry_ref, *, L, R, nb):
    t = pl.program_id(1)
    j = pl.program_id(2)
    tn = wa_ref.shape[1]
    rb = L // nb

    @pl.when((pl.program_id(0) == 0) & (t == 0) & (j == 0))
    def _():
        _fill_tap_table(wdw_ref, wb_ref)
        carry_ref[...] = jnp.zeros(carry_ref.shape, F32)

    @pl.when(j == 0)
    def _():
        _norm_mod(x_ref, h_ref, g_ref, sc_ref, sh_ref, 1, L)

    wa = wa_ref[...].astype(BF16)
    wg = wg_ref[...].astype(BF16)
    tail = jnp.where(t == 0, 0.0, carry_ref[j])
    for blk in range(nb):
        hb = h_ref[blk * rb:(blk + 1) * rb, :]
        a = _dot(hb, wa) + ba_ref[...]
        g = _dot(hb, wg) + bg_ref[...]
        u = a * jax.nn.sigmoid(g)
        ufull = jnp.concatenate([tail, u], axis=0)
        for cb in range(tn // LANES):
            gcols = pl.ds(pl.multiple_of(j * tn + cb * LANES, LANES), LANES)
            lcols = slice(cb * LANES, (cb + 1) * LANES)
            bias = bdw_ref[:, gcols]
            for r in range(rb // R):
                win = ufull[r * R:r * R + R + CONV_HIST, lcols]
                y_ref[blk * rb + r * R:blk * rb + (r + 1) * R, lcols] = _dwconv_block(win, wb_ref, bias, gcols, R)
        tail = u[rb - CONV_HIST:, :]
    carry_ref[j] = tail
    tail_ref[...] = tail


def _conv_mix(x, mod, g_norm, w_in, b_in, w_dw, b_dw, *, layer, tm, R, nb):
    Bp, T, _ = x.shape
    tn = 256
    J = D_MODEL // tn
    b2 = b_in.reshape(-1, 1, 2 * D_MODEL)
    return pl.pallas_call(
        functools.partial(_conv_mix_kernel, L=tm, R=R, nb=nb),
        grid=(Bp, T // tm, J),
        in_specs=[_row_spec(tm), _vec_spec(4 * layer + 0), _mod_spec(1, layer, 1), _mod_spec(1, layer, 0),
                  pl.BlockSpec((None, D_MODEL, tn), lambda b, t, j: (0, 0, j)),
                  pl.BlockSpec((None, D_MODEL, tn), lambda b, t, j: (0, 0, j + J)),
                  pl.BlockSpec((None, 1, tn), lambda b, t, j: (0, 0, j)),
                  pl.BlockSpec((None, 1, tn), lambda b, t, j: (0, 0, j + J)),
                  pl.BlockSpec((None, CONV_K, D_MODEL), lambda b, t, j: (0, 0, 0)),
                  _vec_spec(0)],
        out_specs=[pl.BlockSpec((None, tm, tn), lambda b, t, j: (b, t, j)),
                   pl.BlockSpec((None, None, CONV_HIST, tn), lambda b, t, j: (b, t, 0, j))],
        out_shape=[jax.ShapeDtypeStruct((Bp, T, D_MODEL), F32),
                   jax.ShapeDtypeStruct((Bp, T // tm, CONV_HIST, D_MODEL), F32)],
        scratch_shapes=[pltpu.VMEM((tm, D_MODEL), BF16),
                        pltpu.VMEM((CONV_K, SUBLANES, D_MODEL), F32),
                        pltpu.VMEM((J, CONV_HIST, tn), F32)],
        compiler_params=_cparams(3),
        name="conv_mix",
    )(x, g_norm, mod, mod, w_in, w_in, b2, b2, w_dw, b_dw.reshape(-1, 1, D_MODEL))


def _conv_proj_kernel(y_ref, lng_ref, lnb_ref, wout_ref, bout_ref, x_ref, g_ref, ga_ref, x1_ref, a_ref, *, L):
    a_ref[...] = y_ref[...]
    _ln_silu(a_ref, lng_ref, lnb_ref, 1, L)
    x1_ref[...] = _dot(a_ref[...], wout_ref[...]) + bout_ref[...]
    _residual(x1_ref, x_ref, g_ref, ga_ref, x1_ref, 1, L)


def _conv_proj(y, x, mod, g_norm, ln_g, ln_b, w_out, b_out, *, layer, tm):
    Bp, T, _ = x.shape
    vec = lambda a: a.reshape(-1, 1, D_MODEL)
    return pl.pallas_call(
        functools.partial(_conv_proj_kernel, L=tm),
        grid=(Bp, T // tm),
        in_specs=[_row_spec(tm), _vec_spec(0), _vec_spec(0),
                  pl.BlockSpec((None, D_MODEL, D_MODEL), lambda b, t: (0, 0, 0), pipeline_mode=pl.Buffered(1)),
                  _vec_spec(0), _row_spec(tm), _vec_spec(4 * layer + 1), _mod_spec(1, layer, 2)],
        out_specs=_row_spec(tm),
        out_shape=jax.ShapeDtypeStruct((Bp, T, D_MODEL), F32),
        scratch_shapes=[pltpu.VMEM((tm, D_MODEL), F32)],
        compiler_params=_cparams(2),
        name="conv_proj",
    )(y, vec(ln_g), vec(ln_b), w_out, vec(b_out), x, g_norm, mod)


def _dwconv(full_ref, wb_ref, bdw_ref, y_ref, S, L, R):
    def body(cb, c):
        cols = pl.ds(pl.multiple_of(cb * LANES, LANES), LANES)
        bias = bdw_ref[:, cols]
        for s in range(S):
            for r in range(L // R):
                win = full_ref[s, r * R:r * R + R + CONV_HIST, cols]
                y_ref[s * L + r * R:s * L + (r + 1) * R, cols] = _dwconv_block(win, wb_ref, bias, cols, R)
        return c

    lax.fori_loop(0, D_MODEL // LANES, body, 0)


def _ln_silu(y_ref, lng_ref, lnb_ref, S, L):
    def fn(s, y):
        mu = jnp.mean(y, axis=-1, keepdims=True)
        yc = y - mu
        var = jnp.mean(yc * yc, axis=-1, keepdims=True)
        z = yc * lax.rsqrt(var + EPS) * lng_ref[...] + lnb_ref[...]
        return z * jax.nn.sigmoid(z)

    _row_loop(S, L, [y_ref], y_ref, fn)


def _conv_out_kernel(u_ref, hist_ref, wdw_ref, bdw_ref, lng_ref, lnb_ref, wout_ref, bout_ref, x_ref, g_ref, ga_ref,
                     x1_ref, full_ref, wb_ref, y_ref, *, S, L, R):
    _fill_tap_table(wdw_ref, wb_ref)
    pad = CONV_HIST - (CONV_K - 1)
    full_ref[:, 0:pad, :] = jnp.zeros((S, pad, D_MODEL), F32)
    full_ref[:, pad:CONV_HIST, :] = hist_ref[...]
    full_ref[:, CONV_HIST:CONV_HIST + L, :] = u_ref[...].reshape(S, L, D_MODEL)
    _dwconv(full_ref, wb_ref, bdw_ref, y_ref, S, L, R)
    _ln_silu(y_ref, lng_ref, lnb_ref, S, L)
    x1_ref[...] = _dot(y_ref[...], wout_ref[...]) + bout_ref[...]
    _residual(x1_ref, x_ref, g_ref, ga_ref, x1_ref, S, L)


def _conv_out(u, hist, x, mod, g_norm, w_dw, b_dw, ln_g, ln_b, w_out, b_out, *, layer, S, L, tm, R):
    Bp, T, _ = x.shape
    assert Bp == 1 and T == tm
    vec = lambda a: a.reshape(-1, 1, D_MODEL)
    return pl.pallas_call(
        functools.partial(_conv_out_kernel, S=S, L=L, R=R),
        grid=(Bp, T // tm),
        in_specs=[_row_spec(tm), pl.BlockSpec((S, CONV_K - 1, D_MODEL), lambda b, t: (0, 0, 0)),
                  pl.BlockSpec((None, CONV_K, D_MODEL), lambda b, t: (0, 0, 0)),
                  _vec_spec(0), _vec_spec(0), _vec_spec(0),
                  pl.BlockSpec((None, D_MODEL, D_MODEL), lambda b, t: (0, 0, 0), pipeline_mode=pl.Buffered(1)),
                  _vec_spec(0), _row_spec(tm), _vec_spec(4 * layer + 1), _mod_spec(S, layer, 2)],
        out_specs=_row_spec(tm),
        out_shape=jax.ShapeDtypeStruct((Bp, T, D_MODEL), F32),
        scratch_shapes=[pltpu.VMEM((S, CONV_HIST + L, D_MODEL), F32),
                        pltpu.VMEM((CONV_K, SUBLANES, D_MODEL), F32),
                        pltpu.VMEM((tm, D_MODEL), F32)],
        compiler_params=_cparams(2),
        name="conv_out",
    )(u, hist, w_dw, vec(b_dw), vec(ln_g), vec(ln_b), w_out, vec(b_out), x, g_norm, mod)


def _ffn_kernel(*refs, S, L, nb, prompt):
    if prompt:
        (x_ref, g2_ref, sc_ref, sh_ref, wg_ref, wv_ref, wdw_ref, bdw_ref, wd_ref, g3_ref, ga_ref,
         x2_ref, st_ref, h_ref, carry_ref) = refs
    else:
        (x_ref, g2_ref, sc_ref, sh_ref, wg_ref, wv_ref, wdw_ref, bdw_ref, wd_ref, g3_ref, ga_ref, hist_ref,
         x2_ref, st_ref, h_ref) = refs
    t = pl.program_id(1)
    j = pl.program_id(2)
    tm = x_ref.shape[0]
    rb = tm // nb
    seg = min(L, rb)
    pad = SUBLANES

    @pl.when(j == 0)
    def _():
        _norm_mod(x_ref, h_ref, g2_ref, sc_ref, sh_ref, S, L)
        x2_ref[...] = jnp.zeros(x2_ref.shape, F32)

    wg = wg_ref[...].astype(BF16)
    wv = wv_ref[...].astype(BF16)
    wd = wd_ref[...].astype(BF16)
    w0 = wdw_ref[0:1, :]
    w1 = wdw_ref[1:2, :]
    w2 = wdw_ref[2:3, :]
    bd = bdw_ref[...]
    tail = None
    for blk in range(nb):
        rows = slice(blk * rb, (blk + 1) * rb)
        hb = h_ref[rows, :]
        g = _dot(hb, wg)
        v = _dot(hb, wv)
        p1s, p2s = [], []
        for q in range(rb // seg):
            gq = g[q * seg:(q + 1) * seg, :]
            if not prompt:
                s = blk * (rb // seg) + q
                hist = hist_ref[s]
                st_ref[s] = gq[seg - (FFN_K - 1):, :]
            elif blk == 0:
                hist = jnp.where(t == 0, 0.0, carry_ref[j])
            else:
                hist = tail
            gfull = jnp.concatenate([hist, gq], axis=0)
            p1s.append(pltpu.roll(gfull, 1, axis=0)[pad:, :])
            p2s.append(pltpu.roll(gfull, 2, axis=0)[pad:, :])
            tail = gq[seg - pad:, :]
        p1 = p1s[0] if len(p1s) == 1 else jnp.concatenate(p1s, axis=0)
        p2 = p2s[0] if len(p2s) == 1 else jnp.concatenate(p2s, axis=0)
        gc = g * w2 + p1 * w1 + p2 * w0 + bd
        act = (jax.nn.gelu(gc) * v).astype(BF16)
        x2_ref[rows, :] += _dot(act, wd)
    if prompt:
        carry_ref[j] = tail
        st_ref[0] = tail[pad - (FFN_K - 1):, :]

    @pl.when(j == pl.num_programs(2) - 1)
    def _():
        _residual(x2_ref, x_ref, g3_ref, ga_ref, x2_ref, S, L)


def _ffn(x, hist, mod, g_norm, w_up, w_dw, b_dw, w_down, *, layer, S, L, tm, tf, nb):
    Bp, T, _ = x.shape
    J = D_FF // tf
    nT = T // tm
    prompt = hist is None
    x_spec = pl.BlockSpec((None, tm, D_MODEL), lambda b, t, j: (b, t, 0), pipeline_mode=pl.Buffered(1))
    in_specs = [x_spec, _vec_spec(4 * layer + 2), _mod_spec(S, layer, 4), _mod_spec(S, layer, 3),
                pl.BlockSpec((None, D_MODEL, tf), lambda b, t, j: (layer, 0, j)),
                pl.BlockSpec((None, D_MODEL, tf), lambda b, t, j: (layer, 0, j + J)),
                pl.BlockSpec((None, FFN_K, tf), lambda b, t, j: (layer, 0, j)),
                pl.BlockSpec((None, 1, tf), lambda b, t, j: (layer, 0, j)),
                pl.BlockSpec((None, tf, D_MODEL), lambda b, t, j: (layer, j, 0)),
                _vec_spec(4 * layer + 3), _mod_spec(S, layer, 5)]
    args = [x, g_norm, mod, mod, w_up, w_up, w_dw, b_dw.reshape(DEPTH, 1, D_FF), w_down, g_norm, mod]
    scratch = [pltpu.VMEM((tm, D_MODEL), BF16)]
    if prompt:
        scratch.append(pltpu.VMEM((J, SUBLANES, tf), F32))
    else:
        in_specs.append(pl.BlockSpec((S, SUBLANES, tf), lambda b, t, j: (0, 0, j)))
        args.append(hist)
    return pl.pallas_call(
        functools.partial(_ffn_kernel, S=S, L=L, nb=nb, prompt=prompt),
        grid=(Bp, nT, J),
        in_specs=in_specs,
        out_specs=[_row_spec(tm),
                   pl.BlockSpec((None, None, S, FFN_K - 1, tf), lambda b, t, j: (b, t, 0, 0, j))],
        out_shape=[jax.ShapeDtypeStruct((Bp, T, D_MODEL), F32),
                   jax.ShapeDtypeStruct((Bp, nT, S, FFN_K - 1, D_FF), F32)],
        scratch_shapes=scratch,
        compiler_params=pltpu.CompilerParams(dimension_semantics=("arbitrary",) * 3,
                                             vmem_limit_bytes=VMEM_LIMIT_FFN),
        name="ffn",
    )(*args)


def _qkv_kernel(x_ref, g_ref, sc_ref, sh_ref, wq_ref, wk_ref, wv_ref, q_ref, k_ref, v_ref, h_ref, *, S, L):
    _norm_mod(x_ref, h_ref, g_ref, sc_ref, sh_ref, S, L)
    h = h_ref[...]
    q_ref[...] = _dot(h, wq_ref[...])
    k_ref[...] = _dot(h, wk_ref[...])
    v_ref[...] = _dot(h, wv_ref[...])


def _qkv(x, mod, g_norm, w_q, w_k, w_v, *, layer, S, L, tm):
    Bp, T, _ = x.shape
    kv_spec = pl.BlockSpec((None, tm, KV_DIM), lambda b, t: (b, t, 0))
    resident = lambda n: pl.BlockSpec((None, D_MODEL, n), lambda b, t: (0, 0, 0), pipeline_mode=pl.Buffered(1))
    return pl.pallas_call(
        functools.partial(_qkv_kernel, S=S, L=L),
        grid=(Bp, T // tm),
        in_specs=[_row_spec(tm), _vec_spec(4 * layer + 0), _mod_spec(S, layer, 1), _mod_spec(S, layer, 0),
                  resident(D_MODEL), resident(KV_DIM), resident(KV_DIM)],
        out_specs=[_row_spec(tm), kv_spec, kv_spec],
        out_shape=[jax.ShapeDtypeStruct((Bp, T, D_MODEL), F32),
                   jax.ShapeDtypeStruct((Bp, T, KV_DIM), F32),
                   jax.ShapeDtypeStruct((Bp, T, KV_DIM), F32)],
        scratch_shapes=[pltpu.VMEM((tm, D_MODEL), F32)],
        compiler_params=_cparams(2),
        name="qkv",
    )(x, g_norm, mod, mod, w_q, w_k, w_v)


def _bias_kernel(rb_ref, bucket_ref, o_ref):
    bk = bucket_ref[...]
    for h in range(N_HEADS):
        acc = jnp.full(bk.shape, NEG_INF, F32)
        for b in range(N_BUCKETS):
            acc = jnp.where(bk == b, rb_ref[b, h], acc)
        o_ref[h] = acc


def _t5_bucket(rel):
    half = N_BUCKETS // 2
    max_exact = half // 2
    n = jnp.abs(rel)
    ret = jnp.where(rel > 0, half, 0)
    nf = jnp.maximum(n, 1).astype(F32)
    large = max_exact + (jnp.log(nf / max_exact) / math.log(MAX_DISTANCE / max_exact)
                         * (half - max_exact)).astype(jnp.int32)
    large = jnp.minimum(large, half - 1)
    return ret + jnp.where(n < max_exact, n, large)


def _bias_table(rel_bias, Qn, lo, hi):
    k = jnp.arange(KEY_WIN)[:, None]
    q = jnp.arange(Qn)[None, :]
    bucket = jnp.where((k >= lo(q)) & (k < hi(q)), _t5_bucket(k - WINDOW - q), -1).astype(jnp.int32)
    table = pl.pallas_call(
        _bias_kernel,
        in_specs=[pl.BlockSpec(memory_space=pltpu.SMEM),
                  pl.BlockSpec((KEY_WIN, Qn), lambda: (0, 0))],
        out_specs=pl.BlockSpec((N_HEADS, KEY_WIN, Qn), lambda: (0, 0, 0)),
        out_shape=jax.ShapeDtypeStruct((N_HEADS, KEY_WIN, Qn), F32),
        name="rel_bias_table",
    )(rel_bias, bucket)
    table = table.reshape(N_KV_HEADS, GROUP, KEY_WIN, Qn).transpose(0, 2, 1, 3)
    return table.reshape(N_KV_HEADS, KEY_WIN, GROUP * Qn)


def _attend(q_ref, r0, Qn, kw, vw, bias_ref, sink_ref, valid, o_ref, qs_ref):
    rows = pl.ds(r0, Qn)
    for g in range(GROUP):
        for h in range(N_KV_HEADS):
            src = (h * GROUP + g) * HEAD_DIM
            qs_ref[g * Qn:(g + 1) * Qn, h * HEAD_DIM:(h + 1) * HEAD_DIM] = q_ref[rows, src:src + HEAD_DIM]
    qt = qs_ref[...].T
    vt = vw.T
    outs = []
    for h in range(N_KV_HEADS):
        hd = slice(h * HEAD_DIM, (h + 1) * HEAD_DIM)
        s = _dot(kw[:, hd], qt[hd, :]) * ATTN_SCALE + bias_ref[h]
        if valid is not None:
            s = jnp.where(valid, s, NEG_INF)
        sk = sink_ref[h]
        mx = jnp.maximum(jnp.max(s, axis=0, keepdims=True), sk)
        p = jnp.exp(s - mx)
        den = jnp.sum(p, axis=0, keepdims=True) + jnp.exp(sk - mx)
        outs.append(_dot(vt[hd, :], p) * (1.0 / den))
    o2 = jnp.concatenate(outs, axis=0).T
    for g in range(GROUP):
        for h in range(N_KV_HEADS):
            dst = (h * GROUP + g) * HEAD_DIM
            o_ref[rows, dst:dst + HEAD_DIM] = o2[g * Qn:(g + 1) * Qn, h * HEAD_DIM:(h + 1) * HEAD_DIM]


def _attn_prompt_kernel(q_ref, kp_ref, kc_ref, vp_ref, vc_ref, bias_ref, sink_ref, o_ref, kw_ref, vw_ref, qs_ref,
                        *, tq, Qn):
    t = pl.program_id(1)
    kw_ref[0:WINDOW, :] = kp_ref[...]
    kw_ref[WINDOW:WINDOW + tq, :] = kc_ref[...]
    vw_ref[0:WINDOW, :] = vp_ref[...]
    vw_ref[WINDOW:WINDOW + tq, :] = vc_ref[...]

    def body(m, carry):
        r0 = pl.multiple_of(m * Qn, Qn)
        kw = kw_ref[pl.ds(r0, KEY_WIN), :]
        vw = vw_ref[pl.ds(r0, KEY_WIN), :]
        pos = t * tq + r0 - WINDOW + lax.broadcasted_iota(jnp.int32, (KEY_WIN, 1), 0)
        _attend(q_ref, r0, Qn, kw, vw, bias_ref, sink_ref, pos >= 0, o_ref, qs_ref)
        return carry

    lax.fori_loop(0, tq // Qn, body, 0)


def _attn_sample_kernel(q_ref, ck_ref, kn_ref, cv_ref, vn_ref, bias_ref, sink_ref, o_ref, kw_ref, vw_ref, qs_ref,
                        *, S, L):
    P = ck_ref.shape[1]
    kw_ref[:, 0:P, :] = ck_ref[...]
    kw_ref[:, P:P + L, :] = kn_ref[...].reshape(S, L, KV_DIM)
    kw_ref[:, P + L:, :] = jnp.zeros((S, KEY_WIN - P - L, KV_DIM), F32)
    vw_ref[:, 0:P, :] = cv_ref[...]
    vw_ref[:, P:P + L, :] = vn_ref[...].reshape(S, L, KV_DIM)
    vw_ref[:, P + L:, :] = jnp.zeros((S, KEY_WIN - P - L, KV_DIM), F32)

    def body(s, carry):
        r0 = pl.multiple_of(s * L, L)
        _attend(q_ref, r0, L, kw_ref[s], vw_ref[s], bias_ref, sink_ref, None, o_ref, qs_ref)
        return carry

    lax.fori_loop(0, S, body, 0)


def _attn_prompt(q, k, v, bias, sink, *, tq):
    Bp, T, _ = q.shape
    Qn = 2 * CHUNK
    per = tq // WINDOW
    prev_spec = pl.BlockSpec((None, WINDOW, KV_DIM), lambda b, t: (b, jnp.maximum(t * per - 1, 0), 0))
    cur_spec = pl.BlockSpec((None, tq, KV_DIM), lambda b, t: (b, t, 0))
    return pl.pallas_call(
        functools.partial(_attn_prompt_kernel, tq=tq, Qn=Qn),
        grid=(Bp, T // tq),
        in_specs=[_row_spec(tq), prev_spec, cur_spec, prev_spec, cur_spec,
                  pl.BlockSpec((N_KV_HEADS, KEY_WIN, GROUP * Qn), lambda b, t: (0, 0, 0)),
                  pl.BlockSpec((N_KV_HEADS, 1, GROUP * Qn), lambda b, t: (0, 0, 0))],
        out_specs=_row_spec(tq),
        out_shape=jax.ShapeDtypeStruct((Bp, T, D_MODEL), F32),
        scratch_shapes=[pltpu.VMEM((WINDOW + tq, KV_DIM), F32), pltpu.VMEM((WINDOW + tq, KV_DIM), F32),
                        pltpu.VMEM((GROUP * Qn, KV_DIM), F32)],
        compiler_params=_cparams(2),
        name="attn_prompt",
    )(q, k, k, v, v, bias, sink)


def _attn_sample(q, k, v, cache_k, cache_v, bias, sink, *, S, L):
    return pl.pallas_call(
        functools.partial(_attn_sample_kernel, S=S, L=L),
        out_shape=jax.ShapeDtypeStruct((S * L, D_MODEL), F32),
        scratch_shapes=[pltpu.VMEM((S, KEY_WIN, KV_DIM), F32), pltpu.VMEM((S, KEY_WIN, KV_DIM), F32),
                        pltpu.VMEM((GROUP * L, KV_DIM), F32)],
        compiler_params=pltpu.CompilerParams(vmem_limit_bytes=VMEM_LIMIT),
        name="attn_sample",
    )(q, cache_k, k, cache_v, v, bias, sink)


def _attn_out_kernel(o_ref, w_ref, x_ref, g_ref, ga_ref, x1_ref, *, S, L):
    x1_ref[...] = _dot(o_ref[...], w_ref[...])
    _residual(x1_ref, x_ref, g_ref, ga_ref, x1_ref, S, L)


def _attn_out(o, x, mod, g_norm, w_o, *, layer, S, L, tm):
    Bp, T, _ = x.shape
    return pl.pallas_call(
        functools.partial(_attn_out_kernel, S=S, L=L),
        grid=(Bp, T // tm),
        in_specs=[_row_spec(tm),
                  pl.BlockSpec((None, D_MODEL, D_MODEL), lambda b, t: (0, 0, 0), pipeline_mode=pl.Buffered(1)),
                  _row_spec(tm), _vec_spec(4 * layer + 1), _mod_spec(S, layer, 2)],
        out_specs=_row_spec(tm),
        out_shape=jax.ShapeDtypeStruct((Bp, T, D_MODEL), F32),
        compiler_params=_cparams(2),
        name="attn_out",
    )(o, w_o, x, g_norm, mod)


def kernel(x_prompt, x_sample, c_prompt, c_sample, cache_conv, cache_k, cache_v, cache_ffn, w_mod, b_mod, g_norm, conv_w_in, conv_b_in, conv_w_dw, conv_b_dw, conv_ln_g, conv_ln_b, conv_w_out, conv_b_out, attn_w_q, attn_w_k, attn_w_v, attn_w_o, attn_sinks, rel_bias, ffn_w_up, ffn_w_dw, ffn_b_dw, ffn_w_down):
    B, T, D = x_prompt.shape
    SB, SL, _ = x_sample.shape
    assert SB == MOD_ROWS_SAMPLE
    c_all = jnp.concatenate(
        [c_sample, jnp.pad(c_prompt[:, None, :], ((0, 0), (0, SUBLANES - 1), (0, 0))).reshape(B * SUBLANES, D)], axis=0)
    mod = _modulation(c_all, w_mod, b_mod)
    gn = g_norm.reshape(DEPTH * 4, 1, D)

    geo_p = dict(S=1, L=1024, tm=1024)
    geo_p_small = dict(S=1, L=512, tm=512)
    geo_s = dict(S=SB, L=SL, tm=SB * SL)

    xp = x_prompt
    xs = x_sample.reshape(1, SB * SL, D)

    yp, tail_p = _conv_mix(xp, mod, gn, conv_w_in, conv_b_in, conv_w_dw, conv_b_dw, layer=0, tm=1024, R=64, nb=2)
    xp = _conv_proj(yp, xp, mod, gn, conv_ln_g, conv_ln_b, conv_w_out, conv_b_out, layer=0, tm=512)
    us = _conv_in(xs, mod, gn, conv_w_in, conv_b_in, layer=0, **geo_s)
    conv_args = (conv_w_dw, conv_b_dw, conv_ln_g, conv_ln_b, conv_w_out, conv_b_out)
    xs = _conv_out(us, cache_conv[0], xs, mod, gn, *conv_args, layer=0, R=SL, **geo_s)
    conv_state_p = tail_p[:, -1, CONV_HIST - (CONV_K - 1):, :][None]
    conv_state_s = us.reshape(SB, SL, D)[:, SL - (CONV_K - 1):, :][None]

    ffn_p, ffn_s = [], []
    ffn_hist = jnp.pad(cache_ffn, ((0, 0), (0, 0), (SUBLANES - (FFN_K - 1), 0), (0, 0)))

    def run_ffn(i, xp, xs):
        w = (ffn_w_up, ffn_w_dw, ffn_b_dw, ffn_w_down)
        xp, st_p = _ffn(xp, None, mod, gn, *w, layer=i, tf=512, nb=2, **geo_p)
        xs, st_s = _ffn(xs, ffn_hist[i], mod, gn, *w, layer=i, tf=256, nb=2, **geo_s)
        ffn_p.append(st_p[:, -1, 0])
        ffn_s.append(st_s[0, 0])
        return xp, xs

    xp, xs = run_ffn(0, xp, xs)

    qp, kp, vp = _qkv(xp, mod, gn, attn_w_q, attn_w_k, attn_w_v, layer=1, **geo_p_small)
    qs, ks, vs = _qkv(xs, mod, gn, attn_w_q, attn_w_k, attn_w_v, layer=1, **geo_s)
    P = cache_k.shape[2]
    assert P == WINDOW and P + SL <= KEY_WIN
    bias_p = _bias_table(rel_bias, 2 * CHUNK,
                         lambda q: (q // CHUNK) * CHUNK, lambda q: (q // CHUNK) * CHUNK + WINDOW + CHUNK)
    bias_s = _bias_table(rel_bias, SL, lambda q: 0 * q, lambda q: 0 * q + P + SL)
    sink_p = jnp.repeat(attn_sinks[0], 2 * CHUNK).reshape(N_KV_HEADS, 1, GROUP * 2 * CHUNK)
    sink_s = jnp.repeat(attn_sinks[0], SL).reshape(N_KV_HEADS, 1, GROUP * SL)
    op = _attn_prompt(qp, kp, vp, bias_p, sink_p, tq=256)
    ck = cache_k[0].reshape(SB, P, KV_DIM)
    cv = cache_v[0].reshape(SB, P, KV_DIM)
    os_ = _attn_sample(qs[0], ks[0], vs[0], ck, cv, bias_s, sink_s, S=SB, L=SL)[None]
    xp = _attn_out(op, xp, mod, gn, attn_w_o, layer=1, **geo_p_small)
    xs = _attn_out(os_, xs, mod, gn, attn_w_o, layer=1, **geo_s)
    keep = min(WINDOW, T)
    k_state_p = kp[:, T - keep:].reshape(B, keep, N_KV_HEADS, HEAD_DIM)[None]
    v_state_p = vp[:, T - keep:].reshape(B, keep, N_KV_HEADS, HEAD_DIM)[None]
    k_state_s = jnp.concatenate([ck, ks.reshape(SB, SL, KV_DIM)], axis=1)[:, SL:]
    v_state_s = jnp.concatenate([cv, vs.reshape(SB, SL, KV_DIM)], axis=1)[:, SL:]
    k_state_s = k_state_s.reshape(SB, P, N_KV_HEADS, HEAD_DIM)[None]
    v_state_s = v_state_s.reshape(SB, P, N_KV_HEADS, HEAD_DIM)[None]

    xp, xs = run_ffn(1, xp, xs)

    return (xp, xs.reshape(SB, SL, D), conv_state_p, conv_state_s,
            k_state_p, v_state_p, k_state_s, v_state_s,
            jnp.stack(ffn_p), jnp.stack(ffn_s))
```

```python
import functools
import math

import jax
import jax.numpy as jnp
from jax import lax
from jax.experimental import pallas as pl
from jax.experimental.pallas import tpu as pltpu

F32 = jnp.float32
BF16 = jnp.bfloat16

D_MODEL = 2048
D_FF = 5632
DEPTH = 2
CONV_K = 31
FFN_K = 3
N_HEADS = 32
N_KV_HEADS = 4
GROUP = N_HEADS // N_KV_HEADS
HEAD_DIM = 64
KV_DIM = N_KV_HEADS * HEAD_DIM
CHUNK = 64
WINDOW = 128
N_BUCKETS = 32
MAX_DISTANCE = 128
EPS = 1e-6
NEG_INF = -1e30
ATTN_SCALE = HEAD_DIM ** -0.5

SUBLANES = 8
LANES = 128
CONV_HIST = 32
KEY_WIN = 2 * WINDOW
VMEM_LIMIT = 58 * 1024 * 1024
VMEM_LIMIT_FFN = 62 * 1024 * 1024
MOD_ROWS_SAMPLE = 16


def _cparams(n_grid):
    return pltpu.CompilerParams(dimension_semantics=("arbitrary",) * n_grid,
                                vmem_limit_bytes=VMEM_LIMIT)


def _dot(a, b):
    return jnp.dot(a, b, preferred_element_type=F32)


ROW_SLAB = 64
ROW_GROUP = 2
PROJ_BLOCKS = 2


def _row_loop(S, L, srcs, dst, fn):
    slab = min(L, ROW_SLAB)
    n = L // slab
    group = ROW_GROUP if n % ROW_GROUP == 0 else 1
    for s in range(S):

        def body(r, c, s=s):
            rows = [pl.ds(pl.multiple_of(s * L + (r * group + i) * slab, slab), slab) for i in range(group)]
            vals = [[src[rw, :] for src in srcs] for rw in rows]
            outs = [fn(s, *v) for v in vals]
            for rw, out in zip(rows, outs):
                dst[rw, :] = out.astype(dst.dtype)
            return c

        lax.fori_loop(0, n // group, body, 0)


def _norm_mod(x_ref, dst_ref, g_ref, sc_ref, sh_ref, S, L):
    def fn(s, x):
        inv = lax.rsqrt(jnp.mean(x * x, axis=-1, keepdims=True) + EPS)
        mul = g_ref[...] * (1.0 + sc_ref[s:s + 1, :])
        return x * inv * mul + sh_ref[s:s + 1, :]

    _row_loop(S, L, [x_ref], dst_ref, fn)


def _residual(m_ref, x_ref, g_ref, ga_ref, out_ref, S, L):
    def fn(s, m, x):
        inv = lax.rsqrt(jnp.mean(m * m, axis=-1, keepdims=True) + EPS)
        return x + m * inv * (g_ref[...] * ga_ref[s:s + 1, :])

    _row_loop(S, L, [m_ref, x_ref], out_ref, fn)


def _row_spec(tm):
    return pl.BlockSpec((None, tm, D_MODEL), lambda b, t, *_: (b, t, 0))


def _vec_spec(index, n=D_MODEL):
    return pl.BlockSpec((None, 1, n), lambda *_: (index, 0, 0))


def _mod_spec(S, layer, which):
    if S == 1:
        first = MOD_ROWS_SAMPLE // SUBLANES
        return pl.BlockSpec((None, None, SUBLANES, D_MODEL), lambda b, t, *_: (layer, which, first + b, 0))
    return pl.BlockSpec((None, None, S, D_MODEL), lambda b, t, *_: (layer, which, 0, 0))


def _mod_kernel(c_ref, w_ref, b_ref, o_ref):
    c = c_ref[...]
    o_ref[...] = _dot(c * jax.nn.sigmoid(c), w_ref[...]) + b_ref[...]


def _modulation(c_all, w_mod, b_mod):
    rows = c_all.shape[0]
    tn = 1024
    per = D_MODEL // tn
    return pl.pallas_call(
        _mod_kernel,
        grid=(DEPTH, 6 * per),
        in_specs=[pl.BlockSpec((rows, D_MODEL), lambda i, j: (0, 0)),
                  pl.BlockSpec((None, D_MODEL, tn), lambda i, j: (i, 0, j)),
                  pl.BlockSpec((None, 1, tn), lambda i, j: (i, 0, j))],
        out_specs=pl.BlockSpec((None, None, rows, tn), lambda i, j: (i, j // per, 0, j % per)),
        out_shape=jax.ShapeDtypeStruct((DEPTH, 6, rows, D_MODEL), F32),
        compiler_params=_cparams(2),
        name="modulation",
    )(c_all, w_mod, b_mod.reshape(DEPTH, 1, 6 * D_MODEL))


def _conv_in_kernel(x_ref, g_ref, sc_ref, sh_ref, wa_ref, wg_ref, ba_ref, bg_ref, u_ref, h_ref, *, S, L):
    @pl.when(pl.program_id(2) == 0)
    def _():
        _norm_mod(x_ref, h_ref, g_ref, sc_ref, sh_ref, S, L)

    h = h_ref[...]
    a = _dot(h, wa_ref[...]) + ba_ref[...]
    g = _dot(h, wg_ref[...]) + bg_ref[...]
    u_ref[...] = a * jax.nn.sigmoid(g)


def _conv_in(x, mod, g_norm, w_in, b_in, *, layer, S, L, tm):
    Bp, T, _ = x.shape
    tn = 512
    J = D_MODEL // tn
    b2 = b_in.reshape(-1, 1, 2 * D_MODEL)
    return pl.pallas_call(
        functools.partial(_conv_in_kernel, S=S, L=L),
        grid=(Bp, T // tm, J),
        in_specs=[_row_spec(tm), _vec_spec(4 * layer + 0), _mod_spec(S, layer, 1), _mod_spec(S, layer, 0),
                  pl.BlockSpec((None, D_MODEL, tn), lambda b, t, j: (0, 0, j)),
                  pl.BlockSpec((None, D_MODEL, tn), lambda b, t, j: (0, 0, j + J)),
                  pl.BlockSpec((None, 1, tn), lambda b, t, j: (0, 0, j)),
                  pl.BlockSpec((None, 1, tn), lambda b, t, j: (0, 0, j + J))],
        out_specs=pl.BlockSpec((None, tm, tn), lambda b, t, j: (b, t, j)),
        out_shape=jax.ShapeDtypeStruct((Bp, T, D_MODEL), F32),
        scratch_shapes=[pltpu.VMEM((tm, D_MODEL), BF16)],
        compiler_params=_cparams(3),
        name="conv_in",
    )(x, g_norm, mod, mod, w_in, w_in, b2, b2)


def _dwconv_block(win, wb_ref, bias, cols, R):
    n_groups = (CONV_K + 1 + SUBLANES) // SUBLANES
    win_rows = R + CONV_HIST
    acc = jnp.broadcast_to(bias, (R // SUBLANES, SUBLANES, LANES))
    for sh in range(SUBLANES):
        wsh = win if sh == 0 else pltpu.roll(win, win_rows - sh, axis=0)
        for a in range(n_groups):
            k = SUBLANES * a + sh - 2
            if 0 <= k < CONV_K:
                tap = wsh[SUBLANES * a:SUBLANES * a + R, :]
                acc = acc + tap.reshape(R // SUBLANES, SUBLANES, LANES) * wb_ref[k, :, cols]
    return acc.reshape(R, LANES)


def _fill_tap_table(wdw_ref, wb_ref):
    for k in range(CONV_K):
        wb_ref[k] = jnp.broadcast_to(wdw_ref[k:k + 1, :], (SUBLANES, D_MODEL))


def _conv_mix_kernel(x_ref, g_ref, sc_ref, sh_ref, wa_ref, wg_ref, ba_ref, bg_ref, wdw_ref, bdw_ref,
                     y_ref, tail_ref, h_ref, wb_ref, carry_ref, *, L, R, nb):
    t = pl.program_id(1)
    j = pl.program_id(2)
    tn = wa_ref.shape[1]
    rb = L // nb

    @pl.when((pl.program_id(0) == 0) & (t == 0) & (j == 0))
    def _():
        _fill_tap_table(wdw_ref, wb_ref)
        carry_ref[...] = jnp.zeros(carry_ref.shape, F32)

    @pl.when(j == 0)
    def _():
        _norm_mod(x_ref, h_ref, g_ref, sc_ref, sh_ref, 1, L)

    wa = wa_ref[...].astype(BF16)
    wg = wg_ref[...].astype(BF16)
    tail = jnp.where(t == 0, 0.0, carry_ref[j])
    for blk in range(nb):
        hb = h_ref[blk * rb:(blk + 1) * rb, :]
        a = _dot(hb, wa) + ba_ref[...]
        g = _dot(hb, wg) + bg_ref[...]
        u = a * jax.nn.sigmoid(g)
        ufull = jnp.concatenate([tail, u], axis=0)
        for cb in range(tn // LANES):
            gcols = pl.ds(pl.multiple_of(j * tn + cb * LANES, LANES), LANES)
            lcols = slice(cb * LANES, (cb + 1) * LANES)
            bias = bdw_ref[:, gcols]
            for r in range(rb // R):
                win = ufull[r * R:r * R + R + CONV_HIST, lcols]
                y_ref[blk * rb + r * R:blk * rb + (r + 1) * R, lcols] = _dwconv_block(win, wb_ref, bias, gcols, R)
        tail = u[rb - CONV_HIST:, :]
    carry_ref[j] = tail
    tail_ref[...] = tail


def _conv_mix(x, mod, g_norm, w_in, b_in, w_dw, b_dw, *, layer, tm, R, nb):
    Bp, T, _ = x.shape
    tn = 256
    J = D_MODEL // tn
    b2 = b_in.reshape(-1, 1, 2 * D_MODEL)
    return pl.pallas_call(
        functools.partial(_conv_mix_kernel, L=tm, R=R, nb=nb),
        grid=(Bp, T // tm, J),
        in_specs=[_row_spec(tm), _vec_spec(4 * layer + 0), _mod_spec(1, layer, 1), _mod_spec(1, layer, 0),
                  pl.BlockSpec((None, D_MODEL, tn), lambda b, t, j: (0, 0, j)),
                  pl.BlockSpec((None, D_MODEL, tn), lambda b, t, j: (0, 0, j + J)),
                  pl.BlockSpec((None, 1, tn), lambda b, t, j: (0, 0, j)),
                  pl.BlockSpec((None, 1, tn), lambda b, t, j: (0, 0, j + J)),
                  pl.BlockSpec((None, CONV_K, D_MODEL), lambda b, t, j: (0, 0, 0)),
                  _vec_spec(0)],
        out_specs=[pl.BlockSpec((None, tm, tn), lambda b, t, j: (b, t, j)),
                   pl.BlockSpec((None, None, CONV_HIST, tn), lambda b, t, j: (b, t, 0, j))],
        out_shape=[jax.ShapeDtypeStruct((Bp, T, D_MODEL), F32),
                   jax.ShapeDtypeStruct((Bp, T // tm, CONV_HIST, D_MODEL), F32)],
        scratch_shapes=[pltpu.VMEM((tm, D_MODEL), BF16),
                        pltpu.VMEM((CONV_K, SUBLANES, D_MODEL), F32),
                        pltpu.VMEM((J, CONV_HIST, tn), F32)],
        compiler_params=_cparams(3),
        name="conv_mix",
    )(x, g_norm, mod, mod, w_in, w_in, b2, b2, w_dw, b_dw.reshape(-1, 1, D_MODEL))


def _conv_proj_kernel(y_ref, lng_ref, lnb_ref, wout_ref, bout_ref, x_ref, g_ref, ga_ref, x1_ref, *, L):
    rb = L // PROJ_BLOCKS
    w = wout_ref[...]
    for blk in range(PROJ_BLOCKS):
        rows = slice(blk * rb, (blk + 1) * rb)
        a = _ln_silu_value(y_ref[rows, :], lng_ref, lnb_ref)
        m = _dot(a, w) + bout_ref[...]
        x1_ref[rows, :] = _residual_value(m, x_ref[rows, :], g_ref, ga_ref[0:1, :])


def _conv_proj(y, x, mod, g_norm, ln_g, ln_b, w_out, b_out, *, layer, tm):
    Bp, T, _ = x.shape
    vec = lambda a: a.reshape(-1, 1, D_MODEL)
    return pl.pallas_call(
        functools.partial(_conv_proj_kernel, L=tm),
        grid=(Bp, T // tm),
        in_specs=[_row_spec(tm), _vec_spec(0), _vec_spec(0),
                  pl.BlockSpec((None, D_MODEL, D_MODEL), lambda b, t: (0, 0, 0), pipeline_mode=pl.Buffered(1)),
                  _vec_spec(0), _row_spec(tm), _vec_spec(4 * layer + 1), _mod_spec(1, layer, 2)],
        out_specs=_row_spec(tm),
        out_shape=jax.ShapeDtypeStruct((Bp, T, D_MODEL), F32),
        compiler_params=_cparams(2),
        name="conv_proj",
    )(y, vec(ln_g), vec(ln_b), w_out, vec(b_out), x, g_norm, mod)


def _dwconv(full_ref, wb_ref, bdw_ref, y_ref, S, L, R):
    def body(cb, c):
        cols = pl.ds(pl.multiple_of(cb * LANES, LANES), LANES)
        bias = bdw_ref[:, cols]
        for s in range(S):
            for r in range(L // R):
                win = full_ref[s, r * R:r * R + R + CONV_HIST, cols]
                y_ref[s * L + r * R:s * L + (r + 1) * R, cols] = _dwconv_block(win, wb_ref, bias, cols, R)
        return c

    lax.fori_loop(0, D_MODEL // LANES, body, 0)


def _ln_silu_value(y, lng_ref, lnb_ref):
    mu = jnp.mean(y, axis=-1, keepdims=True)
    yc = y - mu
    var = jnp.mean(yc * yc, axis=-1, keepdims=True)
    z = yc * lax.rsqrt(var + EPS) * lng_ref[...] + lnb_ref[...]
    return z * jax.nn.sigmoid(z)


def _ln_silu(y_ref, lng_ref, lnb_ref, S, L):
    _row_loop(S, L, [y_ref], y_ref, lambda s, y: _ln_silu_value(y, lng_ref, lnb_ref))


def _conv_out_kernel(u_ref, hist_ref, wdw_ref, bdw_ref, lng_ref, lnb_ref, wout_ref, bout_ref, x_ref, g_ref, ga_ref,
                     x1_ref, full_ref, wb_ref, y_ref, *, S, L, R):
    _fill_tap_table(wdw_ref, wb_ref)
    pad = CONV_HIST - (CONV_K - 1)
    full_ref[:, 0:pad, :] = jnp.zeros((S, pad, D_MODEL), F32)
    full_ref[:, pad:CONV_HIST, :] = hist_ref[...]
    full_ref[:, CONV_HIST:CONV_HIST + L, :] = u_ref[...].reshape(S, L, D_MODEL)
    _dwconv(full_ref, wb_ref, bdw_ref, y_ref, S, L, R)
    _ln_silu(y_ref, lng_ref, lnb_ref, S, L)
    x1_ref[...] = _dot(y_ref[...], wout_ref[...]) + bout_ref[...]
    _residual(x1_ref, x_ref, g_ref, ga_ref, x1_ref, S, L)


def _conv_out(u, hist, x, mod, g_norm, w_dw, b_dw, ln_g, ln_b, w_out, b_out, *, layer, S, L, tm, R):
    Bp, T, _ = x.shape
    assert Bp == 1 and T == tm
    vec = lambda a: a.reshape(-1, 1, D_MODEL)
    return pl.pallas_call(
        functools.partial(_conv_out_kernel, S=S, L=L, R=R),
        grid=(Bp, T // tm),
        in_specs=[_row_spec(tm), pl.BlockSpec((S, CONV_K - 1, D_MODEL), lambda b, t: (0, 0, 0)),
                  pl.BlockSpec((None, CONV_K, D_MODEL), lambda b, t: (0, 0, 0)),
                  _vec_spec(0), _vec_spec(0), _vec_spec(0),
                  pl.BlockSpec((None, D_MODEL, D_MODEL), lambda b, t: (0, 0, 0), pipeline_mode=pl.Buffered(1)),
                  _vec_spec(0), _row_spec(tm), _vec_spec(4 * layer + 1), _mod_spec(S, layer, 2)],
        out_specs=_row_spec(tm),
        out_shape=jax.ShapeDtypeStruct((Bp, T, D_MODEL), F32),
        scratch_shapes=[pltpu.VMEM((S, CONV_HIST + L, D_MODEL), F32),
                        pltpu.VMEM((CONV_K, SUBLANES, D_MODEL), F32),
                        pltpu.VMEM((tm, D_MODEL), F32)],
        compiler_params=_cparams(2),
        name="conv_out",
    )(u, hist, w_dw, vec(b_dw), vec(ln_g), vec(ln_b), w_out, vec(b_out), x, g_norm, mod)


def _ffn_kernel(*refs, S, L, nb, prompt):
    if prompt:
        (x_ref, g2_ref, sc_ref, sh_ref, wg_ref, wv_ref, wdw_ref, bdw_ref, wd_ref, g3_ref, ga_ref,
         x2_ref, st_ref, h_ref, carry_ref) = refs
    else:
        (x_ref, g2_ref, sc_ref, sh_ref, wg_ref, wv_ref, wdw_ref, bdw_ref, wd_ref, g3_ref, ga_ref, hist_ref,
         x2_ref, st_ref, h_ref) = refs
    t = pl.program_id(1)
    j = pl.program_id(2)
    tm = x_ref.shape[0]
    rb = tm // nb
    seg = min(L, rb)
    pad = SUBLANES

    @pl.when(j == 0)
    def _():
        _norm_mod(x_ref, h_ref, g2_ref, sc_ref, sh_ref, S, L)
        x2_ref[...] = jnp.zeros(x2_ref.shape, F32)

    wg = wg_ref[...].astype(BF16)
    wv = wv_ref[...].astype(BF16)
    wd = wd_ref[...].astype(BF16)
    w0 = wdw_ref[0:1, :]
    w1 = wdw_ref[1:2, :]
    w2 = wdw_ref[2:3, :]
    bd = bdw_ref[...]
    tail = None
    for blk in range(nb):
        rows = slice(blk * rb, (blk + 1) * rb)
        hb = h_ref[rows, :]
        g = _dot(hb, wg)
        v = _dot(hb, wv)
        p1s, p2s = [], []
        for q in range(rb // seg):
            gq = g[q * seg:(q + 1) * seg, :]
            if not prompt:
                s = blk * (rb // seg) + q
                hist = hist_ref[s]
                st_ref[s] = gq[seg - (FFN_K - 1):, :]
            elif blk == 0:
                hist = jnp.where(t == 0, 0.0, carry_ref[j])
            else:
                hist = tail
            gfull = jnp.concatenate([hist, gq], axis=0)
            p1s.append(pltpu.roll(gfull, 1, axis=0)[pad:, :])
            p2s.append(pltpu.roll(gfull, 2, axis=0)[pad:, :])
            tail = gq[seg - pad:, :]
        p1 = p1s[0] if len(p1s) == 1 else jnp.concatenate(p1s, axis=0)
        p2 = p2s[0] if len(p2s) == 1 else jnp.concatenate(p2s, axis=0)
        gc = g * w2 + p1 * w1 + p2 * w0 + bd
        act = (jax.nn.gelu(gc) * v).astype(BF16)
        x2_ref[rows, :] += _dot(act, wd)
    if prompt:
        carry_ref[j] = tail
        st_ref[0] = tail[pad - (FFN_K - 1):, :]

    @pl.when(j == pl.num_programs(2) - 1)
    def _():
        _residual(x2_ref, x_ref, g3_ref, ga_ref, x2_ref, S, L)


def _ffn(x, hist, mod, g_norm, w_up, w_dw, b_dw, w_down, *, layer, S, L, tm, tf, nb):
    Bp, T, _ = x.shape
    J = D_FF // tf
    nT = T // tm
    prompt = hist is None
    x_spec = pl.BlockSpec((None, tm, D_MODEL), lambda b, t, j: (b, t, 0), pipeline_mode=pl.Buffered(1))
    in_specs = [x_spec, _vec_spec(4 * layer + 2), _mod_spec(S, layer, 4), _mod_spec(S, layer, 3),
                pl.BlockSpec((None, D_MODEL, tf), lambda b, t, j: (layer, 0, j)),
                pl.BlockSpec((None, D_MODEL, tf), lambda b, t, j: (layer, 0, j + J)),
                pl.BlockSpec((None, FFN_K, tf), lambda b, t, j: (layer, 0, j)),
                pl.BlockSpec((None, 1, tf), lambda b, t, j: (layer, 0, j)),
                pl.BlockSpec((None, tf, D_MODEL), lambda b, t, j: (layer, j, 0)),
                _vec_spec(4 * layer + 3), _mod_spec(S, layer, 5)]
    args = [x, g_norm, mod, mod, w_up, w_up, w_dw, b_dw.reshape(DEPTH, 1, D_FF), w_down, g_norm, mod]
    scratch = [pltpu.VMEM((tm, D_MODEL), BF16)]
    if prompt:
        scratch.append(pltpu.VMEM((J, SUBLANES, tf), F32))
    else:
        in_specs.append(pl.BlockSpec((S, SUBLANES, tf), lambda b, t, j: (0, 0, j)))
        args.append(hist)
    return pl.pallas_call(
        functools.partial(_ffn_kernel, S=S, L=L, nb=nb, prompt=prompt),
        grid=(Bp, nT, J),
        in_specs=in_specs,
        out_specs=[_row_spec(tm),
                   pl.BlockSpec((None, None, S, FFN_K - 1, tf), lambda b, t, j: (b, t, 0, 0, j))],
        out_shape=[jax.ShapeDtypeStruct((Bp, T, D_MODEL), F32),
                   jax.ShapeDtypeStruct((Bp, nT, S, FFN_K - 1, D_FF), F32)],
        scratch_shapes=scratch,
        compiler_params=pltpu.CompilerParams(dimension_semantics=("arbitrary",) * 3,
                                             vmem_limit_bytes=VMEM_LIMIT_FFN),
        name="ffn",
    )(*args)


def _qkv_kernel(x_ref, g_ref, sc_ref, sh_ref, wq_ref, wk_ref, wv_ref, q_ref, k_ref, v_ref, h_ref, *, S, L):
    tm = x_ref.shape[0]
    if S > 1:
        _norm_mod(x_ref, h_ref, g_ref, sc_ref, sh_ref, S, L)
        blocks = [(slice(0, tm), h_ref[...])]
    else:
        rb = tm // PROJ_BLOCKS
        mul = g_ref[...] * (1.0 + sc_ref[0:1, :])
        blocks = []
        for blk in range(PROJ_BLOCKS):
            rows = slice(blk * rb, (blk + 1) * rb)
            x = x_ref[rows, :]
            inv = lax.rsqrt(jnp.mean(x * x, axis=-1, keepdims=True) + EPS)
            blocks.append((rows, x * inv * mul + sh_ref[0:1, :]))
    wq = wq_ref[...]
    wk = wk_ref[...]
    wv = wv_ref[...]
    for rows, h in blocks:
        q_ref[rows, :] = _dot(h, wq)
        k_ref[rows, :] = _dot(h, wk)
        v_ref[rows, :] = _dot(h, wv)


def _qkv(x, mod, g_norm, w_q, w_k, w_v, *, layer, S, L, tm):
    Bp, T, _ = x.shape
    kv_spec = pl.BlockSpec((None, tm, KV_DIM), lambda b, t: (b, t, 0))
    resident = lambda n: pl.BlockSpec((None, D_MODEL, n), lambda b, t: (0, 0, 0), pipeline_mode=pl.Buffered(1))
    return pl.pallas_call(
        functools.partial(_qkv_kernel, S=S, L=L),
        grid=(Bp, T // tm),
        in_specs=[_row_spec(tm), _vec_spec(4 * layer + 0), _mod_spec(S, layer, 1), _mod_spec(S, layer, 0),
                  resident(D_MODEL), resident(KV_DIM), resident(KV_DIM)],
        out_specs=[_row_spec(tm), kv_spec, kv_spec],
        out_shape=[jax.ShapeDtypeStruct((Bp, T, D_MODEL), F32),
                   jax.ShapeDtypeStruct((Bp, T, KV_DIM), F32),
                   jax.ShapeDtypeStruct((Bp, T, KV_DIM), F32)],
        scratch_shapes=[pltpu.VMEM((tm, D_MODEL), F32)],
        compiler_params=_cparams(2),
        name="qkv",
    )(x, g_norm, mod, mod, w_q, w_k, w_v)


def _bias_kernel(rb_ref, bucket_ref, o_ref):
    bk = bucket_ref[...]
    for h in range(N_HEADS):
        acc = jnp.full(bk.shape, NEG_INF, F32)
        for b in range(N_BUCKETS):
            acc = jnp.where(bk == b, rb_ref[b, h], acc)
        o_ref[h] = acc


def _t5_bucket(rel):
    half = N_BUCKETS // 2
    max_exact = half // 2
    n = jnp.abs(rel)
    ret = jnp.where(rel > 0, half, 0)
    nf = jnp.maximum(n, 1).astype(F32)
    large = max_exact + (jnp.log(nf / max_exact) / math.log(MAX_DISTANCE / max_exact)
                         * (half - max_exact)).astype(jnp.int32)
    large = jnp.minimum(large, half - 1)
    return ret + jnp.where(n < max_exact, n, large)


def _bias_table(rel_bias, Qn, lo, hi):
    k = jnp.arange(KEY_WIN)[:, None]
    q = jnp.arange(Qn)[None, :]
    bucket = jnp.where((k >= lo(q)) & (k < hi(q)), _t5_bucket(k - WINDOW - q), -1).astype(jnp.int32)
    table = pl.pallas_call(
        _bias_kernel,
        in_specs=[pl.BlockSpec(memory_space=pltpu.SMEM),
                  pl.BlockSpec((KEY_WIN, Qn), lambda: (0, 0))],
        out_specs=pl.BlockSpec((N_HEADS, KEY_WIN, Qn), lambda: (0, 0, 0)),
        out_shape=jax.ShapeDtypeStruct((N_HEADS, KEY_WIN, Qn), F32),
        name="rel_bias_table",
    )(rel_bias, bucket)
    table = table.reshape(N_KV_HEADS, GROUP, KEY_WIN, Qn).transpose(0, 2, 1, 3)
    return table.reshape(N_KV_HEADS, KEY_WIN, GROUP * Qn)


def _attend(q_ref, r0, Qn, kw, vw, bias_ref, sel, sink_ref, o_ref, qs_ref):
    rows = pl.ds(r0, Qn)
    for g in range(GROUP):
        for h in range(N_KV_HEADS):
            src = (h * GROUP + g) * HEAD_DIM
            qs_ref[g * Qn:(g + 1) * Qn, h * HEAD_DIM:(h + 1) * HEAD_DIM] = q_ref[rows, src:src + HEAD_DIM] * ATTN_SCALE
    qt = qs_ref[...].T
    vt = vw.T
    outs = []
    for h in range(N_KV_HEADS):
        hd = slice(h * HEAD_DIM, (h + 1) * HEAD_DIM)
        s = _dot(kw[:, hd], qt[hd, :]) + bias_ref[sel, h]
        sk = sink_ref[h]
        mx = jnp.maximum(jnp.max(s, axis=0, keepdims=True), sk)
        p = jnp.exp(s - mx)
        den = jnp.sum(p, axis=0, keepdims=True) + jnp.exp(sk - mx)
        outs.append(_dot(vt[hd, :], p) * (1.0 / den))
    o2 = jnp.concatenate(outs, axis=0).T
    for g in range(GROUP):
        for h in range(N_KV_HEADS):
            dst = (h * GROUP + g) * HEAD_DIM
            o_ref[rows, dst:dst + HEAD_DIM] = o2[g * Qn:(g + 1) * Qn, h * HEAD_DIM:(h + 1) * HEAD_DIM]


def _attn_prompt_kernel(q_ref, kp_ref, kc_ref, vp_ref, vc_ref, bias_ref, sink_ref, o_ref, kw_ref, vw_ref, qs_ref,
                        *, tq, Qn):
    t = pl.program_id(1)
    kw_ref[0:WINDOW, :] = kp_ref[...]
    kw_ref[WINDOW:WINDOW + tq, :] = kc_ref[...]
    vw_ref[0:WINDOW, :] = vp_ref[...]
    vw_ref[WINDOW:WINDOW + tq, :] = vc_ref[...]

    def body(m, carry):
        r0 = pl.multiple_of(m * Qn, Qn)
        kw = kw_ref[pl.ds(r0, KEY_WIN), :]
        vw = vw_ref[pl.ds(r0, KEY_WIN), :]
        sel = ((t == 0) & (m == 0)).astype(jnp.int32)
        _attend(q_ref, r0, Qn, kw, vw, bias_ref, sel, sink_ref, o_ref, qs_ref)
        return carry

    lax.fori_loop(0, tq // Qn, body, 0)


def _attn_sample_kernel(q_ref, ck_ref, kn_ref, cv_ref, vn_ref, bias_ref, sink_ref, o_ref, kw_ref, vw_ref, qs_ref,
                        *, S, L):
    P = ck_ref.shape[1]
    kw_ref[:, 0:P, :] = ck_ref[...]
    kw_ref[:, P:P + L, :] = kn_ref[...].reshape(S, L, KV_DIM)
    kw_ref[:, P + L:, :] = jnp.zeros((S, KEY_WIN - P - L, KV_DIM), F32)
    vw_ref[:, 0:P, :] = cv_ref[...]
    vw_ref[:, P:P + L, :] = vn_ref[...].reshape(S, L, KV_DIM)
    vw_ref[:, P + L:, :] = jnp.zeros((S, KEY_WIN - P - L, KV_DIM), F32)

    def body(s, carry):
        r0 = pl.multiple_of(s * L, L)
        _attend(q_ref, r0, L, kw_ref[s], vw_ref[s], bias_ref, 0, sink_ref, o_ref, qs_ref)
        return carry

    lax.fori_loop(0, S, body, 0)


def _attn_prompt(q, k, v, bias, sink, *, tq):
    Bp, T, _ = q.shape
    Qn = 2 * CHUNK
    per = tq // WINDOW
    prev_spec = pl.BlockSpec((None, WINDOW, KV_DIM), lambda b, t: (b, jnp.maximum(t * per - 1, 0), 0))
    cur_spec = pl.BlockSpec((None, tq, KV_DIM), lambda b, t: (b, t, 0))
    return pl.pallas_call(
        functools.partial(_attn_prompt_kernel, tq=tq, Qn=Qn),
        grid=(Bp, T // tq),
        in_specs=[_row_spec(tq), prev_spec, cur_spec, prev_spec, cur_spec,
                  pl.BlockSpec((2, N_KV_HEADS, KEY_WIN, GROUP * Qn), lambda b, t: (0, 0, 0, 0),
                               pipeline_mode=pl.Buffered(1)),
                  pl.BlockSpec((N_KV_HEADS, 1, GROUP * Qn), lambda b, t: (0, 0, 0))],
        out_specs=_row_spec(tq),
        out_shape=jax.ShapeDtypeStruct((Bp, T, D_MODEL), F32),
        scratch_shapes=[pltpu.VMEM((WINDOW + tq, KV_DIM), F32), pltpu.VMEM((WINDOW + tq, KV_DIM), F32),
                        pltpu.VMEM((GROUP * Qn, KV_DIM), F32)],
        compiler_params=_cparams(2),
        name="attn_prompt",
    )(q, k, k, v, v, bias, sink)


def _attn_sample(q, k, v, cache_k, cache_v, bias, sink, *, S, L):
    return pl.pallas_call(
        functools.partial(_attn_sample_kernel, S=S, L=L),
        out_shape=jax.ShapeDtypeStruct((S * L, D_MODEL), F32),
        scratch_shapes=[pltpu.VMEM((S, KEY_WIN, KV_DIM), F32), pltpu.VMEM((S, KEY_WIN, KV_DIM), F32),
                        pltpu.VMEM((GROUP * L, KV_DIM), F32)],
        compiler_params=pltpu.CompilerParams(vmem_limit_bytes=VMEM_LIMIT),
        name="attn_sample",
    )(q, cache_k, k, cache_v, v, bias, sink)


def _residual_value(m, x, g_ref, ga):
    inv = lax.rsqrt(jnp.mean(m * m, axis=-1, keepdims=True) + EPS)
    return x + m * inv * (g_ref[...] * ga)


def _attn_out_kernel(o_ref, w_ref, x_ref, g_ref, ga_ref, x1_ref, *, S, L):
    tm = o_ref.shape[0]
    if S > 1:
        x1_ref[...] = _dot(o_ref[...], w_ref[...])
        _residual(x1_ref, x_ref, g_ref, ga_ref, x1_ref, S, L)
        return
    rb = tm // PROJ_BLOCKS
    w = w_ref[...]
    for blk in range(PROJ_BLOCKS):
        rows = slice(blk * rb, (blk + 1) * rb)
        x1_ref[rows, :] = _residual_value(_dot(o_ref[rows, :], w), x_ref[rows, :], g_ref, ga_ref[0:1, :])


def _attn_out(o, x, mod, g_norm, w_o, *, layer, S, L, tm):
    Bp, T, _ = x.shape
    return pl.pallas_call(
        functools.partial(_attn_out_kernel, S=S, L=L),
        grid=(Bp, T // tm),
        in_specs=[_row_spec(tm),
                  pl.BlockSpec((None, D_MODEL, D_MODEL), lambda b, t: (0, 0, 0), pipeline_mode=pl.Buffered(1)),
                  _row_spec(tm), _vec_spec(4 * layer + 1), _mod_spec(S, layer, 2)],
        out_specs=_row_spec(tm),
        out_shape=jax.ShapeDtypeStruct((Bp, T, D_MODEL), F32),
        compiler_params=_cparams(2),
        name="attn_out",
    )(o, w_o, x, g_norm, mod)


def kernel(x_prompt, x_sample, c_prompt, c_sample, cache_conv, cache_k, cache_v, cache_ffn, w_mod, b_mod, g_norm, conv_w_in, conv_b_in, conv_w_dw, conv_b_dw, conv_ln_g, conv_ln_b, conv_w_out, conv_b_out, attn_w_q, attn_w_k, attn_w_v, attn_w_o, attn_sinks, rel_bias, ffn_w_up, ffn_w_dw, ffn_b_dw, ffn_w_down):
    B, T, D = x_prompt.shape
    SB, SL, _ = x_sample.shape
    assert SB == MOD_ROWS_SAMPLE
    c_all = jnp.concatenate(
        [c_sample, jnp.pad(c_prompt[:, None, :], ((0, 0), (0, SUBLANES - 1), (0, 0))).reshape(B * SUBLANES, D)], axis=0)
    mod = _modulation(c_all, w_mod, b_mod)
    gn = g_norm.reshape(DEPTH * 4, 1, D)

    geo_p = dict(S=1, L=1024, tm=1024)
    geo_p_small = dict(S=1, L=512, tm=512)
    geo_s = dict(S=SB, L=SL, tm=SB * SL)

    xp = x_prompt
    xs = x_sample.reshape(1, SB * SL, D)

    yp, tail_p = _conv_mix(xp, mod, gn, conv_w_in, conv_b_in, conv_w_dw, conv_b_dw, layer=0, tm=1024, R=64, nb=2)
    xp = _conv_proj(yp, xp, mod, gn, conv_ln_g, conv_ln_b, conv_w_out, conv_b_out, layer=0, tm=512)
    us = _conv_in(xs, mod, gn, conv_w_in, conv_b_in, layer=0, **geo_s)
    conv_args = (conv_w_dw, conv_b_dw, conv_ln_g, conv_ln_b, conv_w_out, conv_b_out)
    xs = _conv_out(us, cache_conv[0], xs, mod, gn, *conv_args, layer=0, R=SL, **geo_s)
    conv_state_p = tail_p[:, -1, CONV_HIST - (CONV_K - 1):, :][None]
    conv_state_s = us.reshape(SB, SL, D)[:, SL - (CONV_K - 1):, :][None]

    ffn_p, ffn_s = [], []
    ffn_hist = jnp.pad(cache_ffn, ((0, 0), (0, 0), (SUBLANES - (FFN_K - 1), 0), (0, 0)))

    def run_ffn(i, xp, xs):
        w = (ffn_w_up, ffn_w_dw, ffn_b_dw, ffn_w_down)
        xp, st_p = _ffn(xp, None, mod, gn, *w, layer=i, tf=512, nb=2, **geo_p)
        xs, st_s = _ffn(xs, ffn_hist[i], mod, gn, *w, layer=i, tf=512, nb=2, **geo_s)
        ffn_p.append(st_p[:, -1, 0])
        ffn_s.append(st_s[0, 0])
        return xp, xs

    xp, xs = run_ffn(0, xp, xs)

    qp, kp, vp = _qkv(xp, mod, gn, attn_w_q, attn_w_k, attn_w_v, layer=1, **geo_p_small)
    qs, ks, vs = _qkv(xs, mod, gn, attn_w_q, attn_w_k, attn_w_v, layer=1, **geo_s)
    P = cache_k.shape[2]
    assert P == WINDOW and P + SL <= KEY_WIN
    bias_p = _bias_table(rel_bias, 2 * CHUNK,
                         lambda q: (q // CHUNK) * CHUNK, lambda q: (q // CHUNK) * CHUNK + WINDOW + CHUNK)
    before_start = (jnp.arange(KEY_WIN) < WINDOW)[None, :, None]
    bias_p = jnp.stack([bias_p, jnp.where(before_start, NEG_INF, bias_p)])
    bias_s = _bias_table(rel_bias, SL, lambda q: 0 * q, lambda q: 0 * q + P + SL)[None]
    sink_p = jnp.repeat(attn_sinks[0], 2 * CHUNK).reshape(N_KV_HEADS, 1, GROUP * 2 * CHUNK)
    sink_s = jnp.repeat(attn_sinks[0], SL).reshape(N_KV_HEADS, 1, GROUP * SL)
    op = _attn_prompt(qp, kp, vp, bias_p, sink_p, tq=512)
    ck = cache_k[0].reshape(SB, P, KV_DIM)
    cv = cache_v[0].reshape(SB, P, KV_DIM)
    os_ = _attn_sample(qs[0], ks[0], vs[0], ck, cv, bias_s, sink_s, S=SB, L=SL)[None]
    xp = _attn_out(op, xp, mod, gn, attn_w_o, layer=1, **geo_p_small)
    xs = _attn_out(os_, xs, mod, gn, attn_w_o, layer=1, **geo_s)
    keep = min(WINDOW, T)
    k_state_p = kp[:, T - keep:].reshape(B, keep, N_KV_HEADS, HEAD_DIM)[None]
    v_state_p = vp[:, T - keep:].reshape(B, keep, N_KV_HEADS, HEAD_DIM)[None]
    k_state_s = jnp.concatenate([ck, ks.reshape(SB, SL, KV_DIM)], axis=1)[:, SL:]
    v_state_s = jnp.concatenate([cv, vs.reshape(SB, SL, KV_DIM)], axis=1)[:, SL:]
    k_state_s = k_state_s.reshape(SB, P, N_KV_HEADS, HEAD_DIM)[None]
    v_state_s = v_state_s.reshape(SB, P, N_KV_HEADS, HEAD_DIM)[None]

    xp, xs = run_ffn(1, xp, xs)

    return (xp, xs.reshape(SB, SL, D), conv_state_p, conv_state_s,
            k_state_p, v_state_p, k_state_s, v_state_s,
            jnp.stack(ffn_p), jnp.stack(ffn_s))
```

```python
import functools
import math

import jax
import jax.numpy as jnp
from jax import lax
from jax.experimental import pallas as pl
from jax.experimental.pallas import tpu as pltpu

F32 = jnp.float32
BF16 = jnp.bfloat16

D_MODEL = 2048
D_FF = 5632
DEPTH = 2
CONV_K = 31
FFN_K = 3
N_HEADS = 32
N_KV_HEADS = 4
GROUP = N_HEADS // N_KV_HEADS
HEAD_DIM = 64
KV_DIM = N_KV_HEADS * HEAD_DIM
CHUNK = 64
WINDOW = 128
N_BUCKETS = 32
MAX_DISTANCE = 128
EPS = 1e-6
NEG_INF = -1e30
ATTN_SCALE = HEAD_DIM ** -0.5

SUBLANES = 8
LANES = 128
CONV_HIST = 32
KEY_WIN = 2 * WINDOW
VMEM_LIMIT = 58 * 1024 * 1024
VMEM_LIMIT_FFN = 62 * 1024 * 1024
MOD_ROWS_SAMPLE = 16


def _cparams(n_grid):
    return pltpu.CompilerParams(dimension_semantics=("arbitrary",) * n_grid,
                                vmem_limit_bytes=VMEM_LIMIT)


def _dot(a, b):
    return jnp.dot(a, b, preferred_element_type=F32)


ROW_SLAB = 64
ROW_GROUP = 2
PROJ_BLOCKS = 2


def _row_loop(S, L, srcs, dst, fn):
    slab = min(L, ROW_SLAB)
    n = L // slab
    group = ROW_GROUP if n % ROW_GROUP == 0 else 1
    for s in range(S):

        def body(r, c, s=s):
            rows = [pl.ds(pl.multiple_of(s * L + (r * group + i) * slab, slab), slab) for i in range(group)]
            vals = [[src[rw, :] for src in srcs] for rw in rows]
            outs = [fn(s, *v) for v in vals]
            for rw, out in zip(rows, outs):
                dst[rw, :] = out.astype(dst.dtype)
            return c

        lax.fori_loop(0, n // group, body, 0)


def _norm_mod(x_ref, dst_ref, g_ref, sc_ref, sh_ref, S, L):
    def fn(s, x):
        inv = lax.rsqrt(jnp.mean(x * x, axis=-1, keepdims=True) + EPS)
        mul = g_ref[...] * (1.0 + sc_ref[s:s + 1, :])
        return x * inv * mul + sh_ref[s:s + 1, :]

    _row_loop(S, L, [x_ref], dst_ref, fn)


def _residual(m_ref, x_ref, g_ref, ga_ref, out_ref, S, L):
    def fn(s, m, x):
        inv = lax.rsqrt(jnp.mean(m * m, axis=-1, keepdims=True) + EPS)
        return x + m * inv * (g_ref[...] * ga_ref[s:s + 1, :])

    _row_loop(S, L, [m_ref, x_ref], out_ref, fn)


def _row_spec(tm):
    return pl.BlockSpec((None, tm, D_MODEL), lambda b, t, *_: (b, t, 0))


def _vec_spec(index, n=D_MODEL):
    return pl.BlockSpec((None, 1, n), lambda *_: (index, 0, 0))


def _mod_spec(S, layer, which):
    if S == 1:
        first = MOD_ROWS_SAMPLE // SUBLANES
        return pl.BlockSpec((None, None, SUBLANES, D_MODEL), lambda b, t, *_: (layer, which, first + b, 0))
    return pl.BlockSpec((None, None, S, D_MODEL), lambda b, t, *_: (layer, which, 0, 0))


def _mod_kernel(c_ref, w_ref, b_ref, o_ref):
    c = c_ref[...]
    o_ref[...] = _dot(c * jax.nn.sigmoid(c), w_ref[...]) + b_ref[...]


def _modulation(c_all, w_mod, b_mod):
    rows = c_all.shape[0]
    tn = 1024
    per = D_MODEL // tn
    return pl.pallas_call(
        _mod_kernel,
        grid=(DEPTH, 6 * per),
        in_specs=[pl.BlockSpec((rows, D_MODEL), lambda i, j: (0, 0)),
                  pl.BlockSpec((None, D_MODEL, tn), lambda i, j: (i, 0, j)),
                  pl.BlockSpec((None, 1, tn), lambda i, j: (i, 0, j))],
        out_specs=pl.BlockSpec((None, None, rows, tn), lambda i, j: (i, j // per, 0, j % per)),
        out_shape=jax.ShapeDtypeStruct((DEPTH, 6, rows, D_MODEL), F32),
        compiler_params=_cparams(2),
        name="modulation",
    )(c_all, w_mod, b_mod.reshape(DEPTH, 1, 6 * D_MODEL))


def _conv_in_kernel(x_ref, g_ref, sc_ref, sh_ref, wa_ref, wg_ref, ba_ref, bg_ref, u_ref, h_ref, *, S, L):
    @pl.when(pl.program_id(2) == 0)
    def _():
        _norm_mod(x_ref, h_ref, g_ref, sc_ref, sh_ref, S, L)

    h = h_ref[...]
    a = _dot(h, wa_ref[...]) + ba_ref[...]
    g = _dot(h, wg_ref[...]) + bg_ref[...]
    u_ref[...] = a * jax.nn.sigmoid(g)


def _conv_in(x, mod, g_norm, w_in, b_in, *, layer, S, L, tm):
    Bp, T, _ = x.shape
    tn = 512
    J = D_MODEL // tn
    b2 = b_in.reshape(-1, 1, 2 * D_MODEL)
    return pl.pallas_call(
        functools.partial(_conv_in_kernel, S=S, L=L),
        grid=(Bp, T // tm, J),
        in_specs=[_row_spec(tm), _vec_spec(4 * layer + 0), _mod_spec(S, layer, 1), _mod_spec(S, layer, 0),
                  pl.BlockSpec((None, D_MODEL, tn), lambda b, t, j: (0, 0, j)),
                  pl.BlockSpec((None, D_MODEL, tn), lambda b, t, j: (0, 0, j + J)),
                  pl.BlockSpec((None, 1, tn), lambda b, t, j: (0, 0, j)),
                  pl.BlockSpec((None, 1, tn), lambda b, t, j: (0, 0, j + J))],
        out_specs=pl.BlockSpec((None, tm, tn), lambda b, t, j: (b, t, j)),
        out_shape=jax.ShapeDtypeStruct((Bp, T, D_MODEL), F32),
        scratch_shapes=[pltpu.VMEM((tm, D_MODEL), BF16)],
        compiler_params=_cparams(3),
        name="conv_in",
    )(x, g_norm, mod, mod, w_in, w_in, b2, b2)


def _dwconv_block(win, wb_ref, bias, cols, R):
    n_groups = (CONV_K + 1 + SUBLANES) // SUBLANES
    win_rows = R + CONV_HIST
    acc = jnp.broadcast_to(bias, (R // SUBLANES, SUBLANES, LANES))
    for sh in range(SUBLANES):
        wsh = win if sh == 0 else pltpu.roll(win, win_rows - sh, axis=0)
        for a in range(n_groups):
            k = SUBLANES * a + sh - 2
            if 0 <= k < CONV_K:
                tap = wsh[SUBLANES * a:SUBLANES * a + R, :]
                acc = acc + tap.reshape(R // SUBLANES, SUBLANES, LANES) * wb_ref[k, :, cols]
    return acc.reshape(R, LANES)


def _fill_tap_table(wdw_ref, wb_ref):
    for k in range(CONV_K):
        wb_ref[k] = jnp.broadcast_to(wdw_ref[k:k + 1, :], (SUBLANES, D_MODEL))


def _conv_mix_kernel(x_ref, g_ref, sc_ref, sh_ref, wa_ref, wg_ref, ba_ref, bg_ref, wdw_ref, bdw_ref,
                     y_ref, tail_ref, h_ref, wb_ref, carry_ref, *, L, R, nb):
    t = pl.program_id(1)
    j = pl.program_id(2)
    tn = wa_ref.shape[1]
    rb = L // nb

    @pl.when((pl.program_id(0) == 0) & (t == 0) & (j == 0))
    def _():
        _fill_tap_table(wdw_ref, wb_ref)
        carry_ref[...] = jnp.zeros(carry_ref.shape, F32)

    @pl.when(j == 0)
    def _():
        _norm_mod(x_ref, h_ref, g_ref, sc_ref, sh_ref, 1, L)

    wa = wa_ref[...].astype(BF16)
    wg = wg_ref[...].astype(BF16)
    tail = jnp.where(t == 0, 0.0, carry_ref[j])
    for blk in range(nb):
        hb = h_ref[blk * rb:(blk + 1) * rb, :]
        a = _dot(hb, wa) + ba_ref[...]
        g = _dot(hb, wg) + bg_ref[...]
        u = a * jax.nn.sigmoid(g)
        ufull = jnp.concatenate([tail, u], axis=0)
        for cb in range(tn // LANES):
            gcols = pl.ds(pl.multiple_of(j * tn + cb * LANES, LANES), LANES)
            lcols = slice(cb * LANES, (cb + 1) * LANES)
            bias = bdw_ref[:, gcols]
            for r in range(rb // R):
                win = ufull[r * R:r * R + R + CONV_HIST, lcols]
                y_ref[blk * rb + r * R:blk * rb + (r + 1) * R, lcols] = _dwconv_block(win, wb_ref, bias, gcols, R)
        tail = u[rb - CONV_HIST:, :]
    carry_ref[j] = tail
    tail_ref[...] = tail


def _conv_mix(x, mod, g_norm, w_in, b_in, w_dw, b_dw, *, layer, tm, R, nb):
    Bp, T, _ = x.shape
    tn = 256
    J = D_MODEL // tn
    b2 = b_in.reshape(-1, 1, 2 * D_MODEL)
    return pl.pallas_call(
        functools.partial(_conv_mix_kernel, L=tm, R=R, nb=nb),
        grid=(Bp, T // tm, J),
        in_specs=[_row_spec(tm), _vec_spec(4 * layer + 0), _mod_spec(1, layer, 1), _mod_spec(1, layer, 0),
                  pl.BlockSpec((None, D_MODEL, tn), lambda b, t, j: (0, 0, j)),
                  pl.BlockSpec((None, D_MODEL, tn), lambda b, t, j: (0, 0, j + J)),
                  pl.BlockSpec((None, 1, tn), lambda b, t, j: (0, 0, j)),
                  pl.BlockSpec((None, 1, tn), lambda b, t, j: (0, 0, j + J)),
                  pl.BlockSpec((None, CONV_K, D_MODEL), lambda b, t, j: (0, 0, 0)),
                  _vec_spec(0)],
        out_specs=[pl.BlockSpec((None, tm, tn), lambda b, t, j: (b, t, j)),
                   pl.BlockSpec((None, None, CONV_HIST, tn), lambda b, t, j: (b, t, 0, j))],
        out_shape=[jax.ShapeDtypeStruct((Bp, T, D_MODEL), F32),
                   jax.ShapeDtypeStruct((Bp, T // tm, CONV_HIST, D_MODEL), F32)],
        scratch_shapes=[pltpu.VMEM((tm, D_MODEL), BF16),
                        pltpu.VMEM((CONV_K, SUBLANES, D_MODEL), F32),
                        pltpu.VMEM((J, CONV_HIST, tn), F32)],
        compiler_params=_cparams(3),
        name="conv_mix",
    )(x, g_norm, mod, mod, w_in, w_in, b2, b2, w_dw, b_dw.reshape(-1, 1, D_MODEL))


def _conv_proj_kernel(y_ref, lng_ref, lnb_ref, wout_ref, bout_ref, x_ref, g_ref, ga_ref, x1_ref, *, L):
    rb = L // PROJ_BLOCKS
    w = wout_ref[...]
    for blk in range(PROJ_BLOCKS):
        rows = slice(blk * rb, (blk + 1) * rb)
        a = _ln_silu_value(y_ref[rows, :], lng_ref, lnb_ref)
        m = _dot(a, w) + bout_ref[...]
        x1_ref[rows, :] = _residual_value(m, x_ref[rows, :], g_ref, ga_ref[0:1, :])


def _conv_proj(y, x, mod, g_norm, ln_g, ln_b, w_out, b_out, *, layer, tm):
    Bp, T, _ = x.shape
    vec = lambda a: a.reshape(-1, 1, D_MODEL)
    return pl.pallas_call(
        functools.partial(_conv_proj_kernel, L=tm),
        grid=(Bp, T // tm),
        in_specs=[_row_spec(tm), _vec_spec(0), _vec_spec(0),
                  pl.BlockSpec((None, D_MODEL, D_MODEL), lambda b, t: (0, 0, 0), pipeline_mode=pl.Buffered(1)),
                  _vec_spec(0), _row_spec(tm), _vec_spec(4 * layer + 1), _mod_spec(1, layer, 2)],
        out_specs=_row_spec(tm),
        out_shape=jax.ShapeDtypeStruct((Bp, T, D_MODEL), F32),
        compiler_params=_cparams(2),
        name="conv_proj",
    )(y, vec(ln_g), vec(ln_b), w_out, vec(b_out), x, g_norm, mod)


def _dwconv(full_ref, wb_ref, bdw_ref, y_ref, S, L, R):
    def body(cb, c):
        cols = pl.ds(pl.multiple_of(cb * LANES, LANES), LANES)
        bias = bdw_ref[:, cols]
        for s in range(S):
            for r in range(L // R):
                win = full_ref[s, r * R:r * R + R + CONV_HIST, cols]
                y_ref[s * L + r * R:s * L + (r + 1) * R, cols] = _dwconv_block(win, wb_ref, bias, cols, R)
        return c

    lax.fori_loop(0, D_MODEL // LANES, body, 0)


def _ln_silu_value(y, lng_ref, lnb_ref):
    mu = jnp.mean(y, axis=-1, keepdims=True)
    yc = y - mu
    var = jnp.mean(yc * yc, axis=-1, keepdims=True)
    z = yc * lax.rsqrt(var + EPS) * lng_ref[...] + lnb_ref[...]
    return z * jax.nn.sigmoid(z)


def _ln_silu(y_ref, lng_ref, lnb_ref, S, L):
    _row_loop(S, L, [y_ref], y_ref, lambda s, y: _ln_silu_value(y, lng_ref, lnb_ref))


def _conv_out_kernel(u_ref, hist_ref, wdw_ref, bdw_ref, lng_ref, lnb_ref, wout_ref, bout_ref, x_ref, g_ref, ga_ref,
                     x1_ref, full_ref, wb_ref, y_ref, *, S, L, R):
    _fill_tap_table(wdw_ref, wb_ref)
    pad = CONV_HIST - (CONV_K - 1)
    full_ref[:, 0:pad, :] = jnp.zeros((S, pad, D_MODEL), F32)
    full_ref[:, pad:CONV_HIST, :] = hist_ref[...]
    full_ref[:, CONV_HIST:CONV_HIST + L, :] = u_ref[...].reshape(S, L, D_MODEL)
    _dwconv(full_ref, wb_ref, bdw_ref, y_ref, S, L, R)
    _ln_silu(y_ref, lng_ref, lnb_ref, S, L)
    x1_ref[...] = _dot(y_ref[...], wout_ref[...]) + bout_ref[...]
    _residual(x1_ref, x_ref, g_ref, ga_ref, x1_ref, S, L)


def _conv_out(u, hist, x, mod, g_norm, w_dw, b_dw, ln_g, ln_b, w_out, b_out, *, layer, S, L, tm, R):
    Bp, T, _ = x.shape
    assert Bp == 1 and T == tm
    vec = lambda a: a.reshape(-1, 1, D_MODEL)
    return pl.pallas_call(
        functools.partial(_conv_out_kernel, S=S, L=L, R=R),
        grid=(Bp, T // tm),
        in_specs=[_row_spec(tm), pl.BlockSpec((S, CONV_K - 1, D_MODEL), lambda b, t: (0, 0, 0)),
                  pl.BlockSpec((None, CONV_K, D_MODEL), lambda b, t: (0, 0, 0)),
                  _vec_spec(0), _vec_spec(0), _vec_spec(0),
                  pl.BlockSpec((None, D_MODEL, D_MODEL), lambda b, t: (0, 0, 0), pipeline_mode=pl.Buffered(1)),
                  _vec_spec(0), _row_spec(tm), _vec_spec(4 * layer + 1), _mod_spec(S, layer, 2)],
        out_specs=_row_spec(tm),
        out_shape=jax.ShapeDtypeStruct((Bp, T, D_MODEL), F32),
        scratch_shapes=[pltpu.VMEM((S, CONV_HIST + L, D_MODEL), F32),
                        pltpu.VMEM((CONV_K, SUBLANES, D_MODEL), F32),
                        pltpu.VMEM((tm, D_MODEL), F32)],
        compiler_params=_cparams(2),
        name="conv_out",
    )(u, hist, w_dw, vec(b_dw), vec(ln_g), vec(ln_b), w_out, vec(b_out), x, g_norm, mod)


def _ffn_kernel(*refs, S, L, nb, prompt):
    if prompt:
        (x_ref, g2_ref, sc_ref, sh_ref, wg_ref, wv_ref, wdw_ref, bdw_ref, wd_ref, g3_ref, ga_ref,
         x2_ref, st_ref, h_ref, carry_ref) = refs
    else:
        (x_ref, g2_ref, sc_ref, sh_ref, wg_ref, wv_ref, wdw_ref, bdw_ref, wd_ref, g3_ref, ga_ref, hist_ref,
         x2_ref, st_ref, h_ref) = refs
    t = pl.program_id(1)
    j = pl.program_id(2)
    last_j = pl.num_programs(2) - 1
    tm = x_ref.shape[0]
    rb = tm // nb
    seg = min(L, rb)
    pad = SUBLANES
    inline = S == 1

    def step(first, last):
        wg = wg_ref[...].astype(BF16)
        wv = wv_ref[...].astype(BF16)
        wd = wd_ref[...].astype(BF16)
        w0 = wdw_ref[0:1, :]
        w1 = wdw_ref[1:2, :]
        w2 = wdw_ref[2:3, :]
        bd = bdw_ref[...]
        tail = None
        for blk in range(nb):
            rows = slice(blk * rb, (blk + 1) * rb)
            if first:
                x = x_ref[rows, :]
                inv = lax.rsqrt(jnp.mean(x * x, axis=-1, keepdims=True) + EPS)
                hb = (x * inv * (g2_ref[...] * (1.0 + sc_ref[0:1, :])) + sh_ref[0:1, :]).astype(BF16)
                h_ref[rows, :] = hb
            else:
                hb = h_ref[rows, :]
            g = _dot(hb, wg)
            v = _dot(hb, wv)
            p1s, p2s = [], []
            for q in range(rb // seg):
                gq = g[q * seg:(q + 1) * seg, :]
                if not prompt:
                    s = blk * (rb // seg) + q
                    hist = hist_ref[s]
                    st_ref[s] = gq[seg - (FFN_K - 1):, :]
                elif blk == 0:
                    hist = jnp.where(t == 0, 0.0, carry_ref[j])
                else:
                    hist = tail
                gfull = jnp.concatenate([hist, gq], axis=0)
                p1s.append(pltpu.roll(gfull, 1, axis=0)[pad:, :])
                p2s.append(pltpu.roll(gfull, 2, axis=0)[pad:, :])
                tail = gq[seg - pad:, :]
            p1 = p1s[0] if len(p1s) == 1 else jnp.concatenate(p1s, axis=0)
            p2 = p2s[0] if len(p2s) == 1 else jnp.concatenate(p2s, axis=0)
            gc = g * w2 + p1 * w1 + p2 * w0 + bd
            act = (jax.nn.gelu(gc) * v).astype(BF16)
            acc = _dot(act, wd)
            if not first:
                acc = x2_ref[rows, :] + acc
            if last:
                acc = _residual_value(acc, x_ref[rows, :], g3_ref, ga_ref[0:1, :])
            x2_ref[rows, :] = acc
        if prompt:
            carry_ref[j] = tail
            st_ref[0] = tail[pad - (FFN_K - 1):, :]

    if inline:
        pl.when(j == 0)(lambda: step(True, False))
        pl.when(j > 0)(lambda: step(False, False))

        @pl.when(j == last_j)
        def _():
            _residual(x2_ref, x_ref, g3_ref, ga_ref, x2_ref, S, L)
    else:
        @pl.when(j == 0)
        def _():
            _norm_mod(x_ref, h_ref, g2_ref, sc_ref, sh_ref, S, L)
            x2_ref[...] = jnp.zeros(x2_ref.shape, F32)

        step(False, False)

        @pl.when(j == last_j)
        def _():
            _residual(x2_ref, x_ref, g3_ref, ga_ref, x2_ref, S, L)


def _ffn(x, hist, mod, g_norm, w_up, w_dw, b_dw, w_down, *, layer, S, L, tm, tf, nb):
    Bp, T, _ = x.shape
    J = D_FF // tf
    nT = T // tm
    prompt = hist is None
    x_spec = pl.BlockSpec((None, tm, D_MODEL), lambda b, t, j: (b, t, 0), pipeline_mode=pl.Buffered(1))
    in_specs = [x_spec, _vec_spec(4 * layer + 2), _mod_spec(S, layer, 4), _mod_spec(S, layer, 3),
                pl.BlockSpec((None, D_MODEL, tf), lambda b, t, j: (layer, 0, j)),
                pl.BlockSpec((None, D_MODEL, tf), lambda b, t, j: (layer, 0, j + J)),
                pl.BlockSpec((None, FFN_K, tf), lambda b, t, j: (layer, 0, j)),
                pl.BlockSpec((None, 1, tf), lambda b, t, j: (layer, 0, j)),
                pl.BlockSpec((None, tf, D_MODEL), lambda b, t, j: (layer, j, 0)),
                _vec_spec(4 * layer + 3), _mod_spec(S, layer, 5)]
    args = [x, g_norm, mod, mod, w_up, w_up, w_dw, b_dw.reshape(DEPTH, 1, D_FF), w_down, g_norm, mod]
    scratch = [pltpu.VMEM((tm, D_MODEL), BF16)]
    if prompt:
        scratch.append(pltpu.VMEM((J, SUBLANES, tf), F32))
    else:
        in_specs.append(pl.BlockSpec((S, SUBLANES, tf), lambda b, t, j: (0, 0, j)))
        args.append(hist)
    return pl.pallas_call(
        functools.partial(_ffn_kernel, S=S, L=L, nb=nb, prompt=prompt),
        grid=(Bp, nT, J),
        in_specs=in_specs,
        out_specs=[_row_spec(tm),
                   pl.BlockSpec((None, None, S, FFN_K - 1, tf), lambda b, t, j: (b, t, 0, 0, j))],
        out_shape=[jax.ShapeDtypeStruct((Bp, T, D_MODEL), F32),
                   jax.ShapeDtypeStruct((Bp, nT, S, FFN_K - 1, D_FF), F32)],
        scratch_shapes=scratch,
        compiler_params=pltpu.CompilerParams(dimension_semantics=("arbitrary",) * 3,
                                             vmem_limit_bytes=VMEM_LIMIT_FFN),
        name="ffn",
    )(*args)


def _qkv_kernel(x_ref, g_ref, sc_ref, sh_ref, wq_ref, wk_ref, wv_ref, q_ref, k_ref, v_ref, h_ref, *, S, L):
    tm = x_ref.shape[0]
    if S > 1:
        _norm_mod(x_ref, h_ref, g_ref, sc_ref, sh_ref, S, L)
        blocks = [(slice(0, tm), h_ref[...])]
    else:
        rb = tm // PROJ_BLOCKS
        mul = g_ref[...] * (1.0 + sc_ref[0:1, :])
        blocks = []
        for blk in range(PROJ_BLOCKS):
            rows = slice(blk * rb, (blk + 1) * rb)
            x = x_ref[rows, :]
            inv = lax.rsqrt(jnp.mean(x * x, axis=-1, keepdims=True) + EPS)
            blocks.append((rows, x * inv * mul + sh_ref[0:1, :]))
    wq = wq_ref[...]
    wk = wk_ref[...]
    wv = wv_ref[...]
    for rows, h in blocks:
        q_ref[rows, :] = _dot(h, wq)
        k_ref[rows, :] = _dot(h, wk)
        v_ref[rows, :] = _dot(h, wv)


def _qkv(x, mod, g_norm, w_q, w_k, w_v, *, layer, S, L, tm):
    Bp, T, _ = x.shape
    kv_spec = pl.BlockSpec((None, tm, KV_DIM), lambda b, t: (b, t, 0))
    resident = lambda n: pl.BlockSpec((None, D_MODEL, n), lambda b, t: (0, 0, 0), pipeline_mode=pl.Buffered(1))
    return pl.pallas_call(
        functools.partial(_qkv_kernel, S=S, L=L),
        grid=(Bp, T // tm),
        in_specs=[_row_spec(tm), _vec_spec(4 * layer + 0), _mod_spec(S, layer, 1), _mod_spec(S, layer, 0),
                  resident(D_MODEL), resident(KV_DIM), resident(KV_DIM)],
        out_specs=[_row_spec(tm), kv_spec, kv_spec],
        out_shape=[jax.ShapeDtypeStruct((Bp, T, D_MODEL), F32),
                   jax.ShapeDtypeStruct((Bp, T, KV_DIM), F32),
                   jax.ShapeDtypeStruct((Bp, T, KV_DIM), F32)],
        scratch_shapes=[pltpu.VMEM((tm, D_MODEL), F32)],
        compiler_params=_cparams(2),
        name="qkv",
    )(x, g_norm, mod, mod, w_q, w_k, w_v)


REL_SPAN = 3 * LANES


def _bias_kernel(rb_ref, prof_ref, valid_ref, o_ref, *, Qn):
    prof = prof_ref[...]
    valid = valid_ref[...] != 0
    early = lax.broadcasted_iota(jnp.int32, valid.shape, 0) < WINDOW
    for hh in range(N_HEADS):
        f = jnp.zeros(prof.shape, F32)
        for b in range(N_BUCKETS):
            f = jnp.where(prof == b, rb_ref[b, hh], f)
        h, g = divmod(hh, GROUP)
        off = (g * Qn) % LANES
        x = jnp.broadcast_to(f[0:1, :], (KEY_WIN, REL_SPAN))
        r = pltpu.roll(x, WINDOW + off, axis=1, stride=1, stride_axis=0)
        tab = jnp.where(valid, r[:, off:off + Qn], NEG_INF)
        o_ref[0, h, :, g * Qn:(g + 1) * Qn] = tab
        if o_ref.shape[0] > 1:
            o_ref[1, h, :, g * Qn:(g + 1) * Qn] = jnp.where(early, NEG_INF, tab)


def _t5_bucket(rel):
    half = N_BUCKETS // 2
    max_exact = half // 2
    n = jnp.abs(rel)
    ret = jnp.where(rel > 0, half, 0)
    nf = jnp.maximum(n, 1).astype(F32)
    large = max_exact + (jnp.log(nf / max_exact) / math.log(MAX_DISTANCE / max_exact)
                         * (half - max_exact)).astype(jnp.int32)
    large = jnp.minimum(large, half - 1)
    return ret + jnp.where(n < max_exact, n, large)


def _bias_table(rel_bias, Qn, lo, hi, variants):
    assert KEY_WIN + Qn - 1 <= REL_SPAN
    k = jnp.arange(KEY_WIN)[:, None]
    q = jnp.arange(Qn)[None, :]
    valid = ((k >= lo(q)) & (k < hi(q))).astype(jnp.int32)
    prof = jnp.broadcast_to(_t5_bucket(WINDOW - jnp.arange(REL_SPAN)).astype(jnp.int32)[None, :], (SUBLANES, REL_SPAN))
    return pl.pallas_call(
        functools.partial(_bias_kernel, Qn=Qn),
        in_specs=[pl.BlockSpec(memory_space=pltpu.SMEM),
                  pl.BlockSpec((SUBLANES, REL_SPAN), lambda: (0, 0)),
                  pl.BlockSpec((KEY_WIN, Qn), lambda: (0, 0))],
        out_specs=pl.BlockSpec((variants, N_KV_HEADS, KEY_WIN, GROUP * Qn), lambda: (0, 0, 0, 0)),
        out_shape=jax.ShapeDtypeStruct((variants, N_KV_HEADS, KEY_WIN, GROUP * Qn), F32),
        name="rel_bias_table",
    )(rel_bias, prof, valid)


def _attend(q_ref, r0, Qn, kw, vw, bias_ref, sel, sink_ref, o_ref, qs_ref):
    rows = pl.ds(r0, Qn)
    for g in range(GROUP):
        for h in range(N_KV_HEADS):
            src = (h * GROUP + g) * HEAD_DIM
            qs_ref[g * Qn:(g + 1) * Qn, h * HEAD_DIM:(h + 1) * HEAD_DIM] = q_ref[rows, src:src + HEAD_DIM] * ATTN_SCALE
    qt = qs_ref[...].T
    vt = vw.T
    outs = []
    for h in range(N_KV_HEADS):
        hd = slice(h * HEAD_DIM, (h + 1) * HEAD_DIM)
        s = _dot(kw[:, hd], qt[hd, :]) + bias_ref[sel, h]
        sk = sink_ref[h]
        mx = jnp.maximum(jnp.max(s, axis=0, keepdims=True), sk)
        p = jnp.exp(s - mx)
        den = jnp.sum(p, axis=0, keepdims=True) + jnp.exp(sk - mx)
        outs.append(_dot(vt[hd, :], p) * (1.0 / den))
    o2 = jnp.concatenate(outs, axis=0).T
    for g in range(GROUP):
        for h in range(N_KV_HEADS):
            dst = (h * GROUP + g) * HEAD_DIM
            o_ref[rows, dst:dst + HEAD_DIM] = o2[g * Qn:(g + 1) * Qn, h * HEAD_DIM:(h + 1) * HEAD_DIM]


def _attn_prompt_kernel(q_ref, kp_ref, kc_ref, vp_ref, vc_ref, bias_ref, sink_ref, o_ref, kw_ref, vw_ref, qs_ref,
                        *, tq, Qn):
    t = pl.program_id(1)
    kw_ref[0:WINDOW, :] = kp_ref[...]
    kw_ref[WINDOW:WINDOW + tq, :] = kc_ref[...]
    vw_ref[0:WINDOW, :] = vp_ref[...]
    vw_ref[WINDOW:WINDOW + tq, :] = vc_ref[...]

    def body(m, carry):
        r0 = pl.multiple_of(m * Qn, Qn)
        kw = kw_ref[pl.ds(r0, KEY_WIN), :]
        vw = vw_ref[pl.ds(r0, KEY_WIN), :]
        sel = ((t == 0) & (m == 0)).astype(jnp.int32)
        _attend(q_ref, r0, Qn, kw, vw, bias_ref, sel, sink_ref, o_ref, qs_ref)
        return carry

    lax.fori_loop(0, tq // Qn, body, 0)


def _attn_sample_kernel(q_ref, ck_ref, kn_ref, cv_ref, vn_ref, bias_ref, sink_ref, o_ref, kw_ref, vw_ref, qs_ref,
                        *, S, L):
    P = ck_ref.shape[1]
    kw_ref[:, 0:P, :] = ck_ref[...]
    kw_ref[:, P:P + L, :] = kn_ref[...].reshape(S, L, KV_DIM)
    kw_ref[:, P + L:, :] = jnp.zeros((S, KEY_WIN - P - L, KV_DIM), F32)
    vw_ref[:, 0:P, :] = cv_ref[...]
    vw_ref[:, P:P + L, :] = vn_ref[...].reshape(S, L, KV_DIM)
    vw_ref[:, P + L:, :] = jnp.zeros((S, KEY_WIN - P - L, KV_DIM), F32)

    def body(s, carry):
        r0 = pl.multiple_of(s * L, L)
        _attend(q_ref, r0, L, kw_ref[s], vw_ref[s], bias_ref, 0, sink_ref, o_ref, qs_ref)
        return carry

    lax.fori_loop(0, S, body, 0)


def _attn_prompt(q, k, v, bias, sink, *, tq):
    Bp, T, _ = q.shape
    Qn = 2 * CHUNK
    per = tq // WINDOW
    prev_spec = pl.BlockSpec((None, WINDOW, KV_DIM), lambda b, t: (b, jnp.maximum(t * per - 1, 0), 0))
    cur_spec = pl.BlockSpec((None, tq, KV_DIM), lambda b, t: (b, t, 0))
    return pl.pallas_call(
        functools.partial(_attn_prompt_kernel, tq=tq, Qn=Qn),
        grid=(Bp, T // tq),
        in_specs=[_row_spec(tq), prev_spec, cur_spec, prev_spec, cur_spec,
                  pl.BlockSpec((2, N_KV_HEADS, KEY_WIN, GROUP * Qn), lambda b, t: (0, 0, 0, 0),
                               pipeline_mode=pl.Buffered(1)),
                  pl.BlockSpec((N_KV_HEADS, 1, GROUP * Qn), lambda b, t: (0, 0, 0))],
        out_specs=_row_spec(tq),
        out_shape=jax.ShapeDtypeStruct((Bp, T, D_MODEL), F32),
        scratch_shapes=[pltpu.VMEM((WINDOW + tq, KV_DIM), F32), pltpu.VMEM((WINDOW + tq, KV_DIM), F32),
                        pltpu.VMEM((GROUP * Qn, KV_DIM), F32)],
        compiler_params=_cparams(2),
        name="attn_prompt",
    )(q, k, k, v, v, bias, sink)


def _attn_sample(q, k, v, cache_k, cache_v, bias, sink, *, S, L):
    return pl.pallas_call(
        functools.partial(_attn_sample_kernel, S=S, L=L),
        out_shape=jax.ShapeDtypeStruct((S * L, D_MODEL), F32),
        scratch_shapes=[pltpu.VMEM((S, KEY_WIN, KV_DIM), F32), pltpu.VMEM((S, KEY_WIN, KV_DIM), F32),
                        pltpu.VMEM((GROUP * L, KV_DIM), F32)],
        compiler_params=pltpu.CompilerParams(vmem_limit_bytes=VMEM_LIMIT),
        name="attn_sample",
    )(q, cache_k, k, cache_v, v, bias, sink)


def _residual_value(m, x, g_ref, ga):
    inv = lax.rsqrt(jnp.mean(m * m, axis=-1, keepdims=True) + EPS)
    return x + m * inv * (g_ref[...] * ga)


def _attn_out_kernel(o_ref, w_ref, x_ref, g_ref, ga_ref, x1_ref, *, S, L):
    tm = o_ref.shape[0]
    if S > 1:
        x1_ref[...] = _dot(o_ref[...], w_ref[...])
        _residual(x1_ref, x_ref, g_ref, ga_ref, x1_ref, S, L)
        return
    rb = tm // PROJ_BLOCKS
    w = w_ref[...]
    for blk in range(PROJ_BLOCKS):
        rows = slice(blk * rb, (blk + 1) * rb)
        x1_ref[rows, :] = _residual_value(_dot(o_ref[rows, :], w), x_ref[rows, :], g_ref, ga_ref[0:1, :])


def _attn_out(o, x, mod, g_norm, w_o, *, layer, S, L, tm):
    Bp, T, _ = x.shape
    return pl.pallas_call(
        functools.partial(_attn_out_kernel, S=S, L=L),
        grid=(Bp, T // tm),
        in_specs=[_row_spec(tm),
                  pl.BlockSpec((None, D_MODEL, D_MODEL), lambda b, t: (0, 0, 0), pipeline_mode=pl.Buffered(1)),
                  _row_spec(tm), _vec_spec(4 * layer + 1), _mod_spec(S, layer, 2)],
        out_specs=_row_spec(tm),
        out_shape=jax.ShapeDtypeStruct((Bp, T, D_MODEL), F32),
        compiler_params=_cparams(2),
        name="attn_out",
    )(o, w_o, x, g_norm, mod)


def kernel(x_prompt, x_sample, c_prompt, c_sample, cache_conv, cache_k, cache_v, cache_ffn, w_mod, b_mod, g_norm, conv_w_in, conv_b_in, conv_w_dw, conv_b_dw, conv_ln_g, conv_ln_b, conv_w_out, conv_b_out, attn_w_q, attn_w_k, attn_w_v, attn_w_o, attn_sinks, rel_bias, ffn_w_up, ffn_w_dw, ffn_b_dw, ffn_w_down):
    B, T, D = x_prompt.shape
    SB, SL, _ = x_sample.shape
    assert SB == MOD_ROWS_SAMPLE
    c_all = jnp.concatenate(
        [c_sample, jnp.pad(c_prompt[:, None, :], ((0, 0), (0, SUBLANES - 1), (0, 0))).reshape(B * SUBLANES, D)], axis=0)
    mod = _modulation(c_all, w_mod, b_mod)
    gn = g_norm.reshape(DEPTH * 4, 1, D)

    geo_p = dict(S=1, L=1024, tm=1024)
    geo_p_small = dict(S=1, L=512, tm=512)
    geo_s = dict(S=SB, L=SL, tm=SB * SL)

    xp = x_prompt
    xs = x_sample.reshape(1, SB * SL, D)

    yp, tail_p = _conv_mix(xp, mod, gn, conv_w_in, conv_b_in, conv_w_dw, conv_b_dw, layer=0, tm=1024, R=64, nb=2)
    xp = _conv_proj(yp, xp, mod, gn, conv_ln_g, conv_ln_b, conv_w_out, conv_b_out, layer=0, tm=512)
    us = _conv_in(xs, mod, gn, conv_w_in, conv_b_in, layer=0, **geo_s)
    conv_args = (conv_w_dw, conv_b_dw, conv_ln_g, conv_ln_b, conv_w_out, conv_b_out)
    xs = _conv_out(us, cache_conv[0], xs, mod, gn, *conv_args, layer=0, R=SL, **geo_s)
    conv_state_p = tail_p[:, -1, CONV_HIST - (CONV_K - 1):, :][None]
    conv_state_s = us.reshape(SB, SL, D)[:, SL - (CONV_K - 1):, :][None]

    ffn_p, ffn_s = [], []
    ffn_hist = jnp.pad(cache_ffn, ((0, 0), (0, 0), (SUBLANES - (FFN_K - 1), 0), (0, 0)))

    def run_ffn(i, xp, xs):
        w = (ffn_w_up, ffn_w_dw, ffn_b_dw, ffn_w_down)
        xp, st_p = _ffn(xp, None, mod, gn, *w, layer=i, tf=512, nb=2, **geo_p)
        xs, st_s = _ffn(xs, ffn_hist[i], mod, gn, *w, layer=i, tf=512, nb=2, **geo_s)
        ffn_p.append(st_p[:, -1, 0])
        ffn_s.append(st_s[0, 0])
        return xp, xs

    xp, xs = run_ffn(0, xp, xs)

    qp, kp, vp = _qkv(xp, mod, gn, attn_w_q, attn_w_k, attn_w_v, layer=1, **geo_p_small)
    qs, ks, vs = _qkv(xs, mod, gn, attn_w_q, attn_w_k, attn_w_v, layer=1, **geo_s)
    P = cache_k.shape[2]
    assert P == WINDOW and P + SL <= KEY_WIN
    bias_p = _bias_table(rel_bias, 2 * CHUNK,
                         lambda q: (q // CHUNK) * CHUNK, lambda q: (q // CHUNK) * CHUNK + WINDOW + CHUNK, 2)
    bias_s = _bias_table(rel_bias, SL, lambda q: 0 * q, lambda q: 0 * q + P + SL, 1)
    sink_p = jnp.repeat(attn_sinks[0], 2 * CHUNK).reshape(N_KV_HEADS, 1, GROUP * 2 * CHUNK)
    sink_s = jnp.repeat(attn_sinks[0], SL).reshape(N_KV_HEADS, 1, GROUP * SL)
    op = _attn_prompt(qp, kp, vp, bias_p, sink_p, tq=512)
    ck = cache_k[0].reshape(SB, P, KV_DIM)
    cv = cache_v[0].reshape(SB, P, KV_DIM)
    os_ = _attn_sample(qs[0], ks[0], vs[0], ck, cv, bias_s, sink_s, S=SB, L=SL)[None]
    xp = _attn_out(op, xp, mod, gn, attn_w_o, layer=1, **geo_p_small)
    xs = _attn_out(os_, xs, mod, gn, attn_w_o, layer=1, **geo_s)
    keep = min(WINDOW, T)
    k_state_p = kp[:, T - keep:].reshape(B, keep, N_KV_HEADS, HEAD_DIM)[None]
    v_state_p = vp[:, T - keep:].reshape(B, keep, N_KV_HEADS, HEAD_DIM)[None]
    k_state_s = jnp.concatenate([ck, ks.reshape(SB, SL, KV_DIM)], axis=1)[:, SL:]
    v_state_s = jnp.concatenate([cv, vs.reshape(SB, SL, KV_DIM)], axis=1)[:, SL:]
    k_state_s = k_state_s.reshape(SB, P, N_KV_HEADS, HEAD_DIM)[None]
    v_state_s = v_state_s.reshape(SB, P, N_KV_HEADS, HEAD_DIM)[None]

    xp, xs = run_ffn(1, xp, xs)

    return (xp, xs.reshape(SB, SL, D), conv_state_p, conv_state_s,
            k_state_p, v_state_p, k_state_s, v_state_s,
            jnp.stack(ffn_p), jnp.stack(ffn_s))
```

```python
import functools
import math

import jax
import jax.numpy as jnp
from jax import lax
from jax.experimental import pallas as pl
from jax.experimental.pallas import tpu as pltpu

F32 = jnp.float32
BF16 = jnp.bfloat16

D_MODEL = 2048
D_FF = 5632
DEPTH = 2
CONV_K = 31
FFN_K = 3
N_HEADS = 32
N_KV_HEADS = 4
GROUP = N_HEADS // N_KV_HEADS
HEAD_DIM = 64
KV_DIM = N_KV_HEADS * HEAD_DIM
CHUNK = 64
WINDOW = 128
N_BUCKETS = 32
MAX_DISTANCE = 128
EPS = 1e-6
NEG_INF = -1e30
ATTN_SCALE = HEAD_DIM ** -0.5

SUBLANES = 8
LANES = 128
CONV_HIST = 32
KEY_WIN = 2 * WINDOW
VMEM_LIMIT = 58 * 1024 * 1024
VMEM_LIMIT_FFN = 62 * 1024 * 1024
MOD_ROWS_SAMPLE = 16


def _cparams(n_grid):
    return pltpu.CompilerParams(dimension_semantics=("arbitrary",) * n_grid,
                                vmem_limit_bytes=VMEM_LIMIT)


def _dot(a, b):
    return jnp.dot(a, b, preferred_element_type=F32)


ROW_SLAB = 64
ROW_GROUP = 2
PROJ_BLOCKS = 2
ATTN_UNITS_PER_ITER = 4


def _row_loop(S, L, srcs, dst, fn):
    slab = min(L, ROW_SLAB)
    n = L // slab
    group = ROW_GROUP if n % ROW_GROUP == 0 else 1
    for s in range(S):

        def body(r, c, s=s):
            rows = [pl.ds(pl.multiple_of(s * L + (r * group + i) * slab, slab), slab) for i in range(group)]
            vals = [[src[rw, :] for src in srcs] for rw in rows]
            outs = [fn(s, *v) for v in vals]
            for rw, out in zip(rows, outs):
                dst[rw, :] = out.astype(dst.dtype)
            return c

        lax.fori_loop(0, n // group, body, 0)


def _norm_mod(x_ref, dst_ref, g_ref, sc_ref, sh_ref, S, L):
    def fn(s, x):
        inv = lax.rsqrt(jnp.mean(x * x, axis=-1, keepdims=True) + EPS)
        mul = g_ref[...] * (1.0 + sc_ref[s:s + 1, :])
        return x * inv * mul + sh_ref[s:s + 1, :]

    _row_loop(S, L, [x_ref], dst_ref, fn)


def _residual(m_ref, x_ref, g_ref, ga_ref, out_ref, S, L):
    def fn(s, m, x):
        inv = lax.rsqrt(jnp.mean(m * m, axis=-1, keepdims=True) + EPS)
        return x + m * inv * (g_ref[...] * ga_ref[s:s + 1, :])

    _row_loop(S, L, [m_ref, x_ref], out_ref, fn)


def _row_spec(tm):
    return pl.BlockSpec((None, tm, D_MODEL), lambda b, t, *_: (b, t, 0))


def _vec_spec(index, n=D_MODEL):
    return pl.BlockSpec((None, 1, n), lambda *_: (index, 0, 0))


def _mod_spec(S, layer, which):
    if S == 1:
        first = MOD_ROWS_SAMPLE // SUBLANES
        return pl.BlockSpec((None, None, SUBLANES, D_MODEL), lambda b, t, *_: (layer, which, first + b, 0))
    return pl.BlockSpec((None, None, S, D_MODEL), lambda b, t, *_: (layer, which, 0, 0))


def _mod_kernel(c_ref, w_ref, b_ref, o_ref):
    c = c_ref[...]
    o_ref[...] = _dot(c * jax.nn.sigmoid(c), w_ref[...]) + b_ref[...]


def _modulation(c_all, w_mod, b_mod):
    rows = c_all.shape[0]
    tn = 1024
    per = D_MODEL // tn
    return pl.pallas_call(
        _mod_kernel,
        grid=(DEPTH, 6 * per),
        in_specs=[pl.BlockSpec((rows, D_MODEL), lambda i, j: (0, 0)),
                  pl.BlockSpec((None, D_MODEL, tn), lambda i, j: (i, 0, j)),
                  pl.BlockSpec((None, 1, tn), lambda i, j: (i, 0, j))],
        out_specs=pl.BlockSpec((None, None, rows, tn), lambda i, j: (i, j // per, 0, j % per)),
        out_shape=jax.ShapeDtypeStruct((DEPTH, 6, rows, D_MODEL), F32),
        compiler_params=_cparams(2),
        name="modulation",
    )(c_all, w_mod, b_mod.reshape(DEPTH, 1, 6 * D_MODEL))


def _conv_in_kernel(x_ref, g_ref, sc_ref, sh_ref, wa_ref, wg_ref, ba_ref, bg_ref, u_ref, h_ref, *, S, L):
    @pl.when(pl.program_id(2) == 0)
    def _():
        _norm_mod(x_ref, h_ref, g_ref, sc_ref, sh_ref, S, L)

    h = h_ref[...]
    a = _dot(h, wa_ref[...]) + ba_ref[...]
    g = _dot(h, wg_ref[...]) + bg_ref[...]
    u_ref[...] = a * jax.nn.sigmoid(g)


def _conv_in(x, mod, g_norm, w_in, b_in, *, layer, S, L, tm):
    Bp, T, _ = x.shape
    tn = 512
    J = D_MODEL // tn
    b2 = b_in.reshape(-1, 1, 2 * D_MODEL)
    return pl.pallas_call(
        functools.partial(_conv_in_kernel, S=S, L=L),
        grid=(Bp, T // tm, J),
        in_specs=[_row_spec(tm), _vec_spec(4 * layer + 0), _mod_spec(S, layer, 1), _mod_spec(S, layer, 0),
                  pl.BlockSpec((None, D_MODEL, tn), lambda b, t, j: (0, 0, j)),
                  pl.BlockSpec((None, D_MODEL, tn), lambda b, t, j: (0, 0, j + J)),
                  pl.BlockSpec((None, 1, tn), lambda b, t, j: (0, 0, j)),
                  pl.BlockSpec((None, 1, tn), lambda b, t, j: (0, 0, j + J))],
        out_specs=pl.BlockSpec((None, tm, tn), lambda b, t, j: (b, t, j)),
        out_shape=jax.ShapeDtypeStruct((Bp, T, D_MODEL), F32),
        scratch_shapes=[pltpu.VMEM((tm, D_MODEL), F32)],
        compiler_params=_cparams(3),
        name="conv_in",
    )(x, g_norm, mod, mod, w_in, w_in, b2, b2)


def _dwconv_block(win, wb_ref, bias, cols, R):
    n_groups = (CONV_K + 1 + SUBLANES) // SUBLANES
    win_rows = R + CONV_HIST
    acc = jnp.broadcast_to(bias, (R // SUBLANES, SUBLANES, LANES))
    for sh in range(SUBLANES):
        wsh = win if sh == 0 else pltpu.roll(win, win_rows - sh, axis=0)
        for a in range(n_groups):
            k = SUBLANES * a + sh - 2
            if 0 <= k < CONV_K:
                tap = wsh[SUBLANES * a:SUBLANES * a + R, :]
                acc = acc + tap.reshape(R // SUBLANES, SUBLANES, LANES) * wb_ref[k, :, cols]
    return acc.reshape(R, LANES)


def _fill_tap_table(wdw_ref, wb_ref):
    for k in range(CONV_K):
        wb_ref[k] = jnp.broadcast_to(wdw_ref[k:k + 1, :], (SUBLANES, D_MODEL))


def _conv_mix_kernel(x_ref, g_ref, sc_ref, sh_ref, wa_ref, wg_ref, ba_ref, bg_ref, wdw_ref, bdw_ref,
                     y_ref, tail_ref, h_ref, wb_ref, carry_ref, *, L, R, nb):
    t = pl.program_id(1)
    j = pl.program_id(2)
    tn = wa_ref.shape[1]
    rb = L // nb

    @pl.when((pl.program_id(0) == 0) & (t == 0) & (j == 0))
    def _():
        _fill_tap_table(wdw_ref, wb_ref)
        carry_ref[...] = jnp.zeros(carry_ref.shape, F32)

    @pl.when(j == 0)
    def _():
        _norm_mod(x_ref, h_ref, g_ref, sc_ref, sh_ref, 1, L)

    wa = wa_ref[...].astype(BF16)
    wg = wg_ref[...].astype(BF16)
    tail = jnp.where(t == 0, 0.0, carry_ref[j])
    for blk in range(nb):
        hb = h_ref[blk * rb:(blk + 1) * rb, :]
        a = _dot(hb, wa) + ba_ref[...]
        g = _dot(hb, wg) + bg_ref[...]
        u = a * jax.nn.sigmoid(g)
        ufull = jnp.concatenate([tail, u], axis=0)
        for cb in range(tn // LANES):
            gcols = pl.ds(pl.multiple_of(j * tn + cb * LANES, LANES), LANES)
            lcols = slice(cb * LANES, (cb + 1) * LANES)
            bias = bdw_ref[:, gcols]
            for r in range(rb // R):
                win = ufull[r * R:r * R + R + CONV_HIST, lcols]
                y_ref[blk * rb + r * R:blk * rb + (r + 1) * R, lcols] = _dwconv_block(win, wb_ref, bias, gcols, R)
        tail = u[rb - CONV_HIST:, :]
    carry_ref[j] = tail
    tail_ref[...] = tail


def _conv_mix(x, mod, g_norm, w_in, b_in, w_dw, b_dw, *, layer, tm, R, nb):
    Bp, T, _ = x.shape
    tn = 256
    J = D_MODEL // tn
    b2 = b_in.reshape(-1, 1, 2 * D_MODEL)
    return pl.pallas_call(
        functools.partial(_conv_mix_kernel, L=tm, R=R, nb=nb),
        grid=(Bp, T // tm, J),
        in_specs=[_row_spec(tm), _vec_spec(4 * layer + 0), _mod_spec(1, layer, 1), _mod_spec(1, layer, 0),
                  pl.BlockSpec((None, D_MODEL, tn), lambda b, t, j: (0, 0, j)),
                  pl.BlockSpec((None, D_MODEL, tn), lambda b, t, j: (0, 0, j + J)),
                  pl.BlockSpec((None, 1, tn), lambda b, t, j: (0, 0, j)),
                  pl.BlockSpec((None, 1, tn), lambda b, t, j: (0, 0, j + J)),
                  pl.BlockSpec((None, CONV_K, D_MODEL), lambda b, t, j: (0, 0, 0)),
                  _vec_spec(0)],
        out_specs=[pl.BlockSpec((None, tm, tn), lambda b, t, j: (b, t, j)),
                   pl.BlockSpec((None, None, CONV_HIST, tn), lambda b, t, j: (b, t, 0, j))],
        out_shape=[jax.ShapeDtypeStruct((Bp, T, D_MODEL), F32),
                   jax.ShapeDtypeStruct((Bp, T // tm, CONV_HIST, D_MODEL), F32)],
        scratch_shapes=[pltpu.VMEM((tm, D_MODEL), BF16),
                        pltpu.VMEM((CONV_K, SUBLANES, D_MODEL), F32),
                        pltpu.VMEM((J, CONV_HIST, tn), F32)],
        compiler_params=_cparams(3),
        name="conv_mix",
    )(x, g_norm, mod, mod, w_in, w_in, b2, b2, w_dw, b_dw.reshape(-1, 1, D_MODEL))


def _conv_proj_kernel(y_ref, lng_ref, lnb_ref, wout_ref, bout_ref, x_ref, g_ref, ga_ref, x1_ref, *, L):
    rb = L // PROJ_BLOCKS
    w = wout_ref[...]
    for blk in range(PROJ_BLOCKS):
        rows = slice(blk * rb, (blk + 1) * rb)
        a = _ln_silu_value(y_ref[rows, :], lng_ref, lnb_ref)
        m = _dot(a, w) + bout_ref[...]
        x1_ref[rows, :] = _residual_value(m, x_ref[rows, :], g_ref, ga_ref[0:1, :])


def _conv_proj(y, x, mod, g_norm, ln_g, ln_b, w_out, b_out, *, layer, tm):
    Bp, T, _ = x.shape
    vec = lambda a: a.reshape(-1, 1, D_MODEL)
    return pl.pallas_call(
        functools.partial(_conv_proj_kernel, L=tm),
        grid=(Bp, T // tm),
        in_specs=[_row_spec(tm), _vec_spec(0), _vec_spec(0),
                  pl.BlockSpec((None, D_MODEL, D_MODEL), lambda b, t: (0, 0, 0), pipeline_mode=pl.Buffered(1)),
                  _vec_spec(0), _row_spec(tm), _vec_spec(4 * layer + 1), _mod_spec(1, layer, 2)],
        out_specs=_row_spec(tm),
        out_shape=jax.ShapeDtypeStruct((Bp, T, D_MODEL), F32),
        compiler_params=_cparams(2),
        name="conv_proj",
    )(y, vec(ln_g), vec(ln_b), w_out, vec(b_out), x, g_norm, mod)


def _dwconv(full_ref, wb_ref, bdw_ref, y_ref, S, L, R):
    def body(cb, c):
        cols = pl.ds(pl.multiple_of(cb * LANES, LANES), LANES)
        bias = bdw_ref[:, cols]
        for s in range(S):
            for r in range(L // R):
                win = full_ref[s, r * R:r * R + R + CONV_HIST, cols]
                y_ref[s * L + r * R:s * L + (r + 1) * R, cols] = _dwconv_block(win, wb_ref, bias, cols, R)
        return c

    lax.fori_loop(0, D_MODEL // LANES, body, 0)


def _ln_silu_value(y, lng_ref, lnb_ref):
    mu = jnp.mean(y, axis=-1, keepdims=True)
    yc = y - mu
    var = jnp.mean(yc * yc, axis=-1, keepdims=True)
    z = yc * lax.rsqrt(var + EPS) * lng_ref[...] + lnb_ref[...]
    return z * jax.nn.sigmoid(z)


def _ln_silu(y_ref, lng_ref, lnb_ref, S, L):
    _row_loop(S, L, [y_ref], y_ref, lambda s, y: _ln_silu_value(y, lng_ref, lnb_ref))


def _conv_out_kernel(u_ref, hist_ref, wdw_ref, bdw_ref, lng_ref, lnb_ref, wout_ref, bout_ref, x_ref, g_ref, ga_ref,
                     x1_ref, full_ref, wb_ref, y_ref, *, S, L, R):
    _fill_tap_table(wdw_ref, wb_ref)
    pad = CONV_HIST - (CONV_K - 1)
    full_ref[:, 0:pad, :] = jnp.zeros((S, pad, D_MODEL), F32)
    full_ref[:, pad:CONV_HIST, :] = hist_ref[...]
    full_ref[:, CONV_HIST:CONV_HIST + L, :] = u_ref[...].reshape(S, L, D_MODEL)
    _dwconv(full_ref, wb_ref, bdw_ref, y_ref, S, L, R)
    _ln_silu(y_ref, lng_ref, lnb_ref, S, L)
    x1_ref[...] = _dot(y_ref[...], wout_ref[...]) + bout_ref[...]
    _residual(x1_ref, x_ref, g_ref, ga_ref, x1_ref, S, L)


def _conv_out(u, hist, x, mod, g_norm, w_dw, b_dw, ln_g, ln_b, w_out, b_out, *, layer, S, L, tm, R):
    Bp, T, _ = x.shape
    assert Bp == 1 and T == tm
    vec = lambda a: a.reshape(-1, 1, D_MODEL)
    return pl.pallas_call(
        functools.partial(_conv_out_kernel, S=S, L=L, R=R),
        grid=(Bp, T // tm),
        in_specs=[_row_spec(tm), pl.BlockSpec((S, CONV_K - 1, D_MODEL), lambda b, t: (0, 0, 0)),
                  pl.BlockSpec((None, CONV_K, D_MODEL), lambda b, t: (0, 0, 0)),
                  _vec_spec(0), _vec_spec(0), _vec_spec(0),
                  pl.BlockSpec((None, D_MODEL, D_MODEL), lambda b, t: (0, 0, 0), pipeline_mode=pl.Buffered(1)),
                  _vec_spec(0), _row_spec(tm), _vec_spec(4 * layer + 1), _mod_spec(S, layer, 2)],
        out_specs=_row_spec(tm),
        out_shape=jax.ShapeDtypeStruct((Bp, T, D_MODEL), F32),
        scratch_shapes=[pltpu.VMEM((S, CONV_HIST + L, D_MODEL), F32),
                        pltpu.VMEM((CONV_K, SUBLANES, D_MODEL), F32),
                        pltpu.VMEM((tm, D_MODEL), F32)],
        compiler_params=_cparams(2),
        name="conv_out",
    )(u, hist, w_dw, vec(b_dw), vec(ln_g), vec(ln_b), w_out, vec(b_out), x, g_norm, mod)


def _ffn_kernel(*refs, S, L, nb, prompt):
    if prompt:
        (x_ref, g2_ref, sc_ref, sh_ref, wg_ref, wv_ref, wdw_ref, bdw_ref, wd_ref, g3_ref, ga_ref,
         x2_ref, st_ref, h_ref, carry_ref) = refs
    else:
        (x_ref, g2_ref, sc_ref, sh_ref, wg_ref, wv_ref, wdw_ref, bdw_ref, wd_ref, g3_ref, ga_ref, hist_ref,
         x2_ref, st_ref, h_ref) = refs
    t = pl.program_id(1)
    j = pl.program_id(2)
    last_j = pl.num_programs(2) - 1
    tm = x_ref.shape[0]
    rb = tm // nb
    seg = min(L, rb)
    pad = SUBLANES
    inline = S == 1

    mm_dtype = h_ref.dtype

    def step(first, last):
        wg = wg_ref[...].astype(mm_dtype)
        wv = wv_ref[...].astype(mm_dtype)
        wd = wd_ref[...].astype(mm_dtype)
        w0 = wdw_ref[0:1, :]
        w1 = wdw_ref[1:2, :]
        w2 = wdw_ref[2:3, :]
        bd = bdw_ref[...]
        tail = None
        for blk in range(nb):
            rows = slice(blk * rb, (blk + 1) * rb)
            if first:
                x = x_ref[rows, :]
                inv = lax.rsqrt(jnp.mean(x * x, axis=-1, keepdims=True) + EPS)
                hb = (x * inv * (g2_ref[...] * (1.0 + sc_ref[0:1, :])) + sh_ref[0:1, :]).astype(mm_dtype)
                h_ref[rows, :] = hb
            else:
                hb = h_ref[rows, :]
            g = _dot(hb, wg)
            v = _dot(hb, wv)
            p1s, p2s = [], []
            for q in range(rb // seg):
                gq = g[q * seg:(q + 1) * seg, :]
                if not prompt:
                    s = blk * (rb // seg) + q
                    hist = hist_ref[s]
                    st_ref[s] = gq[seg - (FFN_K - 1):, :]
                elif blk == 0:
                    hist = jnp.where(t == 0, 0.0, carry_ref[j])
                else:
                    hist = tail
                gfull = jnp.concatenate([hist, gq], axis=0)
                p1s.append(pltpu.roll(gfull, 1, axis=0)[pad:, :])
                p2s.append(pltpu.roll(gfull, 2, axis=0)[pad:, :])
                tail = gq[seg - pad:, :]
            p1 = p1s[0] if len(p1s) == 1 else jnp.concatenate(p1s, axis=0)
            p2 = p2s[0] if len(p2s) == 1 else jnp.concatenate(p2s, axis=0)
            gc = g * w2 + p1 * w1 + p2 * w0 + bd
            act = (jax.nn.gelu(gc) * v).astype(mm_dtype)
            acc = _dot(act, wd)
            if not first:
                acc = x2_ref[rows, :] + acc
            if last:
                acc = _residual_value(acc, x_ref[rows, :], g3_ref, ga_ref[0:1, :])
            x2_ref[rows, :] = acc
        if prompt:
            carry_ref[j] = tail
            st_ref[0] = tail[pad - (FFN_K - 1):, :]

    if inline:
        pl.when(j == 0)(lambda: step(True, False))
        pl.when(j > 0)(lambda: step(False, False))

        @pl.when(j == last_j)
        def _():
            _residual(x2_ref, x_ref, g3_ref, ga_ref, x2_ref, S, L)
    else:
        @pl.when(j == 0)
        def _():
            _norm_mod(x_ref, h_ref, g2_ref, sc_ref, sh_ref, S, L)
            x2_ref[...] = jnp.zeros(x2_ref.shape, F32)

        step(False, False)

        @pl.when(j == last_j)
        def _():
            _residual(x2_ref, x_ref, g3_ref, ga_ref, x2_ref, S, L)


def _ffn(x, hist, mod, g_norm, w_up, w_dw, b_dw, w_down, *, layer, S, L, tm, tf, nb):
    Bp, T, _ = x.shape
    J = D_FF // tf
    nT = T // tm
    prompt = hist is None
    x_spec = pl.BlockSpec((None, tm, D_MODEL), lambda b, t, j: (b, t, 0), pipeline_mode=pl.Buffered(1))
    in_specs = [x_spec, _vec_spec(4 * layer + 2), _mod_spec(S, layer, 4), _mod_spec(S, layer, 3),
                pl.BlockSpec((None, D_MODEL, tf), lambda b, t, j: (layer, 0, j)),
                pl.BlockSpec((None, D_MODEL, tf), lambda b, t, j: (layer, 0, j + J)),
                pl.BlockSpec((None, FFN_K, tf), lambda b, t, j: (layer, 0, j)),
                pl.BlockSpec((None, 1, tf), lambda b, t, j: (layer, 0, j)),
                pl.BlockSpec((None, tf, D_MODEL), lambda b, t, j: (layer, j, 0)),
                _vec_spec(4 * layer + 3), _mod_spec(S, layer, 5)]
    args = [x, g_norm, mod, mod, w_up, w_up, w_dw, b_dw.reshape(DEPTH, 1, D_FF), w_down, g_norm, mod]
    scratch = [pltpu.VMEM((tm, D_MODEL), BF16 if prompt else F32)]
    if prompt:
        scratch.append(pltpu.VMEM((J, SUBLANES, tf), F32))
    else:
        in_specs.append(pl.BlockSpec((S, SUBLANES, tf), lambda b, t, j: (0, 0, j)))
        args.append(hist)
    return pl.pallas_call(
        functools.partial(_ffn_kernel, S=S, L=L, nb=nb, prompt=prompt),
        grid=(Bp, nT, J),
        in_specs=in_specs,
        out_specs=[_row_spec(tm),
                   pl.BlockSpec((None, None, S, FFN_K - 1, tf), lambda b, t, j: (b, t, 0, 0, j))],
        out_shape=[jax.ShapeDtypeStruct((Bp, T, D_MODEL), F32),
                   jax.ShapeDtypeStruct((Bp, nT, S, FFN_K - 1, D_FF), F32)],
        scratch_shapes=scratch,
        compiler_params=pltpu.CompilerParams(dimension_semantics=("arbitrary",) * 3,
                                             vmem_limit_bytes=VMEM_LIMIT_FFN),
        name="ffn",
    )(*args)


def _qkv_kernel(x_ref, g_ref, sc_ref, sh_ref, wq_ref, wk_ref, wv_ref, q_ref, k_ref, v_ref, h_ref, *, S, L):
    tm = x_ref.shape[0]
    if S > 1:
        _norm_mod(x_ref, h_ref, g_ref, sc_ref, sh_ref, S, L)
        blocks = [(slice(0, tm), h_ref[...])]
    else:
        rb = tm // PROJ_BLOCKS
        mul = g_ref[...] * (1.0 + sc_ref[0:1, :])
        blocks = []
        for blk in range(PROJ_BLOCKS):
            rows = slice(blk * rb, (blk + 1) * rb)
            x = x_ref[rows, :]
            inv = lax.rsqrt(jnp.mean(x * x, axis=-1, keepdims=True) + EPS)
            blocks.append((rows, x * inv * mul + sh_ref[0:1, :]))
    wq = wq_ref[...]
    wk = wk_ref[...]
    wv = wv_ref[...]
    for rows, h in blocks:
        q_ref[rows, :] = _dot(h, wq)
        k_ref[rows, :] = _dot(h, wk)
        v_ref[rows, :] = _dot(h, wv)


def _qkv(x, mod, g_norm, w_q, w_k, w_v, *, layer, S, L, tm):
    Bp, T, _ = x.shape
    kv_spec = pl.BlockSpec((None, tm, KV_DIM), lambda b, t: (b, t, 0))
    resident = lambda n: pl.BlockSpec((None, D_MODEL, n), lambda b, t: (0, 0, 0), pipeline_mode=pl.Buffered(1))
    return pl.pallas_call(
        functools.partial(_qkv_kernel, S=S, L=L),
        grid=(Bp, T // tm),
        in_specs=[_row_spec(tm), _vec_spec(4 * layer + 0), _mod_spec(S, layer, 1), _mod_spec(S, layer, 0),
                  resident(D_MODEL), resident(KV_DIM), resident(KV_DIM)],
        out_specs=[_row_spec(tm), kv_spec, kv_spec],
        out_shape=[jax.ShapeDtypeStruct((Bp, T, D_MODEL), F32),
                   jax.ShapeDtypeStruct((Bp, T, KV_DIM), F32),
                   jax.ShapeDtypeStruct((Bp, T, KV_DIM), F32)],
        scratch_shapes=[pltpu.VMEM((tm, D_MODEL), F32)],
        compiler_params=_cparams(2),
        name="qkv",
    )(x, g_norm, mod, mod, w_q, w_k, w_v)


REL_SPAN = 3 * LANES


def _bias_kernel(rb_ref, prof_ref, valid_ref, o_ref, *, Qn):
    prof = prof_ref[...]
    valid = valid_ref[...] != 0
    early = lax.broadcasted_iota(jnp.int32, valid.shape, 0) < WINDOW
    for hh in range(N_HEADS):
        f = jnp.zeros(prof.shape, F32)
        for b in range(N_BUCKETS):
            f = jnp.where(prof == b, rb_ref[b, hh], f)
        h, g = divmod(hh, GROUP)
        off = (g * Qn) % LANES
        x = jnp.broadcast_to(f[0:1, :], (KEY_WIN, REL_SPAN))
        r = pltpu.roll(x, WINDOW + off, axis=1, stride=1, stride_axis=0)
        tab = jnp.where(valid, r[:, off:off + Qn], NEG_INF)
        o_ref[0, h, :, g * Qn:(g + 1) * Qn] = tab
        if o_ref.shape[0] > 1:
            o_ref[1, h, :, g * Qn:(g + 1) * Qn] = jnp.where(early, NEG_INF, tab)


def _t5_bucket(rel):
    half = N_BUCKETS // 2
    max_exact = half // 2
    n = jnp.abs(rel)
    ret = jnp.where(rel > 0, half, 0)
    nf = jnp.maximum(n, 1).astype(F32)
    large = max_exact + (jnp.log(nf / max_exact) / math.log(MAX_DISTANCE / max_exact)
                         * (half - max_exact)).astype(jnp.int32)
    large = jnp.minimum(large, half - 1)
    return ret + jnp.where(n < max_exact, n, large)


def _bias_table(rel_bias, Qn, lo, hi, variants):
    assert KEY_WIN + Qn - 1 <= REL_SPAN
    k = jnp.arange(KEY_WIN)[:, None]
    q = jnp.arange(Qn)[None, :]
    valid = ((k >= lo(q)) & (k < hi(q))).astype(jnp.int32)
    prof = jnp.broadcast_to(_t5_bucket(WINDOW - jnp.arange(REL_SPAN)).astype(jnp.int32)[None, :], (SUBLANES, REL_SPAN))
    return pl.pallas_call(
        functools.partial(_bias_kernel, Qn=Qn),
        in_specs=[pl.BlockSpec(memory_space=pltpu.SMEM),
                  pl.BlockSpec((SUBLANES, REL_SPAN), lambda: (0, 0)),
                  pl.BlockSpec((KEY_WIN, Qn), lambda: (0, 0))],
        out_specs=pl.BlockSpec((variants, N_KV_HEADS, KEY_WIN, GROUP * Qn), lambda: (0, 0, 0, 0)),
        out_shape=jax.ShapeDtypeStruct((variants, N_KV_HEADS, KEY_WIN, GROUP * Qn), F32),
        name="rel_bias_table",
    )(rel_bias, prof, valid)


def _attend(q_ref, r0, Qn, kw, vw, bias_ref, sel, sink_ref, o_ref, qs_ref):
    rows = pl.ds(r0, Qn)
    for g in range(GROUP):
        for h in range(N_KV_HEADS):
            src = (h * GROUP + g) * HEAD_DIM
            qs_ref[g * Qn:(g + 1) * Qn, h * HEAD_DIM:(h + 1) * HEAD_DIM] = q_ref[rows, src:src + HEAD_DIM] * ATTN_SCALE
    qt = qs_ref[...].T
    vt = vw.T
    outs = []
    for h in range(N_KV_HEADS):
        hd = slice(h * HEAD_DIM, (h + 1) * HEAD_DIM)
        s = _dot(kw[:, hd], qt[hd, :]) + bias_ref[sel, h]
        sk = sink_ref[h]
        mx = jnp.maximum(jnp.max(s, axis=0, keepdims=True), sk)
        p = jnp.exp(s - mx)
        den = jnp.sum(p, axis=0, keepdims=True) + jnp.exp(sk - mx)
        outs.append(_dot(vt[hd, :], p) * (1.0 / den))
    o2 = jnp.concatenate(outs, axis=0).T
    for g in range(GROUP):
        for h in range(N_KV_HEADS):
            dst = (h * GROUP + g) * HEAD_DIM
            o_ref[rows, dst:dst + HEAD_DIM] = o2[g * Qn:(g + 1) * Qn, h * HEAD_DIM:(h + 1) * HEAD_DIM]


def _attn_prompt_kernel(q_ref, kp_ref, kc_ref, vp_ref, vc_ref, bias_ref, sink_ref, o_ref, kw_ref, vw_ref, qs_ref,
                        *, tq, Qn):
    t = pl.program_id(1)
    kw_ref[0:WINDOW, :] = kp_ref[...]
    kw_ref[WINDOW:WINDOW + tq, :] = kc_ref[...]
    vw_ref[0:WINDOW, :] = vp_ref[...]
    vw_ref[WINDOW:WINDOW + tq, :] = vc_ref[...]

    def body(mp, carry):
        for u in range(ATTN_UNITS_PER_ITER):
            m = mp * ATTN_UNITS_PER_ITER + u
            r0 = pl.multiple_of(m * Qn, Qn)
            kw = kw_ref[pl.ds(r0, KEY_WIN), :]
            vw = vw_ref[pl.ds(r0, KEY_WIN), :]
            sel = ((t == 0) & (m == 0)).astype(jnp.int32)
            _attend(q_ref, r0, Qn, kw, vw, bias_ref, sel, sink_ref, o_ref, qs_ref.at[u])
        return carry

    lax.fori_loop(0, tq // (Qn * ATTN_UNITS_PER_ITER), body, 0)


def _attn_sample_kernel(q_ref, ck_ref, kn_ref, cv_ref, vn_ref, bias_ref, sink_ref, o_ref, kw_ref, vw_ref, qs_ref,
                        *, S, L):
    P = ck_ref.shape[1]
    kw_ref[:, 0:P, :] = ck_ref[...]
    kw_ref[:, P:P + L, :] = kn_ref[...].reshape(S, L, KV_DIM)
    kw_ref[:, P + L:, :] = jnp.zeros((S, KEY_WIN - P - L, KV_DIM), F32)
    vw_ref[:, 0:P, :] = cv_ref[...]
    vw_ref[:, P:P + L, :] = vn_ref[...].reshape(S, L, KV_DIM)
    vw_ref[:, P + L:, :] = jnp.zeros((S, KEY_WIN - P - L, KV_DIM), F32)

    def body(s, carry):
        r0 = pl.multiple_of(s * L, L)
        _attend(q_ref, r0, L, kw_ref[s], vw_ref[s], bias_ref, 0, sink_ref, o_ref, qs_ref)
        return carry

    lax.fori_loop(0, S, body, 0)


def _attn_prompt(q, k, v, bias, sink, *, tq):
    Bp, T, _ = q.shape
    Qn = 2 * CHUNK
    per = tq // WINDOW
    prev_spec = pl.BlockSpec((None, WINDOW, KV_DIM), lambda b, t: (b, jnp.maximum(t * per - 1, 0), 0))
    cur_spec = pl.BlockSpec((None, tq, KV_DIM), lambda b, t: (b, t, 0))
    return pl.pallas_call(
        functools.partial(_attn_prompt_kernel, tq=tq, Qn=Qn),
        grid=(Bp, T // tq),
        in_specs=[_row_spec(tq), prev_spec, cur_spec, prev_spec, cur_spec,
                  pl.BlockSpec((2, N_KV_HEADS, KEY_WIN, GROUP * Qn), lambda b, t: (0, 0, 0, 0),
                               pipeline_mode=pl.Buffered(1)),
                  pl.BlockSpec((N_KV_HEADS, 1, GROUP * Qn), lambda b, t: (0, 0, 0))],
        out_specs=_row_spec(tq),
        out_shape=jax.ShapeDtypeStruct((Bp, T, D_MODEL), F32),
        scratch_shapes=[pltpu.VMEM((WINDOW + tq, KV_DIM), F32), pltpu.VMEM((WINDOW + tq, KV_DIM), F32),
                        pltpu.VMEM((ATTN_UNITS_PER_ITER, GROUP * Qn, KV_DIM), F32)],
        compiler_params=_cparams(2),
        name="attn_prompt",
    )(q, k, k, v, v, bias, sink)


def _attn_sample(q, k, v, cache_k, cache_v, bias, sink, *, S, L):
    return pl.pallas_call(
        functools.partial(_attn_sample_kernel, S=S, L=L),
        out_shape=jax.ShapeDtypeStruct((S * L, D_MODEL), F32),
        scratch_shapes=[pltpu.VMEM((S, KEY_WIN, KV_DIM), F32), pltpu.VMEM((S, KEY_WIN, KV_DIM), F32),
                        pltpu.VMEM((GROUP * L, KV_DIM), F32)],
        compiler_params=pltpu.CompilerParams(vmem_limit_bytes=VMEM_LIMIT),
        name="attn_sample",
    )(q, cache_k, k, cache_v, v, bias, sink)


def _residual_value(m, x, g_ref, ga):
    inv = lax.rsqrt(jnp.mean(m * m, axis=-1, keepdims=True) + EPS)
    return x + m * inv * (g_ref[...] * ga)


def _attn_out_kernel(o_ref, w_ref, x_ref, g_ref, ga_ref, x1_ref, *, S, L):
    tm = o_ref.shape[0]
    if S > 1:
        x1_ref[...] = _dot(o_ref[...], w_ref[...])
        _residual(x1_ref, x_ref, g_ref, ga_ref, x1_ref, S, L)
        return
    rb = tm // PROJ_BLOCKS
    w = w_ref[...]
    for blk in range(PROJ_BLOCKS):
        rows = slice(blk * rb, (blk + 1) * rb)
        x1_ref[rows, :] = _residual_value(_dot(o_ref[rows, :], w), x_ref[rows, :], g_ref, ga_ref[0:1, :])


def _attn_out(o, x, mod, g_norm, w_o, *, layer, S, L, tm):
    Bp, T, _ = x.shape
    return pl.pallas_call(
        functools.partial(_attn_out_kernel, S=S, L=L),
        grid=(Bp, T // tm),
        in_specs=[_row_spec(tm),
                  pl.BlockSpec((None, D_MODEL, D_MODEL), lambda b, t: (0, 0, 0), pipeline_mode=pl.Buffered(1)),
                  _row_spec(tm), _vec_spec(4 * layer + 1), _mod_spec(S, layer, 2)],
        out_specs=_row_spec(tm),
        out_shape=jax.ShapeDtypeStruct((Bp, T, D_MODEL), F32),
        compiler_params=_cparams(2),
        name="attn_out",
    )(o, w_o, x, g_norm, mod)


def kernel(x_prompt, x_sample, c_prompt, c_sample, cache_conv, cache_k, cache_v, cache_ffn, w_mod, b_mod, g_norm, conv_w_in, conv_b_in, conv_w_dw, conv_b_dw, conv_ln_g, conv_ln_b, conv_w_out, conv_b_out, attn_w_q, attn_w_k, attn_w_v, attn_w_o, attn_sinks, rel_bias, ffn_w_up, ffn_w_dw, ffn_b_dw, ffn_w_down):
    B, T, D = x_prompt.shape
    SB, SL, _ = x_sample.shape
    assert SB == MOD_ROWS_SAMPLE
    c_all = jnp.concatenate(
        [c_sample, jnp.pad(c_prompt[:, None, :], ((0, 0), (0, SUBLANES - 1), (0, 0))).reshape(B * SUBLANES, D)], axis=0)
    mod = _modulation(c_all, w_mod, b_mod)
    gn = g_norm.reshape(DEPTH * 4, 1, D)

    geo_p = dict(S=1, L=1024, tm=1024)
    geo_p_small = dict(S=1, L=512, tm=512)
    geo_s = dict(S=SB, L=SL, tm=SB * SL)

    xp = x_prompt
    xs = x_sample.reshape(1, SB * SL, D)

    yp, tail_p = _conv_mix(xp, mod, gn, conv_w_in, conv_b_in, conv_w_dw, conv_b_dw, layer=0, tm=1024, R=64, nb=2)
    xp = _conv_proj(yp, xp, mod, gn, conv_ln_g, conv_ln_b, conv_w_out, conv_b_out, layer=0, tm=512)
    us = _conv_in(xs, mod, gn, conv_w_in, conv_b_in, layer=0, **geo_s)
    conv_args = (conv_w_dw, conv_b_dw, conv_ln_g, conv_ln_b, conv_w_out, conv_b_out)
    xs = _conv_out(us, cache_conv[0], xs, mod, gn, *conv_args, layer=0, R=SL, **geo_s)
    conv_state_p = tail_p[:, -1, CONV_HIST - (CONV_K - 1):, :][None]
    conv_state_s = us.reshape(SB, SL, D)[:, SL - (CONV_K - 1):, :][None]

    ffn_p, ffn_s = [], []
    ffn_hist = jnp.pad(cache_ffn, ((0, 0), (0, 0), (SUBLANES - (FFN_K - 1), 0), (0, 0)))

    def run_ffn(i, xp, xs):
        w = (ffn_w_up, ffn_w_dw, ffn_b_dw, ffn_w_down)
        xp, st_p = _ffn(xp, None, mod, gn, *w, layer=i, tf=512, nb=2, **geo_p)
        xs, st_s = _ffn(xs, ffn_hist[i], mod, gn, *w, layer=i, tf=512, nb=2, **geo_s)
        ffn_p.append(st_p[:, -1, 0])
        ffn_s.append(st_s[0, 0])
        return xp, xs

    xp, xs = run_ffn(0, xp, xs)

    qp, kp, vp = _qkv(xp, mod, gn, attn_w_q, attn_w_k, attn_w_v, layer=1, **geo_p_small)
    qs, ks, vs = _qkv(xs, mod, gn, attn_w_q, attn_w_k, attn_w_v, layer=1, **geo_s)
    P = cache_k.shape[2]
    assert P == WINDOW and P + SL <= KEY_WIN
    bias_p = _bias_table(rel_bias, 2 * CHUNK,
                         lambda q: (q // CHUNK) * CHUNK, lambda q: (q // CHUNK) * CHUNK + WINDOW + CHUNK, 2)
    bias_s = _bias_table(rel_bias, SL, lambda q: 0 * q, lambda q: 0 * q + P + SL, 1)
    sink_p = jnp.repeat(attn_sinks[0], 2 * CHUNK).reshape(N_KV_HEADS, 1, GROUP * 2 * CHUNK)
    sink_s = jnp.repeat(attn_sinks[0], SL).reshape(N_KV_HEADS, 1, GROUP * SL)
    op = _attn_prompt(qp, kp, vp, bias_p, sink_p, tq=512)
    ck = cache_k[0].reshape(SB, P, KV_DIM)
    cv = cache_v[0].reshape(SB, P, KV_DIM)
    os_ = _attn_sample(qs[0], ks[0], vs[0], ck, cv, bias_s, sink_s, S=SB, L=SL)[None]
    xp = _attn_out(op, xp, mod, gn, attn_w_o, layer=1, **geo_p_small)
    xs = _attn_out(os_, xs, mod, gn, attn_w_o, layer=1, **geo_s)
    keep = min(WINDOW, T)
    k_state_p = kp[:, T - keep:].reshape(B, keep, N_KV_HEADS, HEAD_DIM)[None]
    v_state_p = vp[:, T - keep:].reshape(B, keep, N_KV_HEADS, HEAD_DIM)[None]
    k_state_s = jnp.concatenate([ck, ks.reshape(SB, SL, KV_DIM)], axis=1)[:, SL:]
    v_state_s = jnp.concatenate([cv, vs.reshape(SB, SL, KV_DIM)], axis=1)[:, SL:]
    k_state_s = k_state_s.reshape(SB, P, N_KV_HEADS, HEAD_DIM)[None]
    v_state_s = v_state_s.reshape(SB, P, N_KV_HEADS, HEAD_DIM)[None]

    xp, xs = run_ffn(1, xp, xs)

    return (xp, xs.reshape(SB, SL, D), conv_state_p, conv_state_s,
            k_state_p, v_state_p, k_state_s, v_state_s,
            jnp.stack(ffn_p), jnp.stack(ffn_s))
```

```python
import functools
import math

import jax
import jax.numpy as jnp
from jax import lax
from jax.experimental import pallas as pl
from jax.experimental.pallas import tpu as pltpu

F32 = jnp.float32
BF16 = jnp.bfloat16

D_MODEL = 2048
D_FF = 5632
DEPTH = 2
CONV_K = 31
FFN_K = 3
N_HEADS = 32
N_KV_HEADS = 4
GROUP = N_HEADS // N_KV_HEADS
HEAD_DIM = 64
KV_DIM = N_KV_HEADS * HEAD_DIM
CHUNK = 64
WINDOW = 128
N_BUCKETS = 32
MAX_DISTANCE = 128
EPS = 1e-6
NEG_INF = -1e30
ATTN_SCALE = HEAD_DIM ** -0.5

SUBLANES = 8
LANES = 128
CONV_HIST = 32
KEY_WIN = 2 * WINDOW
VMEM_LIMIT = 58 * 1024 * 1024
VMEM_LIMIT_FFN = 62 * 1024 * 1024
MOD_ROWS_SAMPLE = 16


def _cparams(n_grid):
    return pltpu.CompilerParams(dimension_semantics=("arbitrary",) * n_grid,
                                vmem_limit_bytes=VMEM_LIMIT)


def _dot(a, b):
    return jnp.dot(a, b, preferred_element_type=F32)


ROW_SLAB = 64
ROW_GROUP = 2
PROJ_BLOCKS = 2
ATTN_UNITS_PER_ITER = 4


def _row_loop(S, L, srcs, dst, fn):
    slab = min(L, ROW_SLAB)
    n = L // slab
    group = ROW_GROUP if n % ROW_GROUP == 0 else 1
    for s in range(S):

        def body(r, c, s=s):
            rows = [pl.ds(pl.multiple_of(s * L + (r * group + i) * slab, slab), slab) for i in range(group)]
            vals = [[src[rw, :] for src in srcs] for rw in rows]
            outs = [fn(s, *v) for v in vals]
            for rw, out in zip(rows, outs):
                dst[rw, :] = out.astype(dst.dtype)
            return c

        lax.fori_loop(0, n // group, body, 0)


def _norm_mod(x_ref, dst_ref, g_ref, sc_ref, sh_ref, S, L):
    def fn(s, x):
        inv = lax.rsqrt(jnp.mean(x * x, axis=-1, keepdims=True) + EPS)
        mul = g_ref[...] * (1.0 + sc_ref[s:s + 1, :])
        return x * inv * mul + sh_ref[s:s + 1, :]

    _row_loop(S, L, [x_ref], dst_ref, fn)


def _residual(m_ref, x_ref, g_ref, ga_ref, out_ref, S, L):
    def fn(s, m, x):
        inv = lax.rsqrt(jnp.mean(m * m, axis=-1, keepdims=True) + EPS)
        return x + m * inv * (g_ref[...] * ga_ref[s:s + 1, :])

    _row_loop(S, L, [m_ref, x_ref], out_ref, fn)


def _row_spec(tm):
    return pl.BlockSpec((None, tm, D_MODEL), lambda b, t, *_: (b, t, 0))


def _vec_spec(index, n=D_MODEL):
    return pl.BlockSpec((None, 1, n), lambda *_: (index, 0, 0))


def _mod_spec(S, layer, which):
    if S == 1:
        first = MOD_ROWS_SAMPLE // SUBLANES
        return pl.BlockSpec((None, None, SUBLANES, D_MODEL), lambda b, t, *_: (layer, which, first + b, 0))
    return pl.BlockSpec((None, None, S, D_MODEL), lambda b, t, *_: (layer, which, 0, 0))


def _mod_kernel(c_ref, w_ref, b_ref, o_ref):
    c = c_ref[...]
    o_ref[...] = _dot(c * jax.nn.sigmoid(c), w_ref[...]) + b_ref[...]


MOD_HEAD_VECS = 2
MOD_BLOCK = 512


def _modulation_head(c_all, w_mod, b_mod):
    rows = c_all.shape[0]
    tn = 1024
    per = D_MODEL // tn
    return pl.pallas_call(
        _mod_kernel,
        grid=(MOD_HEAD_VECS * per,),
        in_specs=[pl.BlockSpec((rows, D_MODEL), lambda j: (0, 0)),
                  pl.BlockSpec((None, D_MODEL, tn), lambda j: (0, 0, j)),
                  pl.BlockSpec((None, 1, tn), lambda j: (0, 0, j))],
        out_specs=pl.BlockSpec((None, None, rows, tn), lambda j: (0, j // per, 0, j % per)),
        out_shape=jax.ShapeDtypeStruct((1, 6, rows, D_MODEL), F32),
        compiler_params=_cparams(1),
        name="modulation_head",
    )(c_all, w_mod, b_mod.reshape(DEPTH, 1, 6 * D_MODEL))


def _conv_in_kernel(x_ref, g_ref, sc_ref, sh_ref, wa_ref, wg_ref, ba_ref, bg_ref, u_ref, h_ref, *, S, L):
    @pl.when(pl.program_id(2) == 0)
    def _():
        _norm_mod(x_ref, h_ref, g_ref, sc_ref, sh_ref, S, L)

    h = h_ref[...]
    a = _dot(h, wa_ref[...]) + ba_ref[...]
    g = _dot(h, wg_ref[...]) + bg_ref[...]
    u_ref[...] = a * jax.nn.sigmoid(g)


def _conv_in(x, mod, g_norm, w_in, b_in, *, layer, S, L, tm):
    Bp, T, _ = x.shape
    tn = 512
    J = D_MODEL // tn
    b2 = b_in.reshape(-1, 1, 2 * D_MODEL)
    return pl.pallas_call(
        functools.partial(_conv_in_kernel, S=S, L=L),
        grid=(Bp, T // tm, J),
        in_specs=[_row_spec(tm), _vec_spec(4 * layer + 0), _mod_spec(S, layer, 1), _mod_spec(S, layer, 0),
                  pl.BlockSpec((None, D_MODEL, tn), lambda b, t, j: (0, 0, j)),
                  pl.BlockSpec((None, D_MODEL, tn), lambda b, t, j: (0, 0, j + J)),
                  pl.BlockSpec((None, 1, tn), lambda b, t, j: (0, 0, j)),
                  pl.BlockSpec((None, 1, tn), lambda b, t, j: (0, 0, j + J))],
        out_specs=pl.BlockSpec((None, tm, tn), lambda b, t, j: (b, t, j)),
        out_shape=jax.ShapeDtypeStruct((Bp, T, D_MODEL), F32),
        scratch_shapes=[pltpu.VMEM((tm, D_MODEL), F32)],
        compiler_params=_cparams(3),
        name="conv_in",
    )(x, g_norm, mod, mod, w_in, w_in, b2, b2)


def _dwconv_block(win, wb_ref, bias, cols, R):
    n_groups = (CONV_K + 1 + SUBLANES) // SUBLANES
    win_rows = R + CONV_HIST
    acc = jnp.broadcast_to(bias, (R // SUBLANES, SUBLANES, LANES))
    for sh in range(SUBLANES):
        wsh = win if sh == 0 else pltpu.roll(win, win_rows - sh, axis=0)
        for a in range(n_groups):
            k = SUBLANES * a + sh - 2
            if 0 <= k < CONV_K:
                tap = wsh[SUBLANES * a:SUBLANES * a + R, :]
                acc = acc + tap.reshape(R // SUBLANES, SUBLANES, LANES) * wb_ref[k, :, cols]
    return acc.reshape(R, LANES)


def _fill_tap_table(wdw_ref, wb_ref):
    for k in range(CONV_K):
        wb_ref[k] = jnp.broadcast_to(wdw_ref[k:k + 1, :], (SUBLANES, D_MODEL))


def _conv_mix_kernel(x_ref, g_ref, sc_ref, sh_ref, wa_ref, wg_ref, ba_ref, bg_ref, wdw_ref, bdw_ref,
                     c_ref, wm_ref, bm_ref,
                     y_ref, tail_ref, mod_ref, h_ref, wb_ref, carry_ref, *, L, R, nb, n_mod):
    t = pl.program_id(1)
    j = pl.program_id(2)
    tn = wa_ref.shape[1]
    rb = L // nb
    step = (pl.program_id(0) * pl.num_programs(1) + t) * pl.num_programs(2) + j

    @pl.when(step < n_mod)
    def _():
        c = c_ref[...]
        mod_ref[...] = _dot(c * jax.nn.sigmoid(c), wm_ref[...]) + bm_ref[...]

    @pl.when((pl.program_id(0) == 0) & (t == 0) & (j == 0))
    def _():
        _fill_tap_table(wdw_ref, wb_ref)
        carry_ref[...] = jnp.zeros(carry_ref.shape, F32)

    @pl.when(j == 0)
    def _():
        _norm_mod(x_ref, h_ref, g_ref, sc_ref, sh_ref, 1, L)

    wa = wa_ref[...].astype(BF16)
    wg = wg_ref[...].astype(BF16)
    tail = jnp.where(t == 0, 0.0, carry_ref[j])
    for blk in range(nb):
        hb = h_ref[blk * rb:(blk + 1) * rb, :]
        a = _dot(hb, wa) + ba_ref[...]
        g = _dot(hb, wg) + bg_ref[...]
        u = a * jax.nn.sigmoid(g)
        ufull = jnp.concatenate([tail, u], axis=0)
        for cb in range(tn // LANES):
            gcols = pl.ds(pl.multiple_of(j * tn + cb * LANES, LANES), LANES)
            lcols = slice(cb * LANES, (cb + 1) * LANES)
            bias = bdw_ref[:, gcols]
            for r in range(rb // R):
                win = ufull[r * R:r * R + R + CONV_HIST, lcols]
                y_ref[blk * rb + r * R:blk * rb + (r + 1) * R, lcols] = _dwconv_block(win, wb_ref, bias, gcols, R)
        tail = u[rb - CONV_HIST:, :]
    carry_ref[j] = tail
    tail_ref[...] = tail


def _conv_mix(x, mod_head, g_norm, w_in, b_in, w_dw, b_dw, c_all, w_mod, b_mod, *, layer, tm, R, nb):
    Bp, T, _ = x.shape
    tn = 256
    J = D_MODEL // tn
    nT = T // tm
    b2 = b_in.reshape(-1, 1, 2 * D_MODEL)
    rows = c_all.shape[0]
    per_vec = D_MODEL // MOD_BLOCK
    per_layer = 6 * per_vec
    skip = MOD_HEAD_VECS * per_vec
    n_mod = DEPTH * per_layer - skip
    assert n_mod <= Bp * nT * J

    def mod_block(b, t, j):
        sb = jnp.minimum((b * nT + t) * J + j, n_mod - 1) + skip
        return sb // per_layer, sb % per_layer

    def wm_map(b, t, j):
        layer_, cb = mod_block(b, t, j)
        return layer_, 0, cb

    def mod_out_map(b, t, j):
        layer_, cb = mod_block(b, t, j)
        return layer_, cb // per_vec, 0, cb % per_vec

    return pl.pallas_call(
        functools.partial(_conv_mix_kernel, L=tm, R=R, nb=nb, n_mod=n_mod),
        grid=(Bp, nT, J),
        in_specs=[_row_spec(tm), _vec_spec(4 * layer + 0), _mod_spec(1, layer, 1), _mod_spec(1, layer, 0),
                  pl.BlockSpec((None, D_MODEL, tn), lambda b, t, j: (0, 0, j)),
                  pl.BlockSpec((None, D_MODEL, tn), lambda b, t, j: (0, 0, j + J)),
                  pl.BlockSpec((None, 1, tn), lambda b, t, j: (0, 0, j)),
                  pl.BlockSpec((None, 1, tn), lambda b, t, j: (0, 0, j + J)),
                  pl.BlockSpec((None, CONV_K, D_MODEL), lambda b, t, j: (0, 0, 0)),
                  _vec_spec(0),
                  pl.BlockSpec((rows, D_MODEL), lambda b, t, j: (0, 0)),
                  pl.BlockSpec((None, D_MODEL, MOD_BLOCK), wm_map),
                  pl.BlockSpec((None, 1, MOD_BLOCK), wm_map)],
        out_specs=[pl.BlockSpec((None, tm, tn), lambda b, t, j: (b, t, j)),
                   pl.BlockSpec((None, None, CONV_HIST, tn), lambda b, t, j: (b, t, 0, j)),
                   pl.BlockSpec((None, None, rows, MOD_BLOCK), mod_out_map)],
        out_shape=[jax.ShapeDtypeStruct((Bp, T, D_MODEL), F32),
                   jax.ShapeDtypeStruct((Bp, nT, CONV_HIST, D_MODEL), F32),
                   jax.ShapeDtypeStruct((DEPTH, 6, rows, D_MODEL), F32)],
        scratch_shapes=[pltpu.VMEM((tm, D_MODEL), BF16),
                        pltpu.VMEM((CONV_K, SUBLANES, D_MODEL), F32),
                        pltpu.VMEM((J, CONV_HIST, tn), F32)],
        compiler_params=_cparams(3),
        name="conv_mix",
    )(x, g_norm, mod_head, mod_head, w_in, w_in, b2, b2, w_dw, b_dw.reshape(-1, 1, D_MODEL),
      c_all, w_mod, b_mod.reshape(DEPTH, 1, 6 * D_MODEL))


def _conv_proj_kernel(y_ref, lng_ref, lnb_ref, wout_ref, bout_ref, x_ref, g_ref, ga_ref, x1_ref, *, L):
    rb = L // PROJ_BLOCKS
    w = wout_ref[...]
    for blk in range(PROJ_BLOCKS):
        rows = slice(blk * rb, (blk + 1) * rb)
        a = _ln_silu_value(y_ref[rows, :], lng_ref, lnb_ref)
        m = _dot(a, w) + bout_ref[...]
        x1_ref[rows, :] = _residual_value(m, x_ref[rows, :], g_ref, ga_ref[0:1, :])


def _conv_proj(y, x, mod, g_norm, ln_g, ln_b, w_out, b_out, *, layer, tm):
    Bp, T, _ = x.shape
    vec = lambda a: a.reshape(-1, 1, D_MODEL)
    return pl.pallas_call(
        functools.partial(_conv_proj_kernel, L=tm),
        grid=(Bp, T // tm),
        in_specs=[_row_spec(tm), _vec_spec(0), _vec_spec(0),
                  pl.BlockSpec((None, D_MODEL, D_MODEL), lambda b, t: (0, 0, 0), pipeline_mode=pl.Buffered(1)),
                  _vec_spec(0), _row_spec(tm), _vec_spec(4 * layer + 1), _mod_spec(1, layer, 2)],
        out_specs=_row_spec(tm),
        out_shape=jax.ShapeDtypeStruct((Bp, T, D_MODEL), F32),
        compiler_params=_cparams(2),
        name="conv_proj",
    )(y, vec(ln_g), vec(ln_b), w_out, vec(b_out), x, g_norm, mod)


def _dwconv(full_ref, wb_ref, bdw_ref, y_ref, S, L, R):
    def body(cb, c):
        cols = pl.ds(pl.multiple_of(cb * LANES, LANES), LANES)
        bias = bdw_ref[:, cols]
        for s in range(S):
            for r in range(L // R):
                win = full_ref[s, r * R:r * R + R + CONV_HIST, cols]
                y_ref[s * L + r * R:s * L + (r + 1) * R, cols] = _dwconv_block(win, wb_ref, bias, cols, R)
        return c

    lax.fori_loop(0, D_MODEL // LANES, body, 0)


def _ln_silu_value(y, lng_ref, lnb_ref):
    mu = jnp.mean(y, axis=-1, keepdims=True)
    yc = y - mu
    var = jnp.mean(yc * yc, axis=-1, keepdims=True)
    z = yc * lax.rsqrt(var + EPS) * lng_ref[...] + lnb_ref[...]
    return z * jax.nn.sigmoid(z)


def _ln_silu(y_ref, lng_ref, lnb_ref, S, L):
    _row_loop(S, L, [y_ref], y_ref, lambda s, y: _ln_silu_value(y, lng_ref, lnb_ref))


def _conv_out_kernel(u_ref, hist_ref, wdw_ref, bdw_ref, lng_ref, lnb_ref, wout_ref, bout_ref, x_ref, g_ref, ga_ref,
                     x1_ref, full_ref, wb_ref, y_ref, *, S, L, R):
    _fill_tap_table(wdw_ref, wb_ref)
    pad = CONV_HIST - (CONV_K - 1)
    full_ref[:, 0:pad, :] = jnp.zeros((S, pad, D_MODEL), F32)
    full_ref[:, pad:CONV_HIST, :] = hist_ref[...]
    full_ref[:, CONV_HIST:CONV_HIST + L, :] = u_ref[...].reshape(S, L, D_MODEL)
    _dwconv(full_ref, wb_ref, bdw_ref, y_ref, S, L, R)
    _ln_silu(y_ref, lng_ref, lnb_ref, S, L)
    x1_ref[...] = _dot(y_ref[...], wout_ref[...]) + bout_ref[...]
    _residual(x1_ref, x_ref, g_ref, ga_ref, x1_ref, S, L)


def _conv_out(u, hist, x, mod, g_norm, w_dw, b_dw, ln_g, ln_b, w_out, b_out, *, layer, S, L, tm, R):
    Bp, T, _ = x.shape
    assert Bp == 1 and T == tm
    vec = lambda a: a.reshape(-1, 1, D_MODEL)
    return pl.pallas_call(
        functools.partial(_conv_out_kernel, S=S, L=L, R=R),
        grid=(Bp, T // tm),
        in_specs=[_row_spec(tm), pl.BlockSpec((S, CONV_K - 1, D_MODEL), lambda b, t: (0, 0, 0)),
                  pl.BlockSpec((None, CONV_K, D_MODEL), lambda b, t: (0, 0, 0)),
                  _vec_spec(0), _vec_spec(0), _vec_spec(0),
                  pl.BlockSpec((None, D_MODEL, D_MODEL), lambda b, t: (0, 0, 0), pipeline_mode=pl.Buffered(1)),
                  _vec_spec(0), _row_spec(tm), _vec_spec(4 * layer + 1), _mod_spec(S, layer, 2)],
        out_specs=_row_spec(tm),
        out_shape=jax.ShapeDtypeStruct((Bp, T, D_MODEL), F32),
        scratch_shapes=[pltpu.VMEM((S, CONV_HIST + L, D_MODEL), F32),
                        pltpu.VMEM((CONV_K, SUBLANES, D_MODEL), F32),
                        pltpu.VMEM((tm, D_MODEL), F32)],
        compiler_params=_cparams(2),
        name="conv_out",
    )(u, hist, w_dw, vec(b_dw), vec(ln_g), vec(ln_b), w_out, vec(b_out), x, g_norm, mod)


def _ffn_kernel(*refs, S, L, nb, prompt):
    if prompt:
        (x_ref, g2_ref, sc_ref, sh_ref, wg_ref, wv_ref, wdw_ref, bdw_ref, wd_ref, g3_ref, ga_ref,
         x2_ref, st_ref, h_ref, carry_ref) = refs
    else:
        (x_ref, g2_ref, sc_ref, sh_ref, wg_ref, wv_ref, wdw_ref, bdw_ref, wd_ref, g3_ref, ga_ref, hist_ref,
         x2_ref, st_ref, h_ref) = refs
    t = pl.program_id(1)
    j = pl.program_id(2)
    last_j = pl.num_programs(2) - 1
    tm = x_ref.shape[0]
    rb = tm // nb
    seg = min(L, rb)
    pad = SUBLANES
    inline = S == 1

    mm_dtype = h_ref.dtype

    def step(first, last):
        wg = wg_ref[...].astype(mm_dtype)
        wv = wv_ref[...].astype(mm_dtype)
        wd = wd_ref[...].astype(mm_dtype)
        w0 = wdw_ref[0:1, :]
        w1 = wdw_ref[1:2, :]
        w2 = wdw_ref[2:3, :]
        bd = bdw_ref[...]
        tail = None
        for blk in range(nb):
            rows = slice(blk * rb, (blk + 1) * rb)
            if first:
                x = x_ref[rows, :]
                inv = lax.rsqrt(jnp.mean(x * x, axis=-1, keepdims=True) + EPS)
                hb = (x * inv * (g2_ref[...] * (1.0 + sc_ref[0:1, :])) + sh_ref[0:1, :]).astype(mm_dtype)
                h_ref[rows, :] = hb
            else:
                hb = h_ref[rows, :]
            g = _dot(hb, wg)
            v = _dot(hb, wv)
            p1s, p2s = [], []
            for q in range(rb // seg):
                gq = g[q * seg:(q + 1) * seg, :]
                if not prompt:
                    s = blk * (rb // seg) + q
                    hist = hist_ref[s]
                    st_ref[s] = gq[seg - (FFN_K - 1):, :]
                elif blk == 0:
                    hist = jnp.where(t == 0, 0.0, carry_ref[j])
                else:
                    hist = tail
                gfull = jnp.concatenate([hist, gq], axis=0)
                p1s.append(pltpu.roll(gfull, 1, axis=0)[pad:, :])
                p2s.append(pltpu.roll(gfull, 2, axis=0)[pad:, :])
                tail = gq[seg - pad:, :]
            p1 = p1s[0] if len(p1s) == 1 else jnp.concatenate(p1s, axis=0)
            p2 = p2s[0] if len(p2s) == 1 else jnp.concatenate(p2s, axis=0)
            gc = g * w2 + p1 * w1 + p2 * w0 + bd
            act = (jax.nn.gelu(gc) * v).astype(mm_dtype)
            acc = _dot(act, wd)
            if not first:
                acc = x2_ref[rows, :] + acc
            if last:
                acc = _residual_value(acc, x_ref[rows, :], g3_ref, ga_ref[0:1, :])
            x2_ref[rows, :] = acc
        if prompt:
            carry_ref[j] = tail
            st_ref[0] = tail[pad - (FFN_K - 1):, :]

    if inline:
        pl.when(j == 0)(lambda: step(True, False))
        pl.when(j > 0)(lambda: step(False, False))

        @pl.when(j == last_j)
        def _():
            _residual(x2_ref, x_ref, g3_ref, ga_ref, x2_ref, S, L)
    else:
        @pl.when(j == 0)
        def _():
            _norm_mod(x_ref, h_ref, g2_ref, sc_ref, sh_ref, S, L)
            x2_ref[...] = jnp.zeros(x2_ref.shape, F32)

        step(False, False)

        @pl.when(j == last_j)
        def _():
            _residual(x2_ref, x_ref, g3_ref, ga_ref, x2_ref, S, L)


def _ffn(x, hist, mod, g_norm, w_up, w_dw, b_dw, w_down, *, layer, S, L, tm, tf, nb):
    Bp, T, _ = x.shape
    J = D_FF // tf
    nT = T // tm
    prompt = hist is None
    x_spec = pl.BlockSpec((None, tm, D_MODEL), lambda b, t, j: (b, t, 0), pipeline_mode=pl.Buffered(1))
    in_specs = [x_spec, _vec_spec(4 * layer + 2), _mod_spec(S, layer, 4), _mod_spec(S, layer, 3),
                pl.BlockSpec((None, D_MODEL, tf), lambda b, t, j: (layer, 0, j)),
                pl.BlockSpec((None, D_MODEL, tf), lambda b, t, j: (layer, 0, j + J)),
                pl.BlockSpec((None, FFN_K, tf), lambda b, t, j: (layer, 0, j)),
                pl.BlockSpec((None, 1, tf), lambda b, t, j: (layer, 0, j)),
                pl.BlockSpec((None, tf, D_MODEL), lambda b, t, j: (layer, j, 0)),
                _vec_spec(4 * layer + 3), _mod_spec(S, layer, 5)]
    args = [x, g_norm, mod, mod, w_up, w_up, w_dw, b_dw.reshape(DEPTH, 1, D_FF), w_down, g_norm, mod]
    scratch = [pltpu.VMEM((tm, D_MODEL), BF16 if prompt else F32)]
    if prompt:
        scratch.append(pltpu.VMEM((J, SUBLANES, tf), F32))
    else:
        in_specs.append(pl.BlockSpec((S, SUBLANES, tf), lambda b, t, j: (0, 0, j)))
        args.append(hist)
    return pl.pallas_call(
        functools.partial(_ffn_kernel, S=S, L=L, nb=nb, prompt=prompt),
        grid=(Bp, nT, J),
        in_specs=in_specs,
        out_specs=[_row_spec(tm),
                   pl.BlockSpec((None, None, S, FFN_K - 1, tf), lambda b, t, j: (b, t, 0, 0, j))],
        out_shape=[jax.ShapeDtypeStruct((Bp, T, D_MODEL), F32),
                   jax.ShapeDtypeStruct((Bp, nT, S, FFN_K - 1, D_FF), F32)],
        scratch_shapes=scratch,
        compiler_params=pltpu.CompilerParams(dimension_semantics=("arbitrary",) * 3,
                                             vmem_limit_bytes=VMEM_LIMIT_FFN),
        name="ffn",
    )(*args)


def _qkv_kernel(x_ref, g_ref, sc_ref, sh_ref, wq_ref, wk_ref, wv_ref, q_ref, k_ref, v_ref, h_ref, *, S, L):
    tm = x_ref.shape[0]
    if S > 1:
        _norm_mod(x_ref, h_ref, g_ref, sc_ref, sh_ref, S, L)
        blocks = [(slice(0, tm), h_ref[...])]
    else:
        rb = tm // PROJ_BLOCKS
        mul = g_ref[...] * (1.0 + sc_ref[0:1, :])
        blocks = []
        for blk in range(PROJ_BLOCKS):
            rows = slice(blk * rb, (blk + 1) * rb)
            x = x_ref[rows, :]
            inv = lax.rsqrt(jnp.mean(x * x, axis=-1, keepdims=True) + EPS)
            blocks.append((rows, x * inv * mul + sh_ref[0:1, :]))
    wq = wq_ref[...]
    wk = wk_ref[...]
    wv = wv_ref[...]
    for rows, h in blocks:
        q_ref[rows, :] = _dot(h, wq)
        k_ref[rows, :] = _dot(h, wk)
        v_ref[rows, :] = _dot(h, wv)


def _qkv(x, mod, g_norm, w_q, w_k, w_v, *, layer, S, L, tm):
    Bp, T, _ = x.shape
    kv_spec = pl.BlockSpec((None, tm, KV_DIM), lambda b, t: (b, t, 0))
    resident = lambda n: pl.BlockSpec((None, D_MODEL, n), lambda b, t: (0, 0, 0), pipeline_mode=pl.Buffered(1))
    return pl.pallas_call(
        functools.partial(_qkv_kernel, S=S, L=L),
        grid=(Bp, T // tm),
        in_specs=[_row_spec(tm), _vec_spec(4 * layer + 0), _mod_spec(S, layer, 1), _mod_spec(S, layer, 0),
                  resident(D_MODEL), resident(KV_DIM), resident(KV_DIM)],
        out_specs=[_row_spec(tm), kv_spec, kv_spec],
        out_shape=[jax.ShapeDtypeStruct((Bp, T, D_MODEL), F32),
                   jax.ShapeDtypeStruct((Bp, T, KV_DIM), F32),
                   jax.ShapeDtypeStruct((Bp, T, KV_DIM), F32)],
        scratch_shapes=[pltpu.VMEM((tm, D_MODEL), F32)],
        compiler_params=_cparams(2),
        name="qkv",
    )(x, g_norm, mod, mod, w_q, w_k, w_v)


REL_SPAN = 3 * LANES


def _bias_kernel(rb_ref, prof_ref, valid_ref, o_ref, *, Qn):
    prof = prof_ref[...]
    valid = valid_ref[...] != 0
    early = lax.broadcasted_iota(jnp.int32, valid.shape, 0) < WINDOW
    for hh in range(N_HEADS):
        f = jnp.zeros(prof.shape, F32)
        for b in range(N_BUCKETS):
            f = jnp.where(prof == b, rb_ref[b, hh], f)
        h, g = divmod(hh, GROUP)
        off = (g * Qn) % LANES
        x = jnp.broadcast_to(f[0:1, :], (KEY_WIN, REL_SPAN))
        r = pltpu.roll(x, WINDOW + off, axis=1, stride=1, stride_axis=0)
        tab = jnp.where(valid, r[:, off:off + Qn], NEG_INF)
        o_ref[0, h, :, g * Qn:(g + 1) * Qn] = tab
        if o_ref.shape[0] > 1:
            o_ref[1, h, :, g * Qn:(g + 1) * Qn] = jnp.where(early, NEG_INF, tab)


def _t5_bucket(rel):
    half = N_BUCKETS // 2
    max_exact = half // 2
    n = jnp.abs(rel)
    ret = jnp.where(rel > 0, half, 0)
    nf = jnp.maximum(n, 1).astype(F32)
    large = max_exact + (jnp.log(nf / max_exact) / math.log(MAX_DISTANCE / max_exact)
                         * (half - max_exact)).astype(jnp.int32)
    large = jnp.minimum(large, half - 1)
    return ret + jnp.where(n < max_exact, n, large)


def _bias_table(rel_bias, Qn, lo, hi, variants):
    assert KEY_WIN + Qn - 1 <= REL_SPAN
    k = jnp.arange(KEY_WIN)[:, None]
    q = jnp.arange(Qn)[None, :]
    valid = ((k >= lo(q)) & (k < hi(q))).astype(jnp.int32)
    prof = jnp.broadcast_to(_t5_bucket(WINDOW - jnp.arange(REL_SPAN)).astype(jnp.int32)[None, :], (SUBLANES, REL_SPAN))
    return pl.pallas_call(
        functools.partial(_bias_kernel, Qn=Qn),
        in_specs=[pl.BlockSpec(memory_space=pltpu.SMEM),
                  pl.BlockSpec((SUBLANES, REL_SPAN), lambda: (0, 0)),
                  pl.BlockSpec((KEY_WIN, Qn), lambda: (0, 0))],
        out_specs=pl.BlockSpec((variants, N_KV_HEADS, KEY_WIN, GROUP * Qn), lambda: (0, 0, 0, 0)),
        out_shape=jax.ShapeDtypeStruct((variants, N_KV_HEADS, KEY_WIN, GROUP * Qn), F32),
        name="rel_bias_table",
    )(rel_bias, prof, valid)


def _attend(q_ref, r0, Qn, kw, vw, bias_ref, sel, sink_ref, o_ref, qs_ref):
    rows = pl.ds(r0, Qn)
    for g in range(GROUP):
        for h in range(N_KV_HEADS):
            src = (h * GROUP + g) * HEAD_DIM
            qs_ref[g * Qn:(g + 1) * Qn, h * HEAD_DIM:(h + 1) * HEAD_DIM] = q_ref[rows, src:src + HEAD_DIM] * ATTN_SCALE
    qt = qs_ref[...].T
    vt = vw.T
    outs = []
    for h in range(N_KV_HEADS):
        hd = slice(h * HEAD_DIM, (h + 1) * HEAD_DIM)
        s = _dot(kw[:, hd], qt[hd, :]) + bias_ref[sel, h]
        sk = sink_ref[h]
        mx = jnp.maximum(jnp.max(s, axis=0, keepdims=True), sk)
        p = jnp.exp(s - mx)
        den = jnp.sum(p, axis=0, keepdims=True) + jnp.exp(sk - mx)
        outs.append(_dot(vt[hd, :], p) * (1.0 / den))
    o2 = jnp.concatenate(outs, axis=0).T
    for g in range(GROUP):
        for h in range(N_KV_HEADS):
            dst = (h * GROUP + g) * HEAD_DIM
            o_ref[rows, dst:dst + HEAD_DIM] = o2[g * Qn:(g + 1) * Qn, h * HEAD_DIM:(h + 1) * HEAD_DIM]


def _attn_prompt_kernel(q_ref, kp_ref, kc_ref, vp_ref, vc_ref, bias_ref, sink_ref, o_ref, kw_ref, vw_ref, qs_ref,
                        *, tq, Qn):
    t = pl.program_id(1)
    kw_ref[0:WINDOW, :] = kp_ref[...]
    kw_ref[WINDOW:WINDOW + tq, :] = kc_ref[...]
    vw_ref[0:WINDOW, :] = vp_ref[...]
    vw_ref[WINDOW:WINDOW + tq, :] = vc_ref[...]

    def body(mp, carry):
        for u in range(ATTN_UNITS_PER_ITER):
            m = mp * ATTN_UNITS_PER_ITER + u
            r0 = pl.multiple_of(m * Qn, Qn)
            kw = kw_ref[pl.ds(r0, KEY_WIN), :]
            vw = vw_ref[pl.ds(r0, KEY_WIN), :]
            sel = ((t == 0) & (m == 0)).astype(jnp.int32)
            _attend(q_ref, r0, Qn, kw, vw, bias_ref, sel, sink_ref, o_ref, qs_ref.at[u])
        return carry

    lax.fori_loop(0, tq // (Qn * ATTN_UNITS_PER_ITER), body, 0)


def _attn_sample_kernel(q_ref, ck_ref, kn_ref, cv_ref, vn_ref, bias_ref, sink_ref, o_ref, kw_ref, vw_ref, qs_ref,
                        *, S, L):
    P = ck_ref.shape[1]
    kw_ref[:, 0:P, :] = ck_ref[...]
    kw_ref[:, P:P + L, :] = kn_ref[...].reshape(S, L, KV_DIM)
    kw_ref[:, P + L:, :] = jnp.zeros((S, KEY_WIN - P - L, KV_DIM), F32)
    vw_ref[:, 0:P, :] = cv_ref[...]
    vw_ref[:, P:P + L, :] = vn_ref[...].reshape(S, L, KV_DIM)
    vw_ref[:, P + L:, :] = jnp.zeros((S, KEY_WIN - P - L, KV_DIM), F32)

    def body(s, carry):
        r0 = pl.multiple_of(s * L, L)
        _attend(q_ref, r0, L, kw_ref[s], vw_ref[s], bias_ref, 0, sink_ref, o_ref, qs_ref)
        return carry

    lax.fori_loop(0, S, body, 0)


def _attn_prompt(q, k, v, bias, sink, *, tq):
    Bp, T, _ = q.shape
    Qn = 2 * CHUNK
    per = tq // WINDOW
    prev_spec = pl.BlockSpec((None, WINDOW, KV_DIM), lambda b, t: (b, jnp.maximum(t * per - 1, 0), 0))
    cur_spec = pl.BlockSpec((None, tq, KV_DIM), lambda b, t: (b, t, 0))
    return pl.pallas_call(
        functools.partial(_attn_prompt_kernel, tq=tq, Qn=Qn),
        grid=(Bp, T // tq),
        in_specs=[_row_spec(tq), prev_spec, cur_spec, prev_spec, cur_spec,
                  pl.BlockSpec((2, N_KV_HEADS, KEY_WIN, GROUP * Qn), lambda b, t: (0, 0, 0, 0),
                               pipeline_mode=pl.Buffered(1)),
                  pl.BlockSpec((N_KV_HEADS, 1, GROUP * Qn), lambda b, t: (0, 0, 0))],
        out_specs=_row_spec(tq),
        out_shape=jax.ShapeDtypeStruct((Bp, T, D_MODEL), F32),
        scratch_shapes=[pltpu.VMEM((WINDOW + tq, KV_DIM), F32), pltpu.VMEM((WINDOW + tq, KV_DIM), F32),
                        pltpu.VMEM((ATTN_UNITS_PER_ITER, GROUP * Qn, KV_DIM), F32)],
        compiler_params=_cparams(2),
        name="attn_prompt",
    )(q, k, k, v, v, bias, sink)


def _attn_sample(q, k, v, cache_k, cache_v, bias, sink, *, S, L):
    return pl.pallas_call(
        functools.partial(_attn_sample_kernel, S=S, L=L),
        out_shape=jax.ShapeDtypeStruct((S * L, D_MODEL), F32),
        scratch_shapes=[pltpu.VMEM((S, KEY_WIN, KV_DIM), F32), pltpu.VMEM((S, KEY_WIN, KV_DIM), F32),
                        pltpu.VMEM((GROUP * L, KV_DIM), F32)],
        compiler_params=pltpu.CompilerParams(vmem_limit_bytes=VMEM_LIMIT),
        name="attn_sample",
    )(q, cache_k, k, cache_v, v, bias, sink)


def _residual_value(m, x, g_ref, ga):
    inv = lax.rsqrt(jnp.mean(m * m, axis=-1, keepdims=True) + EPS)
    return x + m * inv * (g_ref[...] * ga)


def _attn_out_kernel(o_ref, w_ref, x_ref, g_ref, ga_ref, x1_ref, *, S, L):
    tm = o_ref.shape[0]
    if S > 1:
        x1_ref[...] = _dot(o_ref[...], w_ref[...])
        _residual(x1_ref, x_ref, g_ref, ga_ref, x1_ref, S, L)
        return
    rb = tm // PROJ_BLOCKS
    w = w_ref[...]
    for blk in range(PROJ_BLOCKS):
        rows = slice(blk * rb, (blk + 1) * rb)
        x1_ref[rows, :] = _residual_value(_dot(o_ref[rows, :], w), x_ref[rows, :], g_ref, ga_ref[0:1, :])


def _attn_out(o, x, mod, g_norm, w_o, *, layer, S, L, tm):
    Bp, T, _ = x.shape
    return pl.pallas_call(
        functools.partial(_attn_out_kernel, S=S, L=L),
        grid=(Bp, T // tm),
        in_specs=[_row_spec(tm),
                  pl.BlockSpec((None, D_MODEL, D_MODEL), lambda b, t: (0, 0, 0), pipeline_mode=pl.Buffered(1)),
                  _row_spec(tm), _vec_spec(4 * layer + 1), _mod_spec(S, layer, 2)],
        out_specs=_row_spec(tm),
        out_shape=jax.ShapeDtypeStruct((Bp, T, D_MODEL), F32),
        compiler_params=_cparams(2),
        name="attn_out",
    )(o, w_o, x, g_norm, mod)


def kernel(x_prompt, x_sample, c_prompt, c_sample, cache_conv, cache_k, cache_v, cache_ffn, w_mod, b_mod, g_norm, conv_w_in, conv_b_in, conv_w_dw, conv_b_dw, conv_ln_g, conv_ln_b, conv_w_out, conv_b_out, attn_w_q, attn_w_k, attn_w_v, attn_w_o, attn_sinks, rel_bias, ffn_w_up, ffn_w_dw, ffn_b_dw, ffn_w_down):
    B, T, D = x_prompt.shape
    SB, SL, _ = x_sample.shape
    assert SB == MOD_ROWS_SAMPLE
    c_all = jnp.concatenate(
        [c_sample, jnp.pad(c_prompt[:, None, :], ((0, 0), (0, SUBLANES - 1), (0, 0))).reshape(B * SUBLANES, D)], axis=0)
    mod_head = _modulation_head(c_all, w_mod, b_mod)
    gn = g_norm.reshape(DEPTH * 4, 1, D)

    geo_p = dict(S=1, L=1024, tm=1024)
    geo_p_small = dict(S=1, L=512, tm=512)
    geo_s = dict(S=SB, L=SL, tm=SB * SL)

    xp = x_prompt
    xs = x_sample.reshape(1, SB * SL, D)

    yp, tail_p, mod = _conv_mix(xp, mod_head, gn, conv_w_in, conv_b_in, conv_w_dw, conv_b_dw, c_all, w_mod, b_mod,
                                layer=0, tm=1024, R=64, nb=2)
    xp = _conv_proj(yp, xp, mod, gn, conv_ln_g, conv_ln_b, conv_w_out, conv_b_out, layer=0, tm=512)
    us = _conv_in(xs, mod_head, gn, conv_w_in, conv_b_in, layer=0, **geo_s)
    conv_args = (conv_w_dw, conv_b_dw, conv_ln_g, conv_ln_b, conv_w_out, conv_b_out)
    xs = _conv_out(us, cache_conv[0], xs, mod, gn, *conv_args, layer=0, R=SL, **geo_s)
    conv_state_p = tail_p[:, -1, CONV_HIST - (CONV_K - 1):, :][None]
    conv_state_s = us.reshape(SB, SL, D)[:, SL - (CONV_K - 1):, :][None]

    ffn_p, ffn_s = [], []
    ffn_hist = jnp.pad(cache_ffn, ((0, 0), (0, 0), (SUBLANES - (FFN_K - 1), 0), (0, 0)))

    def run_ffn(i, xp, xs):
        w = (ffn_w_up, ffn_w_dw, ffn_b_dw, ffn_w_down)
        xp, st_p = _ffn(xp, None, mod, gn, *w, layer=i, tf=512, nb=2, **geo_p)
        xs, st_s = _ffn(xs, ffn_hist[i], mod, gn, *w, layer=i, tf=512, nb=2, **geo_s)
        ffn_p.append(st_p[:, -1, 0])
        ffn_s.append(st_s[0, 0])
        return xp, xs

    xp, xs = run_ffn(0, xp, xs)

    qp, kp, vp = _qkv(xp, mod, gn, attn_w_q, attn_w_k, attn_w_v, layer=1, **geo_p_small)
    qs, ks, vs = _qkv(xs, mod, gn, attn_w_q, attn_w_k, attn_w_v, layer=1, **geo_s)
    P = cache_k.shape[2]
    assert P == WINDOW and P + SL <= KEY_WIN
    bias_p = _bias_table(rel_bias, 2 * CHUNK,
                         lambda q: (q // CHUNK) * CHUNK, lambda q: (q // CHUNK) * CHUNK + WINDOW + CHUNK, 2)
    bias_s = _bias_table(rel_bias, SL, lambda q: 0 * q, lambda q: 0 * q + P + SL, 1)
    sink_p = jnp.repeat(attn_sinks[0], 2 * CHUNK).reshape(N_KV_HEADS, 1, GROUP * 2 * CHUNK)
    sink_s = jnp.repeat(attn_sinks[0], SL).reshape(N_KV_HEADS, 1, GROUP * SL)
    op = _attn_prompt(qp, kp, vp, bias_p, sink_p, tq=512)
    ck = cache_k[0].reshape(SB, P, KV_DIM)
    cv = cache_v[0].reshape(SB, P, KV_DIM)
    os_ = _attn_sample(qs[0], ks[0], vs[0], ck, cv, bias_s, sink_s, S=SB, L=SL)[None]
    xp = _attn_out(op, xp, mod, gn, attn_w_o, layer=1, **geo_p_small)
    xs = _attn_out(os_, xs, mod, gn, attn_w_o, layer=1, **geo_s)
    keep = min(WINDOW, T)
    k_state_p = kp[:, T - keep:].reshape(B, keep, N_KV_HEADS, HEAD_DIM)[None]
    v_state_p = vp[:, T - keep:].reshape(B, keep, N_KV_HEADS, HEAD_DIM)[None]
    k_state_s = jnp.concatenate([ck, ks.reshape(SB, SL, KV_DIM)], axis=1)[:, SL:]
    v_state_s = jnp.concatenate([cv, vs.reshape(SB, SL, KV_DIM)], axis=1)[:, SL:]
    k_state_s = k_state_s.reshape(SB, P, N_KV_HEADS, HEAD_DIM)[None]
    v_state_s = v_state_s.reshape(SB, P, N_KV_HEADS, HEAD_DIM)[None]

    xp, xs = run_ffn(1, xp, xs)

    return (xp, xs.reshape(SB, SL, D), conv_state_p, conv_state_s,
            k_state_p, v_state_p, k_state_s, v_state_s,
            jnp.stack(ffn_p), jnp.stack(ffn_s))
```

```python
import functools
import math

import jax
import jax.numpy as jnp
from jax import lax
from jax.experimental import pallas as pl
from jax.experimental.pallas import tpu as pltpu

F32 = jnp.float32
BF16 = jnp.bfloat16

D_MODEL = 2048
D_FF = 5632
DEPTH = 2
CONV_K = 31
FFN_K = 3
N_HEADS = 32
N_KV_HEADS = 4
GROUP = N_HEADS // N_KV_HEADS
HEAD_DIM = 64
KV_DIM = N_KV_HEADS * HEAD_DIM
CHUNK = 64
WINDOW = 128
N_BUCKETS = 32
MAX_DISTANCE = 128
EPS = 1e-6
NEG_INF = -1e30
ATTN_SCALE = HEAD_DIM ** -0.5

SUBLANES = 8
LANES = 128
CONV_HIST = 32
KEY_WIN = 2 * WINDOW
VMEM_LIMIT = 58 * 1024 * 1024
VMEM_LIMIT_FFN = 62 * 1024 * 1024
MOD_ROWS_SAMPLE = 16


def _cparams(n_grid):
    return pltpu.CompilerParams(dimension_semantics=("arbitrary",) * n_grid,
                                vmem_limit_bytes=VMEM_LIMIT)


def _dot(a, b):
    return jnp.dot(a, b, preferred_element_type=F32)


ROW_SLAB = 64
ROW_GROUP = 2
PROJ_BLOCKS = 2

STREAM_ROWS = 1024
RESIDENT_ROWS = 512
FFN_CHUNK = 512
FFN_ROW_BLOCKS = 2
MIX_CHUNK = 256
MIX_ROW_BLOCKS = 2
MIX_CONV_ROWS = 64
IN_PROJ_CHUNK = 512
MOD_HEAD_CHUNK = 1024
ATTN_UNITS_PER_ITER = 4


def _row_loop(S, L, srcs, dst, fn):
    slab = min(L, ROW_SLAB)
    n = L // slab
    group = ROW_GROUP if n % ROW_GROUP == 0 else 1
    for s in range(S):

        def body(r, c, s=s):
            rows = [pl.ds(pl.multiple_of(s * L + (r * group + i) * slab, slab), slab) for i in range(group)]
            vals = [[src[rw, :] for src in srcs] for rw in rows]
            outs = [fn(s, *v) for v in vals]
            for rw, out in zip(rows, outs):
                dst[rw, :] = out.astype(dst.dtype)
            return c

        lax.fori_loop(0, n // group, body, 0)


def _norm_mod(x_ref, dst_ref, g_ref, sc_ref, sh_ref, S, L):
    def fn(s, x):
        inv = lax.rsqrt(jnp.mean(x * x, axis=-1, keepdims=True) + EPS)
        mul = g_ref[...] * (1.0 + sc_ref[s:s + 1, :])
        return x * inv * mul + sh_ref[s:s + 1, :]

    _row_loop(S, L, [x_ref], dst_ref, fn)


def _residual(m_ref, x_ref, g_ref, ga_ref, out_ref, S, L):
    def fn(s, m, x):
        inv = lax.rsqrt(jnp.mean(m * m, axis=-1, keepdims=True) + EPS)
        return x + m * inv * (g_ref[...] * ga_ref[s:s + 1, :])

    _row_loop(S, L, [m_ref, x_ref], out_ref, fn)


def _row_spec(tm):
    return pl.BlockSpec((None, tm, D_MODEL), lambda b, t, *_: (b, t, 0))


def _vec_spec(index, n=D_MODEL):
    return pl.BlockSpec((None, 1, n), lambda *_: (index, 0, 0))


def _mod_spec(S, layer, which):
    if S == 1:
        first = MOD_ROWS_SAMPLE // SUBLANES
        return pl.BlockSpec((None, None, SUBLANES, D_MODEL), lambda b, t, *_: (layer, which, first + b, 0))
    return pl.BlockSpec((None, None, S, D_MODEL), lambda b, t, *_: (layer, which, 0, 0))


def _mod_kernel(c_ref, w_ref, b_ref, o_ref):
    c = c_ref[...]
    o_ref[...] = _dot(c * jax.nn.sigmoid(c), w_ref[...]) + b_ref[...]


MOD_HEAD_VECS = 2
MOD_BLOCK = 512


def _modulation_head(c_all, w_mod, b_mod):
    rows = c_all.shape[0]
    tn = MOD_HEAD_CHUNK
    per = D_MODEL // tn
    return pl.pallas_call(
        _mod_kernel,
        grid=(MOD_HEAD_VECS * per,),
        in_specs=[pl.BlockSpec((rows, D_MODEL), lambda j: (0, 0)),
                  pl.BlockSpec((None, D_MODEL, tn), lambda j: (0, 0, j)),
                  pl.BlockSpec((None, 1, tn), lambda j: (0, 0, j))],
        out_specs=pl.BlockSpec((None, None, rows, tn), lambda j: (0, j // per, 0, j % per)),
        out_shape=jax.ShapeDtypeStruct((1, 6, rows, D_MODEL), F32),
        compiler_params=_cparams(1),
        name="modulation_head",
    )(c_all, w_mod, b_mod.reshape(DEPTH, 1, 6 * D_MODEL))


def _conv_in_kernel(x_ref, g_ref, sc_ref, sh_ref, wa_ref, wg_ref, ba_ref, bg_ref, u_ref, h_ref, *, S, L):
    @pl.when(pl.program_id(2) == 0)
    def _():
        _norm_mod(x_ref, h_ref, g_ref, sc_ref, sh_ref, S, L)

    h = h_ref[...]
    a = _dot(h, wa_ref[...]) + ba_ref[...]
    g = _dot(h, wg_ref[...]) + bg_ref[...]
    u_ref[...] = a * jax.nn.sigmoid(g)


def _conv_in(x, mod, g_norm, w_in, b_in, *, layer, S, L, tm):
    Bp, T, _ = x.shape
    tn = IN_PROJ_CHUNK
    J = D_MODEL // tn
    b2 = b_in.reshape(-1, 1, 2 * D_MODEL)
    return pl.pallas_call(
        functools.partial(_conv_in_kernel, S=S, L=L),
        grid=(Bp, T // tm, J),
        in_specs=[_row_spec(tm), _vec_spec(4 * layer + 0), _mod_spec(S, layer, 1), _mod_spec(S, layer, 0),
                  pl.BlockSpec((None, D_MODEL, tn), lambda b, t, j: (0, 0, j)),
                  pl.BlockSpec((None, D_MODEL, tn), lambda b, t, j: (0, 0, j + J)),
                  pl.BlockSpec((None, 1, tn), lambda b, t, j: (0, 0, j)),
                  pl.BlockSpec((None, 1, tn), lambda b, t, j: (0, 0, j + J))],
        out_specs=pl.BlockSpec((None, tm, tn), lambda b, t, j: (b, t, j)),
        out_shape=jax.ShapeDtypeStruct((Bp, T, D_MODEL), F32),
        scratch_shapes=[pltpu.VMEM((tm, D_MODEL), F32)],
        compiler_params=_cparams(3),
        name="conv_in",
    )(x, g_norm, mod, mod, w_in, w_in, b2, b2)


def _dwconv_block(win, wb_ref, bias, cols, R):
    n_groups = (CONV_K + 1 + SUBLANES) // SUBLANES
    win_rows = R + CONV_HIST
    acc = jnp.broadcast_to(bias, (R // SUBLANES, SUBLANES, LANES))
    for sh in range(SUBLANES):
        wsh = win if sh == 0 else pltpu.roll(win, win_rows - sh, axis=0)
        for a in range(n_groups):
            k = SUBLANES * a + sh - 2
            if 0 <= k < CONV_K:
                tap = wsh[SUBLANES * a:SUBLANES * a + R, :]
                acc = acc + tap.reshape(R // SUBLANES, SUBLANES, LANES) * wb_ref[k, :, cols]
    return acc.reshape(R, LANES)


def _fill_tap_table(wdw_ref, wb_ref):
    for k in range(CONV_K):
        wb_ref[k] = jnp.broadcast_to(wdw_ref[k:k + 1, :], (SUBLANES, D_MODEL))


def _conv_mix_kernel(x_ref, g_ref, sc_ref, sh_ref, wa_ref, wg_ref, ba_ref, bg_ref, wdw_ref, bdw_ref,
                     c_ref, wm_ref, bm_ref,
                     y_ref, tail_ref, mod_ref, h_ref, wb_ref, carry_ref, *, L, R, nb, n_mod):
    t = pl.program_id(1)
    j = pl.program_id(2)
    tn = wa_ref.shape[1]
    rb = L // nb
    step = (pl.program_id(0) * pl.num_programs(1) + t) * pl.num_programs(2) + j

    @pl.when(step < n_mod)
    def _():
        c = c_ref[...]
        mod_ref[...] = _dot(c * jax.nn.sigmoid(c), wm_ref[...]) + bm_ref[...]

    @pl.when((pl.program_id(0) == 0) & (t == 0) & (j == 0))
    def _():
        _fill_tap_table(wdw_ref, wb_ref)
        carry_ref[...] = jnp.zeros(carry_ref.shape, F32)

    @pl.when(j == 0)
    def _():
        _norm_mod(x_ref, h_ref, g_ref, sc_ref, sh_ref, 1, L)

    wa = wa_ref[...].astype(BF16)
    wg = wg_ref[...].astype(BF16)
    tail = jnp.where(t == 0, 0.0, carry_ref[j])
    for blk in range(nb):
        hb = h_ref[blk * rb:(blk + 1) * rb, :]
        a = _dot(hb, wa) + ba_ref[...]
        g = _dot(hb, wg) + bg_ref[...]
        u = a * jax.nn.sigmoid(g)
        ufull = jnp.concatenate([tail, u], axis=0)
        for cb in range(tn // LANES):
            gcols = pl.ds(pl.multiple_of(j * tn + cb * LANES, LANES), LANES)
            lcols = slice(cb * LANES, (cb + 1) * LANES)
            bias = bdw_ref[:, gcols]
            for r in range(rb // R):
                win = ufull[r * R:r * R + R + CONV_HIST, lcols]
                y_ref[blk * rb + r * R:blk * rb + (r + 1) * R, lcols] = _dwconv_block(win, wb_ref, bias, gcols, R)
        tail = u[rb - CONV_HIST:, :]
    carry_ref[j] = tail
    tail_ref[...] = tail


def _conv_mix(x, mod_head, g_norm, w_in, b_in, w_dw, b_dw, c_all, w_mod, b_mod, *, layer, tm, R, nb):
    Bp, T, _ = x.shape
    tn = MIX_CHUNK
    J = D_MODEL // tn
    nT = T // tm
    b2 = b_in.reshape(-1, 1, 2 * D_MODEL)
    rows = c_all.shape[0]
    per_vec = D_MODEL // MOD_BLOCK
    per_layer = 6 * per_vec
    skip = MOD_HEAD_VECS * per_vec
    n_mod = DEPTH * per_layer - skip
    assert n_mod <= Bp * nT * J

    def mod_block(b, t, j):
        sb = jnp.minimum((b * nT + t) * J + j, n_mod - 1) + skip
        return sb // per_layer, sb % per_layer

    def wm_map(b, t, j):
        layer_, cb = mod_block(b, t, j)
        return layer_, 0, cb

    def mod_out_map(b, t, j):
        layer_, cb = mod_block(b, t, j)
        return layer_, cb // per_vec, 0, cb % per_vec

    return pl.pallas_call(
        functools.partial(_conv_mix_kernel, L=tm, R=R, nb=nb, n_mod=n_mod),
        grid=(Bp, nT, J),
        in_specs=[_row_spec(tm), _vec_spec(4 * layer + 0), _mod_spec(1, layer, 1), _mod_spec(1, layer, 0),
                  pl.BlockSpec((None, D_MODEL, tn), lambda b, t, j: (0, 0, j)),
                  pl.BlockSpec((None, D_MODEL, tn), lambda b, t, j: (0, 0, j + J)),
                  pl.BlockSpec((None, 1, tn), lambda b, t, j: (0, 0, j)),
                  pl.BlockSpec((None, 1, tn), lambda b, t, j: (0, 0, j + J)),
                  pl.BlockSpec((None, CONV_K, D_MODEL), lambda b, t, j: (0, 0, 0)),
                  _vec_spec(0),
                  pl.BlockSpec((rows, D_MODEL), lambda b, t, j: (0, 0)),
                  pl.BlockSpec((None, D_MODEL, MOD_BLOCK), wm_map),
                  pl.BlockSpec((None, 1, MOD_BLOCK), wm_map)],
        out_specs=[pl.BlockSpec((None, tm, tn), lambda b, t, j: (b, t, j)),
                   pl.BlockSpec((None, None, CONV_HIST, tn), lambda b, t, j: (b, t, 0, j)),
                   pl.BlockSpec((None, None, rows, MOD_BLOCK), mod_out_map)],
        out_shape=[jax.ShapeDtypeStruct((Bp, T, D_MODEL), F32),
                   jax.ShapeDtypeStruct((Bp, nT, CONV_HIST, D_MODEL), F32),
                   jax.ShapeDtypeStruct((DEPTH, 6, rows, D_MODEL), F32)],
        scratch_shapes=[pltpu.VMEM((tm, D_MODEL), BF16),
                        pltpu.VMEM((CONV_K, SUBLANES, D_MODEL), F32),
                        pltpu.VMEM((J, CONV_HIST, tn), F32)],
        compiler_params=_cparams(3),
        name="conv_mix",
    )(x, g_norm, mod_head, mod_head, w_in, w_in, b2, b2, w_dw, b_dw.reshape(-1, 1, D_MODEL),
      c_all, w_mod, b_mod.reshape(DEPTH, 1, 6 * D_MODEL))


def _conv_proj_kernel(y_ref, lng_ref, lnb_ref, wout_ref, bout_ref, x_ref, g_ref, ga_ref, x1_ref, *, L):
    rb = L // PROJ_BLOCKS
    w = wout_ref[...]
    for blk in range(PROJ_BLOCKS):
        rows = slice(blk * rb, (blk + 1) * rb)
        a = _ln_silu_value(y_ref[rows, :], lng_ref, lnb_ref)
        m = _dot(a, w) + bout_ref[...]
        x1_ref[rows, :] = _residual_value(m, x_ref[rows, :], g_ref, ga_ref[0:1, :])


def _conv_proj(y, x, mod, g_norm, ln_g, ln_b, w_out, b_out, *, layer, tm):
    Bp, T, _ = x.shape
    vec = lambda a: a.reshape(-1, 1, D_MODEL)
    return pl.pallas_call(
        functools.partial(_conv_proj_kernel, L=tm),
        grid=(Bp, T // tm),
        in_specs=[_row_spec(tm), _vec_spec(0), _vec_spec(0),
                  pl.BlockSpec((None, D_MODEL, D_MODEL), lambda b, t: (0, 0, 0), pipeline_mode=pl.Buffered(1)),
                  _vec_spec(0), _row_spec(tm), _vec_spec(4 * layer + 1), _mod_spec(1, layer, 2)],
        out_specs=_row_spec(tm),
        out_shape=jax.ShapeDtypeStruct((Bp, T, D_MODEL), F32),
        compiler_params=_cparams(2),
        name="conv_proj",
    )(y, vec(ln_g), vec(ln_b), w_out, vec(b_out), x, g_norm, mod)


def _dwconv(full_ref, wb_ref, bdw_ref, y_ref, S, L, R):
    def body(cb, c):
        cols = pl.ds(pl.multiple_of(cb * LANES, LANES), LANES)
        bias = bdw_ref[:, cols]
        for s in range(S):
            for r in range(L // R):
                win = full_ref[s, r * R:r * R + R + CONV_HIST, cols]
                y_ref[s * L + r * R:s * L + (r + 1) * R, cols] = _dwconv_block(win, wb_ref, bias, cols, R)
        return c

    lax.fori_loop(0, D_MODEL // LANES, body, 0)


def _ln_silu_value(y, lng_ref, lnb_ref):
    mu = jnp.mean(y, axis=-1, keepdims=True)
    yc = y - mu
    var = jnp.mean(yc * yc, axis=-1, keepdims=True)
    z = yc * lax.rsqrt(var + EPS) * lng_ref[...] + lnb_ref[...]
    return z * jax.nn.sigmoid(z)


def _ln_silu(y_ref, lng_ref, lnb_ref, S, L):
    _row_loop(S, L, [y_ref], y_ref, lambda s, y: _ln_silu_value(y, lng_ref, lnb_ref))


def _conv_out_kernel(u_ref, hist_ref, wdw_ref, bdw_ref, lng_ref, lnb_ref, wout_ref, bout_ref, x_ref, g_ref, ga_ref,
                     x1_ref, full_ref, wb_ref, y_ref, *, S, L, R):
    _fill_tap_table(wdw_ref, wb_ref)
    pad = CONV_HIST - (CONV_K - 1)
    full_ref[:, 0:pad, :] = jnp.zeros((S, pad, D_MODEL), F32)
    full_ref[:, pad:CONV_HIST, :] = hist_ref[...]
    full_ref[:, CONV_HIST:CONV_HIST + L, :] = u_ref[...].reshape(S, L, D_MODEL)
    _dwconv(full_ref, wb_ref, bdw_ref, y_ref, S, L, R)
    _ln_silu(y_ref, lng_ref, lnb_ref, S, L)
    x1_ref[...] = _dot(y_ref[...], wout_ref[...]) + bout_ref[...]
    _residual(x1_ref, x_ref, g_ref, ga_ref, x1_ref, S, L)


def _conv_out(u, hist, x, mod, g_norm, w_dw, b_dw, ln_g, ln_b, w_out, b_out, *, layer, S, L, tm, R):
    Bp, T, _ = x.shape
    assert Bp == 1 and T == tm
    vec = lambda a: a.reshape(-1, 1, D_MODEL)
    return pl.pallas_call(
        functools.partial(_conv_out_kernel, S=S, L=L, R=R),
        grid=(Bp, T // tm),
        in_specs=[_row_spec(tm), pl.BlockSpec((S, CONV_K - 1, D_MODEL), lambda b, t: (0, 0, 0)),
                  pl.BlockSpec((None, CONV_K, D_MODEL), lambda b, t: (0, 0, 0)),
                  _vec_spec(0), _vec_spec(0), _vec_spec(0),
                  pl.BlockSpec((None, D_MODEL, D_MODEL), lambda b, t: (0, 0, 0), pipeline_mode=pl.Buffered(1)),
                  _vec_spec(0), _row_spec(tm), _vec_spec(4 * layer + 1), _mod_spec(S, layer, 2)],
        out_specs=_row_spec(tm),
        out_shape=jax.ShapeDtypeStruct((Bp, T, D_MODEL), F32),
        scratch_shapes=[pltpu.VMEM((S, CONV_HIST + L, D_MODEL), F32),
                        pltpu.VMEM((CONV_K, SUBLANES, D_MODEL), F32),
                        pltpu.VMEM((tm, D_MODEL), F32)],
        compiler_params=_cparams(2),
        name="conv_out",
    )(u, hist, w_dw, vec(b_dw), vec(ln_g), vec(ln_b), w_out, vec(b_out), x, g_norm, mod)


def _ffn_kernel(*refs, S, L, nb, prompt):
    if prompt:
        (x_ref, g2_ref, sc_ref, sh_ref, wg_ref, wv_ref, wdw_ref, bdw_ref, wd_ref, g3_ref, ga_ref,
         x2_ref, st_ref, h_ref, carry_ref) = refs
    else:
        (x_ref, g2_ref, sc_ref, sh_ref, wg_ref, wv_ref, wdw_ref, bdw_ref, wd_ref, g3_ref, ga_ref, hist_ref,
         x2_ref, st_ref, h_ref) = refs
    t = pl.program_id(1)
    j = pl.program_id(2)
    last_j = pl.num_programs(2) - 1
    tm = x_ref.shape[0]
    rb = tm // nb
    seg = min(L, rb)
    pad = SUBLANES
    inline = S == 1

    mm_dtype = h_ref.dtype

    def step(first, last):
        wg = wg_ref[...].astype(mm_dtype)
        wv = wv_ref[...].astype(mm_dtype)
        wd = wd_ref[...].astype(mm_dtype)
        w0 = wdw_ref[0:1, :]
        w1 = wdw_ref[1:2, :]
        w2 = wdw_ref[2:3, :]
        bd = bdw_ref[...]
        tail = None
        for blk in range(nb):
            rows = slice(blk * rb, (blk + 1) * rb)
            if first:
                x = x_ref[rows, :]
                inv = lax.rsqrt(jnp.mean(x * x, axis=-1, keepdims=True) + EPS)
                hb = (x * inv * (g2_ref[...] * (1.0 + sc_ref[0:1, :])) + sh_ref[0:1, :]).astype(mm_dtype)
                h_ref[rows, :] = hb
            else:
                hb = h_ref[rows, :]
            g = _dot(hb, wg)
            v = _dot(hb, wv)
            p1s, p2s = [], []
            for q in range(rb // seg):
                gq = g[q * seg:(q + 1) * seg, :]
                if not prompt:
                    s = blk * (rb // seg) + q
                    hist = hist_ref[s]
                    st_ref[s] = gq[seg - (FFN_K - 1):, :]
                elif blk == 0:
                    hist = jnp.where(t == 0, 0.0, carry_ref[j])
                else:
                    hist = tail
                gfull = jnp.concatenate([hist, gq], axis=0)
                p1s.append(pltpu.roll(gfull, 1, axis=0)[pad:, :])
                p2s.append(pltpu.roll(gfull, 2, axis=0)[pad:, :])
                tail = gq[seg - pad:, :]
            p1 = p1s[0] if len(p1s) == 1 else jnp.concatenate(p1s, axis=0)
            p2 = p2s[0] if len(p2s) == 1 else jnp.concatenate(p2s, axis=0)
            gc = g * w2 + p1 * w1 + p2 * w0 + bd
            act = (jax.nn.gelu(gc) * v).astype(mm_dtype)
            acc = _dot(act, wd)
            if not first:
                acc = x2_ref[rows, :] + acc
            if last:
                acc = _residual_value(acc, x_ref[rows, :], g3_ref, ga_ref[0:1, :])
            x2_ref[rows, :] = acc
        if prompt:
            carry_ref[j] = tail
            st_ref[0] = tail[pad - (FFN_K - 1):, :]

    if inline:
        pl.when(j == 0)(lambda: step(True, False))
        pl.when(j > 0)(lambda: step(False, False))

        @pl.when(j == last_j)
        def _():
            _residual(x2_ref, x_ref, g3_ref, ga_ref, x2_ref, S, L)
    else:
        @pl.when(j == 0)
        def _():
            _norm_mod(x_ref, h_ref, g2_ref, sc_ref, sh_ref, S, L)
            x2_ref[...] = jnp.zeros(x2_ref.shape, F32)

        step(False, False)

        @pl.when(j == last_j)
        def _():
            _residual(x2_ref, x_ref, g3_ref, ga_ref, x2_ref, S, L)


def _ffn(x, hist, mod, g_norm, w_up, w_dw, b_dw, w_down, *, layer, S, L, tm, tf, nb):
    Bp, T, _ = x.shape
    J = D_FF // tf
    nT = T // tm
    prompt = hist is None
    x_spec = pl.BlockSpec((None, tm, D_MODEL), lambda b, t, j: (b, t, 0), pipeline_mode=pl.Buffered(1))
    in_specs = [x_spec, _vec_spec(4 * layer + 2), _mod_spec(S, layer, 4), _mod_spec(S, layer, 3),
                pl.BlockSpec((None, D_MODEL, tf), lambda b, t, j: (layer, 0, j)),
                pl.BlockSpec((None, D_MODEL, tf), lambda b, t, j: (layer, 0, j + J)),
                pl.BlockSpec((None, FFN_K, tf), lambda b, t, j: (layer, 0, j)),
                pl.BlockSpec((None, 1, tf), lambda b, t, j: (layer, 0, j)),
                pl.BlockSpec((None, tf, D_MODEL), lambda b, t, j: (layer, j, 0)),
                _vec_spec(4 * layer + 3), _mod_spec(S, layer, 5)]
    args = [x, g_norm, mod, mod, w_up, w_up, w_dw, b_dw.reshape(DEPTH, 1, D_FF), w_down, g_norm, mod]
    scratch = [pltpu.VMEM((tm, D_MODEL), BF16 if prompt else F32)]
    if prompt:
        scratch.append(pltpu.VMEM((J, SUBLANES, tf), F32))
    else:
        in_specs.append(pl.BlockSpec((S, SUBLANES, tf), lambda b, t, j: (0, 0, j)))
        args.append(hist)
    return pl.pallas_call(
        functools.partial(_ffn_kernel, S=S, L=L, nb=nb, prompt=prompt),
        grid=(Bp, nT, J),
        in_specs=in_specs,
        out_specs=[_row_spec(tm),
                   pl.BlockSpec((None, None, S, FFN_K - 1, tf), lambda b, t, j: (b, t, 0, 0, j))],
        out_shape=[jax.ShapeDtypeStruct((Bp, T, D_MODEL), F32),
                   jax.ShapeDtypeStruct((Bp, nT, S, FFN_K - 1, D_FF), F32)],
        scratch_shapes=scratch,
        compiler_params=pltpu.CompilerParams(dimension_semantics=("arbitrary",) * 3,
                                             vmem_limit_bytes=VMEM_LIMIT_FFN),
        name="ffn",
    )(*args)


def _qkv_kernel(x_ref, g_ref, sc_ref, sh_ref, wq_ref, wk_ref, wv_ref, q_ref, k_ref, v_ref, h_ref, *, S, L):
    tm = x_ref.shape[0]
    if S > 1:
        _norm_mod(x_ref, h_ref, g_ref, sc_ref, sh_ref, S, L)
        blocks = [(slice(0, tm), h_ref[...])]
    else:
        rb = tm // PROJ_BLOCKS
        mul = g_ref[...] * (1.0 + sc_ref[0:1, :])
        blocks = []
        for blk in range(PROJ_BLOCKS):
            rows = slice(blk * rb, (blk + 1) * rb)
            x = x_ref[rows, :]
            inv = lax.rsqrt(jnp.mean(x * x, axis=-1, keepdims=True) + EPS)
            blocks.append((rows, x * inv * mul + sh_ref[0:1, :]))
    wq = wq_ref[...]
    wk = wk_ref[...]
    wv = wv_ref[...]
    for rows, h in blocks:
        q_ref[rows, :] = _dot(h, wq)
        k_ref[rows, :] = _dot(h, wk)
        v_ref[rows, :] = _dot(h, wv)


def _qkv(x, mod, g_norm, w_q, w_k, w_v, *, layer, S, L, tm):
    Bp, T, _ = x.shape
    kv_spec = pl.BlockSpec((None, tm, KV_DIM), lambda b, t: (b, t, 0))
    resident = lambda n: pl.BlockSpec((None, D_MODEL, n), lambda b, t: (0, 0, 0), pipeline_mode=pl.Buffered(1))
    return pl.pallas_call(
        functools.partial(_qkv_kernel, S=S, L=L),
        grid=(Bp, T // tm),
        in_specs=[_row_spec(tm), _vec_spec(4 * layer + 0), _mod_spec(S, layer, 1), _mod_spec(S, layer, 0),
                  resident(D_MODEL), resident(KV_DIM), resident(KV_DIM)],
        out_specs=[_row_spec(tm), kv_spec, kv_spec],
        out_shape=[jax.ShapeDtypeStruct((Bp, T, D_MODEL), F32),
                   jax.ShapeDtypeStruct((Bp, T, KV_DIM), F32),
                   jax.ShapeDtypeStruct((Bp, T, KV_DIM), F32)],
        scratch_shapes=[pltpu.VMEM((tm, D_MODEL), F32)],
        compiler_params=_cparams(2),
        name="qkv",
    )(x, g_norm, mod, mod, w_q, w_k, w_v)


REL_SPAN = 3 * LANES


def _bias_kernel(rb_ref, prof_ref, valid_ref, o_ref, *, Qn):
    prof = prof_ref[...]
    valid = valid_ref[...] != 0
    early = lax.broadcasted_iota(jnp.int32, valid.shape, 0) < WINDOW
    for hh in range(N_HEADS):
        f = jnp.zeros(prof.shape, F32)
        for b in range(N_BUCKETS):
            f = jnp.where(prof == b, rb_ref[b, hh], f)
        h, g = divmod(hh, GROUP)
        off = (g * Qn) % LANES
        x = jnp.broadcast_to(f[0:1, :], (KEY_WIN, REL_SPAN))
        r = pltpu.roll(x, WINDOW + off, axis=1, stride=1, stride_axis=0)
        tab = jnp.where(valid, r[:, off:off + Qn], NEG_INF)
        o_ref[0, h, :, g * Qn:(g + 1) * Qn] = tab
        if o_ref.shape[0] > 1:
            o_ref[1, h, :, g * Qn:(g + 1) * Qn] = jnp.where(early, NEG_INF, tab)


def _t5_bucket(rel):
    half = N_BUCKETS // 2
    max_exact = half // 2
    n = jnp.abs(rel)
    ret = jnp.where(rel > 0, half, 0)
    nf = jnp.maximum(n, 1).astype(F32)
    large = max_exact + (jnp.log(nf / max_exact) / math.log(MAX_DISTANCE / max_exact)
                         * (half - max_exact)).astype(jnp.int32)
    large = jnp.minimum(large, half - 1)
    return ret + jnp.where(n < max_exact, n, large)


def _bias_table(rel_bias, Qn, lo, hi, variants):
    assert KEY_WIN + Qn - 1 <= REL_SPAN
    k = jnp.arange(KEY_WIN)[:, None]
    q = jnp.arange(Qn)[None, :]
    valid = ((k >= lo(q)) & (k < hi(q))).astype(jnp.int32)
    prof = jnp.broadcast_to(_t5_bucket(WINDOW - jnp.arange(REL_SPAN)).astype(jnp.int32)[None, :], (SUBLANES, REL_SPAN))
    return pl.pallas_call(
        functools.partial(_bias_kernel, Qn=Qn),
        in_specs=[pl.BlockSpec(memory_space=pltpu.SMEM),
                  pl.BlockSpec((SUBLANES, REL_SPAN), lambda: (0, 0)),
                  pl.BlockSpec((KEY_WIN, Qn), lambda: (0, 0))],
        out_specs=pl.BlockSpec((variants, N_KV_HEADS, KEY_WIN, GROUP * Qn), lambda: (0, 0, 0, 0)),
        out_shape=jax.ShapeDtypeStruct((variants, N_KV_HEADS, KEY_WIN, GROUP * Qn), F32),
        name="rel_bias_table",
    )(rel_bias, prof, valid)


def _attend(q_ref, r0, Qn, kw, vw, bias_ref, sel, sink_ref, o_ref, qs_ref):
    rows = pl.ds(r0, Qn)
    for g in range(GROUP):
        for h in range(N_KV_HEADS):
            src = (h * GROUP + g) * HEAD_DIM
            qs_ref[g * Qn:(g + 1) * Qn, h * HEAD_DIM:(h + 1) * HEAD_DIM] = q_ref[rows, src:src + HEAD_DIM] * ATTN_SCALE
    qt = qs_ref[...].T
    vt = vw.T
    outs = []
    for h in range(N_KV_HEADS):
        hd = slice(h * HEAD_DIM, (h + 1) * HEAD_DIM)
        s = _dot(kw[:, hd], qt[hd, :]) + bias_ref[sel, h]
        sk = sink_ref[h]
        mx = jnp.maximum(jnp.max(s, axis=0, keepdims=True), sk)
        p = jnp.exp(s - mx)
        den = jnp.sum(p, axis=0, keepdims=True) + jnp.exp(sk - mx)
        outs.append(_dot(vt[hd, :], p) * (1.0 / den))
    o2 = jnp.concatenate(outs, axis=0).T
    for g in range(GROUP):
        for h in range(N_KV_HEADS):
            dst = (h * GROUP + g) * HEAD_DIM
            o_ref[rows, dst:dst + HEAD_DIM] = o2[g * Qn:(g + 1) * Qn, h * HEAD_DIM:(h + 1) * HEAD_DIM]


def _attn_prompt_kernel(q_ref, kp_ref, kc_ref, vp_ref, vc_ref, bias_ref, sink_ref, o_ref, kw_ref, vw_ref, qs_ref,
                        *, tq, Qn):
    t = pl.program_id(1)
    kw_ref[0:WINDOW, :] = kp_ref[...]
    kw_ref[WINDOW:WINDOW + tq, :] = kc_ref[...]
    vw_ref[0:WINDOW, :] = vp_ref[...]
    vw_ref[WINDOW:WINDOW + tq, :] = vc_ref[...]

    def body(mp, carry):
        for u in range(ATTN_UNITS_PER_ITER):
            m = mp * ATTN_UNITS_PER_ITER + u
            r0 = pl.multiple_of(m * Qn, Qn)
            kw = kw_ref[pl.ds(r0, KEY_WIN), :]
            vw = vw_ref[pl.ds(r0, KEY_WIN), :]
            sel = ((t == 0) & (m == 0)).astype(jnp.int32)
            _attend(q_ref, r0, Qn, kw, vw, bias_ref, sel, sink_ref, o_ref, qs_ref.at[u])
        return carry

    lax.fori_loop(0, tq // (Qn * ATTN_UNITS_PER_ITER), body, 0)


def _attn_sample_kernel(q_ref, ck_ref, kn_ref, cv_ref, vn_ref, bias_ref, sink_ref, o_ref, kw_ref, vw_ref, qs_ref,
                        *, S, L):
    P = ck_ref.shape[1]
    kw_ref[:, 0:P, :] = ck_ref[...]
    kw_ref[:, P:P + L, :] = kn_ref[...].reshape(S, L, KV_DIM)
    kw_ref[:, P + L:, :] = jnp.zeros((S, KEY_WIN - P - L, KV_DIM), F32)
    vw_ref[:, 0:P, :] = cv_ref[...]
    vw_ref[:, P:P + L, :] = vn_ref[...].reshape(S, L, KV_DIM)
    vw_ref[:, P + L:, :] = jnp.zeros((S, KEY_WIN - P - L, KV_DIM), F32)

    def body(sp, carry):
        for u in range(ATTN_UNITS_PER_ITER):
            s = sp * ATTN_UNITS_PER_ITER + u
            r0 = pl.multiple_of(s * L, L)
            _attend(q_ref, r0, L, kw_ref[s], vw_ref[s], bias_ref, 0, sink_ref, o_ref, qs_ref.at[u])
        return carry

    lax.fori_loop(0, S // ATTN_UNITS_PER_ITER, body, 0)


def _attn_prompt(q, k, v, bias, sink, *, tq):
    Bp, T, _ = q.shape
    Qn = 2 * CHUNK
    per = tq // WINDOW
    prev_spec = pl.BlockSpec((None, WINDOW, KV_DIM), lambda b, t: (b, jnp.maximum(t * per - 1, 0), 0))
    cur_spec = pl.BlockSpec((None, tq, KV_DIM), lambda b, t: (b, t, 0))
    return pl.pallas_call(
        functools.partial(_attn_prompt_kernel, tq=tq, Qn=Qn),
        grid=(Bp, T // tq),
        in_specs=[_row_spec(tq), prev_spec, cur_spec, prev_spec, cur_spec,
                  pl.BlockSpec((2, N_KV_HEADS, KEY_WIN, GROUP * Qn), lambda b, t: (0, 0, 0, 0),
                               pipeline_mode=pl.Buffered(1)),
                  pl.BlockSpec((N_KV_HEADS, 1, GROUP * Qn), lambda b, t: (0, 0, 0))],
        out_specs=_row_spec(tq),
        out_shape=jax.ShapeDtypeStruct((Bp, T, D_MODEL), F32),
        scratch_shapes=[pltpu.VMEM((WINDOW + tq, KV_DIM), F32), pltpu.VMEM((WINDOW + tq, KV_DIM), F32),
                        pltpu.VMEM((ATTN_UNITS_PER_ITER, GROUP * Qn, KV_DIM), F32)],
        compiler_params=_cparams(2),
        name="attn_prompt",
    )(q, k, k, v, v, bias, sink)


def _attn_sample(q, k, v, cache_k, cache_v, bias, sink, *, S, L):
    return pl.pallas_call(
        functools.partial(_attn_sample_kernel, S=S, L=L),
        out_shape=jax.ShapeDtypeStruct((S * L, D_MODEL), F32),
        scratch_shapes=[pltpu.VMEM((S, KEY_WIN, KV_DIM), F32), pltpu.VMEM((S, KEY_WIN, KV_DIM), F32),
                        pltpu.VMEM((ATTN_UNITS_PER_ITER, GROUP * L, KV_DIM), F32)],
        compiler_params=pltpu.CompilerParams(vmem_limit_bytes=VMEM_LIMIT),
        name="attn_sample",
    )(q, cache_k, k, cache_v, v, bias, sink)


def _residual_value(m, x, g_ref, ga):
    inv = lax.rsqrt(jnp.mean(m * m, axis=-1, keepdims=True) + EPS)
    return x + m * inv * (g_ref[...] * ga)


def _attn_out_kernel(o_ref, w_ref, x_ref, g_ref, ga_ref, x1_ref, *, S, L):
    tm = o_ref.shape[0]
    if S > 1:
        x1_ref[...] = _dot(o_ref[...], w_ref[...])
        _residual(x1_ref, x_ref, g_ref, ga_ref, x1_ref, S, L)
        return
    rb = tm // PROJ_BLOCKS
    w = w_ref[...]
    for blk in range(PROJ_BLOCKS):
        rows = slice(blk * rb, (blk + 1) * rb)
        x1_ref[rows, :] = _residual_value(_dot(o_ref[rows, :], w), x_ref[rows, :], g_ref, ga_ref[0:1, :])


def _attn_out(o, x, mod, g_norm, w_o, *, layer, S, L, tm):
    Bp, T, _ = x.shape
    return pl.pallas_call(
        functools.partial(_attn_out_kernel, S=S, L=L),
        grid=(Bp, T // tm),
        in_specs=[_row_spec(tm),
                  pl.BlockSpec((None, D_MODEL, D_MODEL), lambda b, t: (0, 0, 0), pipeline_mode=pl.Buffered(1)),
                  _row_spec(tm), _vec_spec(4 * layer + 1), _mod_spec(S, layer, 2)],
        out_specs=_row_spec(tm),
        out_shape=jax.ShapeDtypeStruct((Bp, T, D_MODEL), F32),
        compiler_params=_cparams(2),
        name="attn_out",
    )(o, w_o, x, g_norm, mod)


def kernel(x_prompt, x_sample, c_prompt, c_sample, cache_conv, cache_k, cache_v, cache_ffn, w_mod, b_mod, g_norm, conv_w_in, conv_b_in, conv_w_dw, conv_b_dw, conv_ln_g, conv_ln_b, conv_w_out, conv_b_out, attn_w_q, attn_w_k, attn_w_v, attn_w_o, attn_sinks, rel_bias, ffn_w_up, ffn_w_dw, ffn_b_dw, ffn_w_down):
    B, T, D = x_prompt.shape
    SB, SL, _ = x_sample.shape
    assert SB == MOD_ROWS_SAMPLE
    c_all = jnp.concatenate(
        [c_sample, jnp.pad(c_prompt[:, None, :], ((0, 0), (0, SUBLANES - 1), (0, 0))).reshape(B * SUBLANES, D)], axis=0)
    mod_head = _modulation_head(c_all, w_mod, b_mod)
    gn = g_norm.reshape(DEPTH * 4, 1, D)

    geo_p = dict(S=1, L=STREAM_ROWS, tm=STREAM_ROWS)
    geo_p_small = dict(S=1, L=RESIDENT_ROWS, tm=RESIDENT_ROWS)
    geo_s = dict(S=SB, L=SL, tm=SB * SL)

    xp = x_prompt
    xs = x_sample.reshape(1, SB * SL, D)

    yp, tail_p, mod = _conv_mix(xp, mod_head, gn, conv_w_in, conv_b_in, conv_w_dw, conv_b_dw, c_all, w_mod, b_mod,
                                layer=0, tm=STREAM_ROWS, R=MIX_CONV_ROWS, nb=MIX_ROW_BLOCKS)
    xp = _conv_proj(yp, xp, mod, gn, conv_ln_g, conv_ln_b, conv_w_out, conv_b_out, layer=0, tm=RESIDENT_ROWS)
    us = _conv_in(xs, mod_head, gn, conv_w_in, conv_b_in, layer=0, **geo_s)
    conv_args = (conv_w_dw, conv_b_dw, conv_ln_g, conv_ln_b, conv_w_out, conv_b_out)
    xs = _conv_out(us, cache_conv[0], xs, mod, gn, *conv_args, layer=0, R=SL, **geo_s)
    conv_state_p = tail_p[:, -1, CONV_HIST - (CONV_K - 1):, :][None]
    conv_state_s = us.reshape(SB, SL, D)[:, SL - (CONV_K - 1):, :][None]

    ffn_p, ffn_s = [], []
    ffn_hist = jnp.pad(cache_ffn, ((0, 0), (0, 0), (SUBLANES - (FFN_K - 1), 0), (0, 0)))

    def run_ffn(i, xp, xs):
        w = (ffn_w_up, ffn_w_dw, ffn_b_dw, ffn_w_down)
        xp, st_p = _ffn(xp, None, mod, gn, *w, layer=i, tf=FFN_CHUNK, nb=FFN_ROW_BLOCKS, **geo_p)
        xs, st_s = _ffn(xs, ffn_hist[i], mod, gn, *w, layer=i, tf=FFN_CHUNK, nb=FFN_ROW_BLOCKS, **geo_s)
        ffn_p.append(st_p[:, -1, 0])
        ffn_s.append(st_s[0, 0])
        return xp, xs

    xp, xs = run_ffn(0, xp, xs)

    qp, kp, vp = _qkv(xp, mod, gn, attn_w_q, attn_w_k, attn_w_v, layer=1, **geo_p_small)
    qs, ks, vs = _qkv(xs, mod, gn, attn_w_q, attn_w_k, attn_w_v, layer=1, **geo_s)
    P = cache_k.shape[2]
    assert P == WINDOW and P + SL <= KEY_WIN
    bias_p = _bias_table(rel_bias, 2 * CHUNK,
                         lambda q: (q // CHUNK) * CHUNK, lambda q: (q // CHUNK) * CHUNK + WINDOW + CHUNK, 2)
    bias_s = _bias_table(rel_bias, SL, lambda q: 0 * q, lambda q: 0 * q + P + SL, 1)
    sink_p = jnp.repeat(attn_sinks[0], 2 * CHUNK).reshape(N_KV_HEADS, 1, GROUP * 2 * CHUNK)
    sink_s = jnp.repeat(attn_sinks[0], SL).reshape(N_KV_HEADS, 1, GROUP * SL)
    op = _attn_prompt(qp, kp, vp, bias_p, sink_p, tq=RESIDENT_ROWS)
    ck = cache_k[0].reshape(SB, P, KV_DIM)
    cv = cache_v[0].reshape(SB, P, KV_DIM)
    os_ = _attn_sample(qs[0], ks[0], vs[0], ck, cv, bias_s, sink_s, S=SB, L=SL)[None]
    xp = _attn_out(op, xp, mod, gn, attn_w_o, layer=1, **geo_p_small)
    xs = _attn_out(os_, xs, mod, gn, attn_w_o, layer=1, **geo_s)
    keep = min(WINDOW, T)
    k_state_p = kp[:, T - keep:].reshape(B, keep, N_KV_HEADS, HEAD_DIM)[None]
    v_state_p = vp[:, T - keep:].reshape(B, keep, N_KV_HEADS, HEAD_DIM)[None]
    k_state_s = jnp.concatenate([ck, ks.reshape(SB, SL, KV_DIM)], axis=1)[:, SL:]
    v_state_s = jnp.concatenate([cv, vs.reshape(SB, SL, KV_DIM)], axis=1)[:, SL:]
    k_state_s = k_state_s.reshape(SB, P, N_KV_HEADS, HEAD_DIM)[None]
    v_state_s = v_state_s.reshape(SB, P, N_KV_HEADS, HEAD_DIM)[None]

    xp, xs = run_ffn(1, xp, xs)

    return (xp, xs.reshape(SB, SL, D), conv_state_p, conv_state_s,
            k_state_p, v_state_p, k_state_s, v_state_s,
            jnp.stack(ffn_p), jnp.stack(ffn_s))
```

```python
import functools
import math

import jax
import jax.numpy as jnp
from jax import lax
from jax.experimental import pallas as pl
from jax.experimental.pallas import tpu as pltpu

F32 = jnp.float32
BF16 = jnp.bfloat16

D_MODEL = 2048
D_FF = 5632
DEPTH = 2
CONV_K = 31
FFN_K = 3
N_HEADS = 32
N_KV_HEADS = 4
GROUP = N_HEADS // N_KV_HEADS
HEAD_DIM = 64
KV_DIM = N_KV_HEADS * HEAD_DIM
CHUNK = 64
WINDOW = 128
N_BUCKETS = 32
MAX_DISTANCE = 128
EPS = 1e-6
NEG_INF = -1e30
ATTN_SCALE = HEAD_DIM ** -0.5

SUBLANES = 8
LANES = 128
CONV_HIST = 32
KEY_WIN = 2 * WINDOW
VMEM_LIMIT = 58 * 1024 * 1024
VMEM_LIMIT_FFN = 62 * 1024 * 1024
MOD_ROWS_SAMPLE = 16


def _cparams(n_grid):
    return pltpu.CompilerParams(dimension_semantics=("arbitrary",) * n_grid,
                                vmem_limit_bytes=VMEM_LIMIT)


def _dot(a, b):
    return jnp.dot(a, b, preferred_element_type=F32)


ROW_SLAB = 64
ROW_GROUP = 2
PROJ_BLOCKS = 2

STREAM_ROWS = 1024
RESIDENT_ROWS = 512
FFN_CHUNK = 512
FFN_ROW_BLOCKS = 2
MIX_CHUNK = 256
MIX_ROW_BLOCKS = 2
MIX_CONV_ROWS = 64
IN_PROJ_CHUNK = 512
MOD_HEAD_CHUNK = 1024
ATTN_UNITS_PER_ITER = 4


def _row_loop(S, L, srcs, dst, fn):
    slab = min(L, ROW_SLAB)
    n = L // slab
    group = ROW_GROUP if n % ROW_GROUP == 0 else 1
    for s in range(S):

        def body(r, c, s=s):
            rows = [pl.ds(pl.multiple_of(s * L + (r * group + i) * slab, slab), slab) for i in range(group)]
            vals = [[src[rw, :] for src in srcs] for rw in rows]
            outs = [fn(s, *v) for v in vals]
            for rw, out in zip(rows, outs):
                dst[rw, :] = out.astype(dst.dtype)
            return c

        lax.fori_loop(0, n // group, body, 0)


def _norm_mod(x_ref, dst_ref, g_ref, sc_ref, sh_ref, S, L):
    def fn(s, x):
        inv = lax.rsqrt(jnp.mean(x * x, axis=-1, keepdims=True) + EPS)
        mul = g_ref[...] * (1.0 + sc_ref[s:s + 1, :])
        return x * inv * mul + sh_ref[s:s + 1, :]

    _row_loop(S, L, [x_ref], dst_ref, fn)


def _residual(m_ref, x_ref, g_ref, ga_ref, out_ref, S, L):
    def fn(s, m, x):
        inv = lax.rsqrt(jnp.mean(m * m, axis=-1, keepdims=True) + EPS)
        return x + m * inv * (g_ref[...] * ga_ref[s:s + 1, :])

    _row_loop(S, L, [m_ref, x_ref], out_ref, fn)


def _row_spec(tm):
    return pl.BlockSpec((None, tm, D_MODEL), lambda b, t, *_: (b, t, 0))


def _vec_spec(index, n=D_MODEL):
    return pl.BlockSpec((None, 1, n), lambda *_: (index, 0, 0))


def _mod_spec(S, layer, which):
    if S == 1:
        first = MOD_ROWS_SAMPLE // SUBLANES
        return pl.BlockSpec((None, None, SUBLANES, D_MODEL), lambda b, t, *_: (layer, which, first + b, 0))
    return pl.BlockSpec((None, None, S, D_MODEL), lambda b, t, *_: (layer, which, 0, 0))


def _mod_kernel(c_ref, w_ref, b_ref, o_ref):
    c = c_ref[...]
    o_ref[...] = _dot(c * jax.nn.sigmoid(c), w_ref[...]) + b_ref[...]


MOD_HEAD_VECS = 2
MOD_BLOCK = 512


def _modulation_head(c_all, w_mod, b_mod):
    rows = c_all.shape[0]
    tn = MOD_HEAD_CHUNK
    per = D_MODEL // tn
    return pl.pallas_call(
        _mod_kernel,
        grid=(MOD_HEAD_VECS * per,),
        in_specs=[pl.BlockSpec((rows, D_MODEL), lambda j: (0, 0)),
                  pl.BlockSpec((None, D_MODEL, tn), lambda j: (0, 0, j)),
                  pl.BlockSpec((None, 1, tn), lambda j: (0, 0, j))],
        out_specs=pl.BlockSpec((None, None, rows, tn), lambda j: (0, j // per, 0, j % per)),
        out_shape=jax.ShapeDtypeStruct((1, MOD_HEAD_VECS, rows, D_MODEL), F32),
        compiler_params=_cparams(1),
        name="modulation_head",
    )(c_all, w_mod, b_mod.reshape(DEPTH, 1, 6 * D_MODEL))


def _conv_in_kernel(x_ref, g_ref, sc_ref, sh_ref, wa_ref, wg_ref, ba_ref, bg_ref, u_ref, h_ref, *, S, L):
    @pl.when(pl.program_id(2) == 0)
    def _():
        _norm_mod(x_ref, h_ref, g_ref, sc_ref, sh_ref, S, L)

    h = h_ref[...]
    a = _dot(h, wa_ref[...]) + ba_ref[...]
    g = _dot(h, wg_ref[...]) + bg_ref[...]
    u_ref[...] = a * jax.nn.sigmoid(g)


def _conv_in(x, mod, g_norm, w_in, b_in, *, layer, S, L, tm):
    Bp, T, _ = x.shape
    tn = IN_PROJ_CHUNK
    J = D_MODEL // tn
    b2 = b_in.reshape(-1, 1, 2 * D_MODEL)
    return pl.pallas_call(
        functools.partial(_conv_in_kernel, S=S, L=L),
        grid=(Bp, T // tm, J),
        in_specs=[_row_spec(tm), _vec_spec(4 * layer + 0), _mod_spec(S, layer, 1), _mod_spec(S, layer, 0),
                  pl.BlockSpec((None, D_MODEL, tn), lambda b, t, j: (0, 0, j)),
                  pl.BlockSpec((None, D_MODEL, tn), lambda b, t, j: (0, 0, j + J)),
                  pl.BlockSpec((None, 1, tn), lambda b, t, j: (0, 0, j)),
                  pl.BlockSpec((None, 1, tn), lambda b, t, j: (0, 0, j + J))],
        out_specs=pl.BlockSpec((None, tm, tn), lambda b, t, j: (b, t, j)),
        out_shape=jax.ShapeDtypeStruct((Bp, T, D_MODEL), F32),
        scratch_shapes=[pltpu.VMEM((tm, D_MODEL), F32)],
        compiler_params=_cparams(3),
        name="conv_in",
    )(x, g_norm, mod, mod, w_in, w_in, b2, b2)


def _dwconv_block(win, wb_ref, bias, cols, R):
    n_groups = (CONV_K + 1 + SUBLANES) // SUBLANES
    win_rows = R + CONV_HIST
    acc = jnp.broadcast_to(bias, (R // SUBLANES, SUBLANES, LANES))
    for sh in range(SUBLANES):
        wsh = win if sh == 0 else pltpu.roll(win, win_rows - sh, axis=0)
        for a in range(n_groups):
            k = SUBLANES * a + sh - 2
            if 0 <= k < CONV_K:
                tap = wsh[SUBLANES * a:SUBLANES * a + R, :]
                acc = acc + tap.reshape(R // SUBLANES, SUBLANES, LANES) * wb_ref[k, :, cols]
    return acc.reshape(R, LANES)


def _fill_tap_table(wdw_ref, wb_ref):
    for k in range(CONV_K):
        wb_ref[k] = jnp.broadcast_to(wdw_ref[k:k + 1, :], (SUBLANES, D_MODEL))


def _conv_mix_kernel(x_ref, g_ref, sc_ref, sh_ref, wa_ref, wg_ref, ba_ref, bg_ref, wdw_ref, bdw_ref,
                     c_ref, wm_ref, bm_ref,
                     y_ref, tail_ref, mod_ref, h_ref, wb_ref, carry_ref, *, L, R, nb, n_mod):
    t = pl.program_id(1)
    j = pl.program_id(2)
    tn = wa_ref.shape[1]
    rb = L // nb
    step = (pl.program_id(0) * pl.num_programs(1) + t) * pl.num_programs(2) + j

    @pl.when(step < n_mod)
    def _():
        c = c_ref[...]
        mod_ref[...] = _dot(c * jax.nn.sigmoid(c), wm_ref[...]) + bm_ref[...]

    @pl.when((pl.program_id(0) == 0) & (t == 0) & (j == 0))
    def _():
        _fill_tap_table(wdw_ref, wb_ref)
        carry_ref[...] = jnp.zeros(carry_ref.shape, F32)

    @pl.when(j == 0)
    def _():
        _norm_mod(x_ref, h_ref, g_ref, sc_ref, sh_ref, 1, L)

    wa = wa_ref[...].astype(BF16)
    wg = wg_ref[...].astype(BF16)
    tail = jnp.where(t == 0, 0.0, carry_ref[j])
    for blk in range(nb):
        hb = h_ref[blk * rb:(blk + 1) * rb, :]
        a = _dot(hb, wa) + ba_ref[...]
        g = _dot(hb, wg) + bg_ref[...]
        u = a * jax.nn.sigmoid(g)
        ufull = jnp.concatenate([tail, u], axis=0)
        for cb in range(tn // LANES):
            gcols = pl.ds(pl.multiple_of(j * tn + cb * LANES, LANES), LANES)
            lcols = slice(cb * LANES, (cb + 1) * LANES)
            bias = bdw_ref[:, gcols]
            for r in range(rb // R):
                win = ufull[r * R:r * R + R + CONV_HIST, lcols]
                y_ref[blk * rb + r * R:blk * rb + (r + 1) * R, lcols] = _dwconv_block(win, wb_ref, bias, gcols, R)
        tail = u[rb - CONV_HIST:, :]
    carry_ref[j] = tail
    tail_ref[...] = tail


def _conv_mix(x, mod_head, g_norm, w_in, b_in, w_dw, b_dw, c_all, w_mod, b_mod, *, layer, tm, R, nb):
    Bp, T, _ = x.shape
    tn = MIX_CHUNK
    J = D_MODEL // tn
    nT = T // tm
    b2 = b_in.reshape(-1, 1, 2 * D_MODEL)
    rows = c_all.shape[0]
    per_vec = D_MODEL // MOD_BLOCK
    per_layer = 6 * per_vec
    n_mod = DEPTH * per_layer
    assert n_mod <= Bp * nT * J

    def mod_block(b, t, j):
        sb = jnp.minimum((b * nT + t) * J + j, n_mod - 1)
        return sb // per_layer, sb % per_layer

    def wm_map(b, t, j):
        layer_, cb = mod_block(b, t, j)
        return layer_, 0, cb

    def mod_out_map(b, t, j):
        layer_, cb = mod_block(b, t, j)
        return layer_, cb // per_vec, 0, cb % per_vec

    return pl.pallas_call(
        functools.partial(_conv_mix_kernel, L=tm, R=R, nb=nb, n_mod=n_mod),
        grid=(Bp, nT, J),
        in_specs=[_row_spec(tm), _vec_spec(4 * layer + 0), _mod_spec(1, layer, 1), _mod_spec(1, layer, 0),
                  pl.BlockSpec((None, D_MODEL, tn), lambda b, t, j: (0, 0, j)),
                  pl.BlockSpec((None, D_MODEL, tn), lambda b, t, j: (0, 0, j + J)),
                  pl.BlockSpec((None, 1, tn), lambda b, t, j: (0, 0, j)),
                  pl.BlockSpec((None, 1, tn), lambda b, t, j: (0, 0, j + J)),
                  pl.BlockSpec((None, CONV_K, D_MODEL), lambda b, t, j: (0, 0, 0)),
                  _vec_spec(0),
                  pl.BlockSpec((rows, D_MODEL), lambda b, t, j: (0, 0)),
                  pl.BlockSpec((None, D_MODEL, MOD_BLOCK), wm_map),
                  pl.BlockSpec((None, 1, MOD_BLOCK), wm_map)],
        out_specs=[pl.BlockSpec((None, tm, tn), lambda b, t, j: (b, t, j)),
                   pl.BlockSpec((None, None, CONV_HIST, tn), lambda b, t, j: (b, t, 0, j)),
                   pl.BlockSpec((None, None, rows, MOD_BLOCK), mod_out_map)],
        out_shape=[jax.ShapeDtypeStruct((Bp, T, D_MODEL), F32),
                   jax.ShapeDtypeStruct((Bp, nT, CONV_HIST, D_MODEL), F32),
                   jax.ShapeDtypeStruct((DEPTH, 6, rows, D_MODEL), F32)],
        scratch_shapes=[pltpu.VMEM((tm, D_MODEL), BF16),
                        pltpu.VMEM((CONV_K, SUBLANES, D_MODEL), F32),
                        pltpu.VMEM((J, CONV_HIST, tn), F32)],
        compiler_params=_cparams(3),
        name="conv_mix",
    )(x, g_norm, mod_head, mod_head, w_in, w_in, b2, b2, w_dw, b_dw.reshape(-1, 1, D_MODEL),
      c_all, w_mod, b_mod.reshape(DEPTH, 1, 6 * D_MODEL))


def _conv_proj_kernel(y_ref, lng_ref, lnb_ref, wout_ref, bout_ref, x_ref, g_ref, ga_ref, x1_ref, *, L):
    rb = L // PROJ_BLOCKS
    w = wout_ref[...]
    for blk in range(PROJ_BLOCKS):
        rows = slice(blk * rb, (blk + 1) * rb)
        a = _ln_silu_value(y_ref[rows, :], lng_ref, lnb_ref)
        m = _dot(a, w) + bout_ref[...]
        x1_ref[rows, :] = _residual_value(m, x_ref[rows, :], g_ref, ga_ref[0:1, :])


def _conv_proj(y, x, mod, g_norm, ln_g, ln_b, w_out, b_out, *, layer, tm):
    Bp, T, _ = x.shape
    vec = lambda a: a.reshape(-1, 1, D_MODEL)
    return pl.pallas_call(
        functools.partial(_conv_proj_kernel, L=tm),
        grid=(Bp, T // tm),
        in_specs=[_row_spec(tm), _vec_spec(0), _vec_spec(0),
                  pl.BlockSpec((None, D_MODEL, D_MODEL), lambda b, t: (0, 0, 0), pipeline_mode=pl.Buffered(1)),
                  _vec_spec(0), _row_spec(tm), _vec_spec(4 * layer + 1), _mod_spec(1, layer, 2)],
        out_specs=_row_spec(tm),
        out_shape=jax.ShapeDtypeStruct((Bp, T, D_MODEL), F32),
        compiler_params=_cparams(2),
        name="conv_proj",
    )(y, vec(ln_g), vec(ln_b), w_out, vec(b_out), x, g_norm, mod)


def _dwconv(full_ref, wb_ref, bdw_ref, y_ref, S, L, R):
    def body(cb, c):
        cols = pl.ds(pl.multiple_of(cb * LANES, LANES), LANES)
        bias = bdw_ref[:, cols]
        for s in range(S):
            for r in range(L // R):
                win = full_ref[s, r * R:r * R + R + CONV_HIST, cols]
                y_ref[s * L + r * R:s * L + (r + 1) * R, cols] = _dwconv_block(win, wb_ref, bias, cols, R)
        return c

    lax.fori_loop(0, D_MODEL // LANES, body, 0)


def _ln_silu_value(y, lng_ref, lnb_ref):
    mu = jnp.mean(y, axis=-1, keepdims=True)
    yc = y - mu
    var = jnp.mean(yc * yc, axis=-1, keepdims=True)
    z = yc * lax.rsqrt(var + EPS) * lng_ref[...] + lnb_ref[...]
    return z * jax.nn.sigmoid(z)


def _ln_silu(y_ref, lng_ref, lnb_ref, S, L):
    _row_loop(S, L, [y_ref], y_ref, lambda s, y: _ln_silu_value(y, lng_ref, lnb_ref))


def _conv_out_kernel(u_ref, hist_ref, wdw_ref, bdw_ref, lng_ref, lnb_ref, wout_ref, bout_ref, x_ref, g_ref, ga_ref,
                     x1_ref, full_ref, wb_ref, y_ref, *, S, L, R):
    _fill_tap_table(wdw_ref, wb_ref)
    pad = CONV_HIST - (CONV_K - 1)
    full_ref[:, 0:pad, :] = jnp.zeros((S, pad, D_MODEL), F32)
    full_ref[:, pad:CONV_HIST, :] = hist_ref[...]
    full_ref[:, CONV_HIST:CONV_HIST + L, :] = u_ref[...].reshape(S, L, D_MODEL)
    _dwconv(full_ref, wb_ref, bdw_ref, y_ref, S, L, R)
    _ln_silu(y_ref, lng_ref, lnb_ref, S, L)
    x1_ref[...] = _dot(y_ref[...], wout_ref[...]) + bout_ref[...]
    _residual(x1_ref, x_ref, g_ref, ga_ref, x1_ref, S, L)


def _conv_out(u, hist, x, mod, g_norm, w_dw, b_dw, ln_g, ln_b, w_out, b_out, *, layer, S, L, tm, R):
    Bp, T, _ = x.shape
    assert Bp == 1 and T == tm
    vec = lambda a: a.reshape(-1, 1, D_MODEL)
    return pl.pallas_call(
        functools.partial(_conv_out_kernel, S=S, L=L, R=R),
        grid=(Bp, T // tm),
        in_specs=[_row_spec(tm), pl.BlockSpec((S, CONV_K - 1, D_MODEL), lambda b, t: (0, 0, 0)),
                  pl.BlockSpec((None, CONV_K, D_MODEL), lambda b, t: (0, 0, 0)),
                  _vec_spec(0), _vec_spec(0), _vec_spec(0),
                  pl.BlockSpec((None, D_MODEL, D_MODEL), lambda b, t: (0, 0, 0), pipeline_mode=pl.Buffered(1)),
                  _vec_spec(0), _row_spec(tm), _vec_spec(4 * layer + 1), _mod_spec(S, layer, 2)],
        out_specs=_row_spec(tm),
        out_shape=jax.ShapeDtypeStruct((Bp, T, D_MODEL), F32),
        scratch_shapes=[pltpu.VMEM((S, CONV_HIST + L, D_MODEL), F32),
                        pltpu.VMEM((CONV_K, SUBLANES, D_MODEL), F32),
                        pltpu.VMEM((tm, D_MODEL), F32)],
        compiler_params=_cparams(2),
        name="conv_out",
    )(u, hist, w_dw, vec(b_dw), vec(ln_g), vec(ln_b), w_out, vec(b_out), x, g_norm, mod)


def _ffn_kernel(*refs, S, L, nb, prompt):
    if prompt:
        (x_ref, g2_ref, sc_ref, sh_ref, wg_ref, wv_ref, wdw_ref, bdw_ref, wd_ref, g3_ref, ga_ref,
         x2_ref, st_ref, h_ref, carry_ref) = refs
    else:
        (x_ref, g2_ref, sc_ref, sh_ref, wg_ref, wv_ref, wdw_ref, bdw_ref, wd_ref, g3_ref, ga_ref, hist_ref,
         x2_ref, st_ref, h_ref) = refs
    t = pl.program_id(1)
    j = pl.program_id(2)
    last_j = pl.num_programs(2) - 1
    tm = x_ref.shape[0]
    rb = tm // nb
    seg = min(L, rb)
    pad = SUBLANES
    inline = S == 1

    mm_dtype = h_ref.dtype

    def step(first, last):
        wg = wg_ref[...].astype(mm_dtype)
        wv = wv_ref[...].astype(mm_dtype)
        wd = wd_ref[...].astype(mm_dtype)
        w0 = wdw_ref[0:1, :]
        w1 = wdw_ref[1:2, :]
        w2 = wdw_ref[2:3, :]
        bd = bdw_ref[...]
        tail = None
        for blk in range(nb):
            rows = slice(blk * rb, (blk + 1) * rb)
            if first:
                x = x_ref[rows, :]
                inv = lax.rsqrt(jnp.mean(x * x, axis=-1, keepdims=True) + EPS)
                hb = (x * inv * (g2_ref[...] * (1.0 + sc_ref[0:1, :])) + sh_ref[0:1, :]).astype(mm_dtype)
                h_ref[rows, :] = hb
            else:
                hb = h_ref[rows, :]
            g = _dot(hb, wg)
            v = _dot(hb, wv)
            p1s, p2s = [], []
            for q in range(rb // seg):
                gq = g[q * seg:(q + 1) * seg, :]
                if not prompt:
                    s = blk * (rb // seg) + q
                    hist = hist_ref[s]
                    st_ref[s] = gq[seg - (FFN_K - 1):, :]
                elif blk == 0:
                    hist = jnp.where(t == 0, 0.0, carry_ref[j])
                else:
                    hist = tail
                gfull = jnp.concatenate([hist, gq], axis=0)
                p1s.append(pltpu.roll(gfull, 1, axis=0)[pad:, :])
                p2s.append(pltpu.roll(gfull, 2, axis=0)[pad:, :])
                tail = gq[seg - pad:, :]
            p1 = p1s[0] if len(p1s) == 1 else jnp.concatenate(p1s, axis=0)
            p2 = p2s[0] if len(p2s) == 1 else jnp.concatenate(p2s, axis=0)
            gc = g * w2 + p1 * w1 + p2 * w0 + bd
            act = (jax.nn.gelu(gc) * v).astype(mm_dtype)
            acc = _dot(act, wd)
            if not first:
                acc = x2_ref[rows, :] + acc
            if last:
                acc = _residual_value(acc, x_ref[rows, :], g3_ref, ga_ref[0:1, :])
            x2_ref[rows, :] = acc
        if prompt:
            carry_ref[j] = tail
            st_ref[0] = tail[pad - (FFN_K - 1):, :]

    if inline:
        pl.when(j == 0)(lambda: step(True, False))
        pl.when(j > 0)(lambda: step(False, False))

        @pl.when(j == last_j)
        def _():
            _residual(x2_ref, x_ref, g3_ref, ga_ref, x2_ref, S, L)
    else:
        @pl.when(j == 0)
        def _():
            _norm_mod(x_ref, h_ref, g2_ref, sc_ref, sh_ref, S, L)
            x2_ref[...] = jnp.zeros(x2_ref.shape, F32)

        step(False, False)

        @pl.when(j == last_j)
        def _():
            _residual(x2_ref, x_ref, g3_ref, ga_ref, x2_ref, S, L)


def _ffn(x, hist, mod, g_norm, w_up, w_dw, b_dw, w_down, *, layer, S, L, tm, tf, nb):
    Bp, T, _ = x.shape
    J = D_FF // tf
    nT = T // tm
    prompt = hist is None
    x_spec = pl.BlockSpec((None, tm, D_MODEL), lambda b, t, j: (b, t, 0), pipeline_mode=pl.Buffered(1))
    in_specs = [x_spec, _vec_spec(4 * layer + 2), _mod_spec(S, layer, 4), _mod_spec(S, layer, 3),
                pl.BlockSpec((None, D_MODEL, tf), lambda b, t, j: (layer, 0, j)),
                pl.BlockSpec((None, D_MODEL, tf), lambda b, t, j: (layer, 0, j + J)),
                pl.BlockSpec((None, FFN_K, tf), lambda b, t, j: (layer, 0, j)),
                pl.BlockSpec((None, 1, tf), lambda b, t, j: (layer, 0, j)),
                pl.BlockSpec((None, tf, D_MODEL), lambda b, t, j: (layer, j, 0)),
                _vec_spec(4 * layer + 3), _mod_spec(S, layer, 5)]
    args = [x, g_norm, mod, mod, w_up, w_up, w_dw, b_dw.reshape(DEPTH, 1, D_FF), w_down, g_norm, mod]
    scratch = [pltpu.VMEM((tm, D_MODEL), BF16 if prompt else F32)]
    if prompt:
        scratch.append(pltpu.VMEM((J, SUBLANES, tf), F32))
    else:
        in_specs.append(pl.BlockSpec((S, SUBLANES, tf), lambda b, t, j: (0, 0, j)))
        args.append(hist)
    return pl.pallas_call(
        functools.partial(_ffn_kernel, S=S, L=L, nb=nb, prompt=prompt),
        grid=(Bp, nT, J),
        in_specs=in_specs,
        out_specs=[_row_spec(tm),
                   pl.BlockSpec((None, None, S, FFN_K - 1, tf), lambda b, t, j: (b, t, 0, 0, j))],
        out_shape=[jax.ShapeDtypeStruct((Bp, T, D_MODEL), F32),
                   jax.ShapeDtypeStruct((Bp, nT, S, FFN_K - 1, D_FF), F32)],
        scratch_shapes=scratch,
        compiler_params=pltpu.CompilerParams(dimension_semantics=("arbitrary",) * 3,
                                             vmem_limit_bytes=VMEM_LIMIT_FFN),
        name="ffn",
    )(*args)


def _qkv_kernel(x_ref, g_ref, sc_ref, sh_ref, wq_ref, wk_ref, wv_ref, q_ref, k_ref, v_ref, h_ref, *, S, L):
    tm = x_ref.shape[0]
    if S > 1:
        _norm_mod(x_ref, h_ref, g_ref, sc_ref, sh_ref, S, L)
        blocks = [(slice(0, tm), h_ref[...])]
    else:
        rb = tm // PROJ_BLOCKS
        mul = g_ref[...] * (1.0 + sc_ref[0:1, :])
        blocks = []
        for blk in range(PROJ_BLOCKS):
            rows = slice(blk * rb, (blk + 1) * rb)
            x = x_ref[rows, :]
            inv = lax.rsqrt(jnp.mean(x * x, axis=-1, keepdims=True) + EPS)
            blocks.append((rows, x * inv * mul + sh_ref[0:1, :]))
    wq = wq_ref[...]
    wk = wk_ref[...]
    wv = wv_ref[...]
    for rows, h in blocks:
        q_ref[rows, :] = _dot(h, wq)
        k_ref[rows, :] = _dot(h, wk)
        v_ref[rows, :] = _dot(h, wv)


def _qkv(x, mod, g_norm, w_q, w_k, w_v, *, layer, S, L, tm):
    Bp, T, _ = x.shape
    kv_spec = pl.BlockSpec((None, tm, KV_DIM), lambda b, t: (b, t, 0))
    resident = lambda n: pl.BlockSpec((None, D_MODEL, n), lambda b, t: (0, 0, 0), pipeline_mode=pl.Buffered(1))
    return pl.pallas_call(
        functools.partial(_qkv_kernel, S=S, L=L),
        grid=(Bp, T // tm),
        in_specs=[_row_spec(tm), _vec_spec(4 * layer + 0), _mod_spec(S, layer, 1), _mod_spec(S, layer, 0),
                  resident(D_MODEL), resident(KV_DIM), resident(KV_DIM)],
        out_specs=[_row_spec(tm), kv_spec, kv_spec],
        out_shape=[jax.ShapeDtypeStruct((Bp, T, D_MODEL), F32),
                   jax.ShapeDtypeStruct((Bp, T, KV_DIM), F32),
                   jax.ShapeDtypeStruct((Bp, T, KV_DIM), F32)],
        scratch_shapes=[pltpu.VMEM((tm, D_MODEL), F32)],
        compiler_params=_cparams(2),
        name="qkv",
    )(x, g_norm, mod, mod, w_q, w_k, w_v)


REL_SPAN = 3 * LANES


def _bias_kernel(rb_ref, prof_ref, valid_ref, o_ref, *, Qn):
    prof = prof_ref[...]
    valid = valid_ref[...] != 0
    early = lax.broadcasted_iota(jnp.int32, valid.shape, 0) < WINDOW
    for hh in range(N_HEADS):
        f = jnp.zeros(prof.shape, F32)
        for b in range(N_BUCKETS):
            f = jnp.where(prof == b, rb_ref[b, hh], f)
        h, g = divmod(hh, GROUP)
        off = (g * Qn) % LANES
        x = jnp.broadcast_to(f[0:1, :], (KEY_WIN, REL_SPAN))
        r = pltpu.roll(x, WINDOW + off, axis=1, stride=1, stride_axis=0)
        tab = jnp.where(valid, r[:, off:off + Qn], NEG_INF)
        o_ref[0, h, :, g * Qn:(g + 1) * Qn] = tab
        if o_ref.shape[0] > 1:
            o_ref[1, h, :, g * Qn:(g + 1) * Qn] = jnp.where(early, NEG_INF, tab)


def _t5_bucket(rel):
    half = N_BUCKETS // 2
    max_exact = half // 2
    n = jnp.abs(rel)
    ret = jnp.where(rel > 0, half, 0)
    nf = jnp.maximum(n, 1).astype(F32)
    large = max_exact + (jnp.log(nf / max_exact) / math.log(MAX_DISTANCE / max_exact)
                         * (half - max_exact)).astype(jnp.int32)
    large = jnp.minimum(large, half - 1)
    return ret + jnp.where(n < max_exact, n, large)


def _bias_table(rel_bias, Qn, lo, hi, variants):
    assert KEY_WIN + Qn - 1 <= REL_SPAN
    k = jnp.arange(KEY_WIN)[:, None]
    q = jnp.arange(Qn)[None, :]
    valid = ((k >= lo(q)) & (k < hi(q))).astype(jnp.int32)
    prof = jnp.broadcast_to(_t5_bucket(WINDOW - jnp.arange(REL_SPAN)).astype(jnp.int32)[None, :], (SUBLANES, REL_SPAN))
    return pl.pallas_call(
        functools.partial(_bias_kernel, Qn=Qn),
        in_specs=[pl.BlockSpec(memory_space=pltpu.SMEM),
                  pl.BlockSpec((SUBLANES, REL_SPAN), lambda: (0, 0)),
                  pl.BlockSpec((KEY_WIN, Qn), lambda: (0, 0))],
        out_specs=pl.BlockSpec((variants, N_KV_HEADS, KEY_WIN, GROUP * Qn), lambda: (0, 0, 0, 0)),
        out_shape=jax.ShapeDtypeStruct((variants, N_KV_HEADS, KEY_WIN, GROUP * Qn), F32),
        name="rel_bias_table",
    )(rel_bias, prof, valid)


def _attend(q_ref, r0, Qn, kw, vw, bias_ref, sel, sink_ref, o_ref, qs_ref):
    rows = pl.ds(r0, Qn)
    for g in range(GROUP):
        for h in range(N_KV_HEADS):
            src = (h * GROUP + g) * HEAD_DIM
            qs_ref[g * Qn:(g + 1) * Qn, h * HEAD_DIM:(h + 1) * HEAD_DIM] = q_ref[rows, src:src + HEAD_DIM] * ATTN_SCALE
    qt = qs_ref[...].T
    vt = vw.T
    outs = []
    for h in range(N_KV_HEADS):
        hd = slice(h * HEAD_DIM, (h + 1) * HEAD_DIM)
        s = _dot(kw[:, hd], qt[hd, :]) + bias_ref[sel, h]
        sk = sink_ref[h]
        mx = jnp.maximum(jnp.max(s, axis=0, keepdims=True), sk)
        p = jnp.exp(s - mx)
        den = jnp.sum(p, axis=0, keepdims=True) + jnp.exp(sk - mx)
        outs.append(_dot(vt[hd, :], p) * (1.0 / den))
    o2 = jnp.concatenate(outs, axis=0).T
    for g in range(GROUP):
        for h in range(N_KV_HEADS):
            dst = (h * GROUP + g) * HEAD_DIM
            o_ref[rows, dst:dst + HEAD_DIM] = o2[g * Qn:(g + 1) * Qn, h * HEAD_DIM:(h + 1) * HEAD_DIM]


def _attn_prompt_kernel(q_ref, kp_ref, kc_ref, vp_ref, vc_ref, bias_ref, sink_ref, o_ref, kw_ref, vw_ref, qs_ref,
                        *, tq, Qn):
    t = pl.program_id(1)
    kw_ref[0:WINDOW, :] = kp_ref[...]
    kw_ref[WINDOW:WINDOW + tq, :] = kc_ref[...]
    vw_ref[0:WINDOW, :] = vp_ref[...]
    vw_ref[WINDOW:WINDOW + tq, :] = vc_ref[...]

    def body(mp, carry):
        for u in range(ATTN_UNITS_PER_ITER):
            m = mp * ATTN_UNITS_PER_ITER + u
            r0 = pl.multiple_of(m * Qn, Qn)
            kw = kw_ref[pl.ds(r0, KEY_WIN), :]
            vw = vw_ref[pl.ds(r0, KEY_WIN), :]
            sel = ((t == 0) & (m == 0)).astype(jnp.int32)
            _attend(q_ref, r0, Qn, kw, vw, bias_ref, sel, sink_ref, o_ref, qs_ref.at[u])
        return carry

    lax.fori_loop(0, tq // (Qn * ATTN_UNITS_PER_ITER), body, 0)


def _attn_sample_kernel(q_ref, ck_ref, kn_ref, cv_ref, vn_ref, bias_ref, sink_ref, o_ref, kw_ref, vw_ref, qs_ref,
                        *, S, L):
    P = ck_ref.shape[1]
    kw_ref[:, 0:P, :] = ck_ref[...]
    kw_ref[:, P:P + L, :] = kn_ref[...].reshape(S, L, KV_DIM)
    kw_ref[:, P + L:, :] = jnp.zeros((S, KEY_WIN - P - L, KV_DIM), F32)
    vw_ref[:, 0:P, :] = cv_ref[...]
    vw_ref[:, P:P + L, :] = vn_ref[...].reshape(S, L, KV_DIM)
    vw_ref[:, P + L:, :] = jnp.zeros((S, KEY_WIN - P - L, KV_DIM), F32)

    def body(sp, carry):
        for u in range(ATTN_UNITS_PER_ITER):
            s = sp * ATTN_UNITS_PER_ITER + u
            r0 = pl.multiple_of(s * L, L)
            _attend(q_ref, r0, L, kw_ref[s], vw_ref[s], bias_ref, 0, sink_ref, o_ref, qs_ref.at[u])
        return carry

    lax.fori_loop(0, S // ATTN_UNITS_PER_ITER, body, 0)


def _attn_prompt(q, k, v, bias, sink, *, tq):
    Bp, T, _ = q.shape
    Qn = 2 * CHUNK
    per = tq // WINDOW
    prev_spec = pl.BlockSpec((None, WINDOW, KV_DIM), lambda b, t: (b, jnp.maximum(t * per - 1, 0), 0))
    cur_spec = pl.BlockSpec((None, tq, KV_DIM), lambda b, t: (b, t, 0))
    return pl.pallas_call(
        functools.partial(_attn_prompt_kernel, tq=tq, Qn=Qn),
        grid=(Bp, T // tq),
        in_specs=[_row_spec(tq), prev_spec, cur_spec, prev_spec, cur_spec,
                  pl.BlockSpec((2, N_KV_HEADS, KEY_WIN, GROUP * Qn), lambda b, t: (0, 0, 0, 0),
                               pipeline_mode=pl.Buffered(1)),
                  pl.BlockSpec((N_KV_HEADS, 1, GROUP * Qn), lambda b, t: (0, 0, 0))],
        out_specs=_row_spec(tq),
        out_shape=jax.ShapeDtypeStruct((Bp, T, D_MODEL), F32),
        scratch_shapes=[pltpu.VMEM((WINDOW + tq, KV_DIM), F32), pltpu.VMEM((WINDOW + tq, KV_DIM), F32),
                        pltpu.VMEM((ATTN_UNITS_PER_ITER, GROUP * Qn, KV_DIM), F32)],
        compiler_params=_cparams(2),
        name="attn_prompt",
    )(q, k, k, v, v, bias, sink)


def _attn_sample(q, k, v, cache_k, cache_v, bias, sink, *, S, L):
    return pl.pallas_call(
        functools.partial(_attn_sample_kernel, S=S, L=L),
        out_shape=jax.ShapeDtypeStruct((S * L, D_MODEL), F32),
        scratch_shapes=[pltpu.VMEM((S, KEY_WIN, KV_DIM), F32), pltpu.VMEM((S, KEY_WIN, KV_DIM), F32),
                        pltpu.VMEM((ATTN_UNITS_PER_ITER, GROUP * L, KV_DIM), F32)],
        compiler_params=pltpu.CompilerParams(vmem_limit_bytes=VMEM_LIMIT),
        name="attn_sample",
    )(q, cache_k, k, cache_v, v, bias, sink)


def _residual_value(m, x, g_ref, ga):
    inv = lax.rsqrt(jnp.mean(m * m, axis=-1, keepdims=True) + EPS)
    return x + m * inv * (g_ref[...] * ga)


def _attn_out_kernel(o_ref, w_ref, x_ref, g_ref, ga_ref, x1_ref, *, S, L):
    tm = o_ref.shape[0]
    if S > 1:
        x1_ref[...] = _dot(o_ref[...], w_ref[...])
        _residual(x1_ref, x_ref, g_ref, ga_ref, x1_ref, S, L)
        return
    rb = tm // PROJ_BLOCKS
    w = w_ref[...]
    for blk in range(PROJ_BLOCKS):
        rows = slice(blk * rb, (blk + 1) * rb)
        x1_ref[rows, :] = _residual_value(_dot(o_ref[rows, :], w), x_ref[rows, :], g_ref, ga_ref[0:1, :])


def _attn_out(o, x, mod, g_norm, w_o, *, layer, S, L, tm):
    Bp, T, _ = x.shape
    return pl.pallas_call(
        functools.partial(_attn_out_kernel, S=S, L=L),
        grid=(Bp, T // tm),
        in_specs=[_row_spec(tm),
                  pl.BlockSpec((None, D_MODEL, D_MODEL), lambda b, t: (0, 0, 0), pipeline_mode=pl.Buffered(1)),
                  _row_spec(tm), _vec_spec(4 * layer + 1), _mod_spec(S, layer, 2)],
        out_specs=_row_spec(tm),
        out_shape=jax.ShapeDtypeStruct((Bp, T, D_MODEL), F32),
        compiler_params=_cparams(2),
        name="attn_out",
    )(o, w_o, x, g_norm, mod)


def kernel(x_prompt, x_sample, c_prompt, c_sample, cache_conv, cache_k, cache_v, cache_ffn, w_mod, b_mod, g_norm, conv_w_in, conv_b_in, conv_w_dw, conv_b_dw, conv_ln_g, conv_ln_b, conv_w_out, conv_b_out, attn_w_q, attn_w_k, attn_w_v, attn_w_o, attn_sinks, rel_bias, ffn_w_up, ffn_w_dw, ffn_b_dw, ffn_w_down):
    B, T, D = x_prompt.shape
    SB, SL, _ = x_sample.shape
    assert SB == MOD_ROWS_SAMPLE
    c_all = jnp.concatenate(
        [c_sample, jnp.pad(c_prompt[:, None, :], ((0, 0), (0, SUBLANES - 1), (0, 0))).reshape(B * SUBLANES, D)], axis=0)
    mod_head = _modulation_head(c_all, w_mod, b_mod)
    gn = g_norm.reshape(DEPTH * 4, 1, D)

    geo_p = dict(S=1, L=STREAM_ROWS, tm=STREAM_ROWS)
    geo_p_small = dict(S=1, L=RESIDENT_ROWS, tm=RESIDENT_ROWS)
    geo_s = dict(S=SB, L=SL, tm=SB * SL)

    xp = x_prompt
    xs = x_sample.reshape(1, SB * SL, D)

    yp, tail_p, mod = _conv_mix(xp, mod_head, gn, conv_w_in, conv_b_in, conv_w_dw, conv_b_dw, c_all, w_mod, b_mod,
                                layer=0, tm=STREAM_ROWS, R=MIX_CONV_ROWS, nb=MIX_ROW_BLOCKS)
    xp = _conv_proj(yp, xp, mod, gn, conv_ln_g, conv_ln_b, conv_w_out, conv_b_out, layer=0, tm=RESIDENT_ROWS)
    us = _conv_in(xs, mod_head, gn, conv_w_in, conv_b_in, layer=0, **geo_s)
    conv_args = (conv_w_dw, conv_b_dw, conv_ln_g, conv_ln_b, conv_w_out, conv_b_out)
    xs = _conv_out(us, cache_conv[0], xs, mod, gn, *conv_args, layer=0, R=SL, **geo_s)
    conv_state_p = tail_p[:, -1, CONV_HIST - (CONV_K - 1):, :][None]
    conv_state_s = us.reshape(SB, SL, D)[:, SL - (CONV_K - 1):, :][None]

    ffn_p, ffn_s = [], []
    ffn_hist = jnp.pad(cache_ffn, ((0, 0), (0, 0), (SUBLANES - (FFN_K - 1), 0), (0, 0)))

    def run_ffn(i, xp, xs):
        w = (ffn_w_up, ffn_w_dw, ffn_b_dw, ffn_w_down)
        xp, st_p = _ffn(xp, None, mod, gn, *w, layer=i, tf=FFN_CHUNK, nb=FFN_ROW_BLOCKS, **geo_p)
        xs, st_s = _ffn(xs, ffn_hist[i], mod, gn, *w, layer=i, tf=FFN_CHUNK, nb=FFN_ROW_BLOCKS, **geo_s)
        ffn_p.append(st_p[:, -1, 0])
        ffn_s.append(st_s[0, 0])
        return xp, xs

    xp, xs = run_ffn(0, xp, xs)

    qp, kp, vp = _qkv(xp, mod, gn, attn_w_q, attn_w_k, attn_w_v, layer=1, **geo_p_small)
    qs, ks, vs = _qkv(xs, mod, gn, attn_w_q, attn_w_k, attn_w_v, layer=1, **geo_s)
    P = cache_k.shape[2]
    assert P == WINDOW and P + SL <= KEY_WIN
    bias_p = _bias_table(rel_bias, 2 * CHUNK,
                         lambda q: (q // CHUNK) * CHUNK, lambda q: (q // CHUNK) * CHUNK + WINDOW + CHUNK, 2)
    bias_s = _bias_table(rel_bias, SL, lambda q: 0 * q, lambda q: 0 * q + P + SL, 1)
    sink_p = jnp.repeat(attn_sinks[0], 2 * CHUNK).reshape(N_KV_HEADS, 1, GROUP * 2 * CHUNK)
    sink_s = jnp.repeat(attn_sinks[0], SL).reshape(N_KV_HEADS, 1, GROUP * SL)
    op = _attn_prompt(qp, kp, vp, bias_p, sink_p, tq=RESIDENT_ROWS)
    ck = cache_k[0].reshape(SB, P, KV_DIM)
    cv = cache_v[0].reshape(SB, P, KV_DIM)
    os_ = _attn_sample(qs[0], ks[0], vs[0], ck, cv, bias_s, sink_s, S=SB, L=SL)[None]
    xp = _attn_out(op, xp, mod, gn, attn_w_o, layer=1, **geo_p_small)
    xs = _attn_out(os_, xs, mod, gn, attn_w_o, layer=1, **geo_s)
    keep = min(WINDOW, T)
    k_state_p = kp[:, T - keep:].reshape(B, keep, N_KV_HEADS, HEAD_DIM)[None]
    v_state_p = vp[:, T - keep:].reshape(B, keep, N_KV_HEADS, HEAD_DIM)[None]
    k_state_s = jnp.concatenate([ck, ks.reshape(SB, SL, KV_DIM)], axis=1)[:, SL:]
    v_state_s = jnp.concatenate([cv, vs.reshape(SB, SL, KV_DIM)], axis=1)[:, SL:]
    k_state_s = k_state_s.reshape(SB, P, N_KV_HEADS, HEAD_DIM)[None]
    v_state_s = v_state_s.reshape(SB, P, N_KV_HEADS, HEAD_DIM)[None]

    xp, xs = run_ffn(1, xp, xs)

    return (xp, xs.reshape(SB, SL, D), conv_state_p, conv_state_s,
            k_state_p, v_state_p, k_state_s, v_state_s,
            jnp.stack(ffn_p), jnp.stack(ffn_s))
```

```python
import functools
import math

import jax
import jax.numpy as jnp
from jax import lax
from jax.experimental import pallas as pl
from jax.experimental.pallas import tpu as pltpu

F32 = jnp.float32
BF16 = jnp.bfloat16

D_MODEL = 2048
D_FF = 5632
DEPTH = 2
CONV_K = 31
FFN_K = 3
N_HEADS = 32
N_KV_HEADS = 4
GROUP = N_HEADS // N_KV_HEADS
HEAD_DIM = 64
KV_DIM = N_KV_HEADS * HEAD_DIM
CHUNK = 64
WINDOW = 128
N_BUCKETS = 32
MAX_DISTANCE = 128
EPS = 1e-6
NEG_INF = -1e30
ATTN_SCALE = HEAD_DIM ** -0.5

SUBLANES = 8
LANES = 128
CONV_HIST = 32
KEY_WIN = 2 * WINDOW
VMEM_LIMIT = 58 * 1024 * 1024
VMEM_LIMIT_FFN = 62 * 1024 * 1024
MOD_ROWS_SAMPLE = 16


def _cparams(n_grid):
    return pltpu.CompilerParams(dimension_semantics=("arbitrary",) * n_grid,
                                vmem_limit_bytes=VMEM_LIMIT)


def _dot(a, b):
    return jnp.dot(a, b, preferred_element_type=F32)


ROW_SLAB = 64
ROW_GROUP = 2
PROJ_BLOCKS = 2

STREAM_ROWS = 1024
RESIDENT_ROWS = 512
FFN_CHUNK = 512
FFN_ROW_BLOCKS = 1
MIX_CHUNK = 256
MIX_ROW_BLOCKS = 4
MIX_CONV_ROWS = 64
IN_PROJ_CHUNK = 512
MOD_HEAD_CHUNK = 1024
ATTN_UNITS_PER_ITER = 4


def _row_loop(S, L, srcs, dst, fn):
    slab = min(L, ROW_SLAB)
    n = L // slab
    group = ROW_GROUP if n % ROW_GROUP == 0 else 1
    for s in range(S):

        def body(r, c, s=s):
            rows = [pl.ds(pl.multiple_of(s * L + (r * group + i) * slab, slab), slab) for i in range(group)]
            vals = [[src[rw, :] for src in srcs] for rw in rows]
            outs = [fn(s, *v) for v in vals]
            for rw, out in zip(rows, outs):
                dst[rw, :] = out.astype(dst.dtype)
            return c

        lax.fori_loop(0, n // group, body, 0)


def _norm_mod(x_ref, dst_ref, g_ref, sc_ref, sh_ref, S, L):
    def fn(s, x):
        inv = lax.rsqrt(jnp.mean(x * x, axis=-1, keepdims=True) + EPS)
        mul = g_ref[...] * (1.0 + sc_ref[s:s + 1, :])
        return x * inv * mul + sh_ref[s:s + 1, :]

    _row_loop(S, L, [x_ref], dst_ref, fn)


def _residual(m_ref, x_ref, g_ref, ga_ref, out_ref, S, L):
    def fn(s, m, x):
        inv = lax.rsqrt(jnp.mean(m * m, axis=-1, keepdims=True) + EPS)
        return x + m * inv * (g_ref[...] * ga_ref[s:s + 1, :])

    _row_loop(S, L, [m_ref, x_ref], out_ref, fn)


def _row_spec(tm):
    return pl.BlockSpec((None, tm, D_MODEL), lambda b, t, *_: (b, t, 0))


def _vec_spec(index, n=D_MODEL):
    return pl.BlockSpec((None, 1, n), lambda *_: (index, 0, 0))


def _mod_spec(S, layer, which):
    if S == 1:
        first = MOD_ROWS_SAMPLE // SUBLANES
        return pl.BlockSpec((None, None, SUBLANES, D_MODEL), lambda b, t, *_: (layer, which, first + b, 0))
    return pl.BlockSpec((None, None, S, D_MODEL), lambda b, t, *_: (layer, which, 0, 0))


def _mod_kernel(c_ref, w_ref, b_ref, o_ref):
    c = c_ref[...]
    o_ref[...] = _dot(c * jax.nn.sigmoid(c), w_ref[...]) + b_ref[...]


MOD_HEAD_VECS = 2
MOD_BLOCK = 512


def _modulation_head(c_all, w_mod, b_mod):
    rows = c_all.shape[0]
    tn = MOD_HEAD_CHUNK
    per = D_MODEL // tn
    return pl.pallas_call(
        _mod_kernel,
        grid=(MOD_HEAD_VECS * per,),
        in_specs=[pl.BlockSpec((rows, D_MODEL), lambda j: (0, 0)),
                  pl.BlockSpec((None, D_MODEL, tn), lambda j: (0, 0, j)),
                  pl.BlockSpec((None, 1, tn), lambda j: (0, 0, j))],
        out_specs=pl.BlockSpec((None, None, rows, tn), lambda j: (0, j // per, 0, j % per)),
        out_shape=jax.ShapeDtypeStruct((1, MOD_HEAD_VECS, rows, D_MODEL), F32),
        compiler_params=_cparams(1),
        name="modulation_head",
    )(c_all, w_mod, b_mod.reshape(DEPTH, 1, 6 * D_MODEL))


def _conv_in_kernel(x_ref, g_ref, sc_ref, sh_ref, wa_ref, wg_ref, ba_ref, bg_ref, u_ref, h_ref, *, S, L):
    @pl.when(pl.program_id(2) == 0)
    def _():
        _norm_mod(x_ref, h_ref, g_ref, sc_ref, sh_ref, S, L)

    h = h_ref[...]
    a = _dot(h, wa_ref[...]) + ba_ref[...]
    g = _dot(h, wg_ref[...]) + bg_ref[...]
    u_ref[...] = a * jax.nn.sigmoid(g)


def _conv_in(x, mod, g_norm, w_in, b_in, *, layer, S, L, tm):
    Bp, T, _ = x.shape
    tn = IN_PROJ_CHUNK
    J = D_MODEL // tn
    b2 = b_in.reshape(-1, 1, 2 * D_MODEL)
    return pl.pallas_call(
        functools.partial(_conv_in_kernel, S=S, L=L),
        grid=(Bp, T // tm, J),
        in_specs=[_row_spec(tm), _vec_spec(4 * layer + 0), _mod_spec(S, layer, 1), _mod_spec(S, layer, 0),
                  pl.BlockSpec((None, D_MODEL, tn), lambda b, t, j: (0, 0, j)),
                  pl.BlockSpec((None, D_MODEL, tn), lambda b, t, j: (0, 0, j + J)),
                  pl.BlockSpec((None, 1, tn), lambda b, t, j: (0, 0, j)),
                  pl.BlockSpec((None, 1, tn), lambda b, t, j: (0, 0, j + J))],
        out_specs=pl.BlockSpec((None, tm, tn), lambda b, t, j: (b, t, j)),
        out_shape=jax.ShapeDtypeStruct((Bp, T, D_MODEL), F32),
        scratch_shapes=[pltpu.VMEM((tm, D_MODEL), F32)],
        compiler_params=_cparams(3),
        name="conv_in",
    )(x, g_norm, mod, mod, w_in, w_in, b2, b2)


def _dwconv_block(win, wb_ref, bias, cols, R):
    n_groups = (CONV_K + 1 + SUBLANES) // SUBLANES
    win_rows = R + CONV_HIST
    acc = jnp.broadcast_to(bias, (R // SUBLANES, SUBLANES, LANES))
    for sh in range(SUBLANES):
        wsh = win if sh == 0 else pltpu.roll(win, win_rows - sh, axis=0)
        for a in range(n_groups):
            k = SUBLANES * a + sh - 2
            if 0 <= k < CONV_K:
                tap = wsh[SUBLANES * a:SUBLANES * a + R, :]
                acc = acc + tap.reshape(R // SUBLANES, SUBLANES, LANES) * wb_ref[k, :, cols]
    return acc.reshape(R, LANES)


def _fill_tap_table(wdw_ref, wb_ref):
    for k in range(CONV_K):
        wb_ref[k] = jnp.broadcast_to(wdw_ref[k:k + 1, :], (SUBLANES, D_MODEL))


def _conv_mix_kernel(x_ref, g_ref, sc_ref, sh_ref, wa_ref, wg_ref, ba_ref, bg_ref, wdw_ref, bdw_ref,
                     c_ref, wm_ref, bm_ref,
                     y_ref, tail_ref, mod_ref, h_ref, wb_ref, carry_ref, *, L, R, nb, n_mod):
    t = pl.program_id(1)
    j = pl.program_id(2)
    tn = wa_ref.shape[1]
    rb = L // nb
    step = (pl.program_id(0) * pl.num_programs(1) + t) * pl.num_programs(2) + j

    @pl.when(step < n_mod)
    def _():
        c = c_ref[...]
        mod_ref[...] = _dot(c * jax.nn.sigmoid(c), wm_ref[...]) + bm_ref[...]

    @pl.when((pl.program_id(0) == 0) & (t == 0) & (j == 0))
    def _():
        _fill_tap_table(wdw_ref, wb_ref)
        carry_ref[...] = jnp.zeros(carry_ref.shape, F32)

    @pl.when(j == 0)
    def _():
        _norm_mod(x_ref, h_ref, g_ref, sc_ref, sh_ref, 1, L)

    wa = wa_ref[...].astype(BF16)
    wg = wg_ref[...].astype(BF16)
    tail = jnp.where(t == 0, 0.0, carry_ref[j])
    for blk in range(nb):
        hb = h_ref[blk * rb:(blk + 1) * rb, :]
        a = _dot(hb, wa) + ba_ref[...]
        g = _dot(hb, wg) + bg_ref[...]
        u = a * jax.nn.sigmoid(g)
        ufull = jnp.concatenate([tail, u], axis=0)
        for cb in range(tn // LANES):
            gcols = pl.ds(pl.multiple_of(j * tn + cb * LANES, LANES), LANES)
            lcols = slice(cb * LANES, (cb + 1) * LANES)
            bias = bdw_ref[:, gcols]
            for r in range(rb // R):
                win = ufull[r * R:r * R + R + CONV_HIST, lcols]
                y_ref[blk * rb + r * R:blk * rb + (r + 1) * R, lcols] = _dwconv_block(win, wb_ref, bias, gcols, R)
        tail = u[rb - CONV_HIST:, :]
    carry_ref[j] = tail
    tail_ref[...] = tail


def _conv_mix(x, mod_head, g_norm, w_in, b_in, w_dw, b_dw, c_all, w_mod, b_mod, *, layer, tm, R, nb):
    Bp, T, _ = x.shape
    tn = MIX_CHUNK
    J = D_MODEL // tn
    nT = T // tm
    b2 = b_in.reshape(-1, 1, 2 * D_MODEL)
    rows = c_all.shape[0]
    per_vec = D_MODEL // MOD_BLOCK
    per_layer = 6 * per_vec
    n_mod = DEPTH * per_layer
    assert n_mod <= Bp * nT * J

    def mod_block(b, t, j):
        sb = jnp.minimum((b * nT + t) * J + j, n_mod - 1)
        return sb // per_layer, sb % per_layer

    def wm_map(b, t, j):
        layer_, cb = mod_block(b, t, j)
        return layer_, 0, cb

    def mod_out_map(b, t, j):
        layer_, cb = mod_block(b, t, j)
        return layer_, cb // per_vec, 0, cb % per_vec

    return pl.pallas_call(
        functools.partial(_conv_mix_kernel, L=tm, R=R, nb=nb, n_mod=n_mod),
        grid=(Bp, nT, J),
        in_specs=[_row_spec(tm), _vec_spec(4 * layer + 0), _mod_spec(1, layer, 1), _mod_spec(1, layer, 0),
                  pl.BlockSpec((None, D_MODEL, tn), lambda b, t, j: (0, 0, j)),
                  pl.BlockSpec((None, D_MODEL, tn), lambda b, t, j: (0, 0, j + J)),
                  pl.BlockSpec((None, 1, tn), lambda b, t, j: (0, 0, j)),
                  pl.BlockSpec((None, 1, tn), lambda b, t, j: (0, 0, j + J)),
                  pl.BlockSpec((None, CONV_K, D_MODEL), lambda b, t, j: (0, 0, 0)),
                  _vec_spec(0),
                  pl.BlockSpec((rows, D_MODEL), lambda b, t, j: (0, 0)),
                  pl.BlockSpec((None, D_MODEL, MOD_BLOCK), wm_map),
                  pl.BlockSpec((None, 1, MOD_BLOCK), wm_map)],
        out_specs=[pl.BlockSpec((None, tm, tn), lambda b, t, j: (b, t, j)),
                   pl.BlockSpec((None, None, CONV_HIST, tn), lambda b, t, j: (b, t, 0, j)),
                   pl.BlockSpec((None, None, rows, MOD_BLOCK), mod_out_map)],
        out_shape=[jax.ShapeDtypeStruct((Bp, T, D_MODEL), F32),
                   jax.ShapeDtypeStruct((Bp, nT, CONV_HIST, D_MODEL), F32),
                   jax.ShapeDtypeStruct((DEPTH, 6, rows, D_MODEL), F32)],
        scratch_shapes=[pltpu.VMEM((tm, D_MODEL), BF16),
                        pltpu.VMEM((CONV_K, SUBLANES, D_MODEL), F32),
                        pltpu.VMEM((J, CONV_HIST, tn), F32)],
        compiler_params=_cparams(3),
        name="conv_mix",
    )(x, g_norm, mod_head, mod_head, w_in, w_in, b2, b2, w_dw, b_dw.reshape(-1, 1, D_MODEL),
      c_all, w_mod, b_mod.reshape(DEPTH, 1, 6 * D_MODEL))


def _conv_proj_kernel(y_ref, lng_ref, lnb_ref, wout_ref, bout_ref, x_ref, g_ref, ga_ref, x1_ref, *, L):
    rb = L // PROJ_BLOCKS
    w = wout_ref[...]
    for blk in range(PROJ_BLOCKS):
        rows = slice(blk * rb, (blk + 1) * rb)
        a = _ln_silu_value(y_ref[rows, :], lng_ref, lnb_ref)
        m = _dot(a, w) + bout_ref[...]
        x1_ref[rows, :] = _residual_value(m, x_ref[rows, :], g_ref, ga_ref[0:1, :])


def _conv_proj(y, x, mod, g_norm, ln_g, ln_b, w_out, b_out, *, layer, tm):
    Bp, T, _ = x.shape
    vec = lambda a: a.reshape(-1, 1, D_MODEL)
    return pl.pallas_call(
        functools.partial(_conv_proj_kernel, L=tm),
        grid=(Bp, T // tm),
        in_specs=[_row_spec(tm), _vec_spec(0), _vec_spec(0),
                  pl.BlockSpec((None, D_MODEL, D_MODEL), lambda b, t: (0, 0, 0), pipeline_mode=pl.Buffered(1)),
                  _vec_spec(0), _row_spec(tm), _vec_spec(4 * layer + 1), _mod_spec(1, layer, 2)],
        out_specs=_row_spec(tm),
        out_shape=jax.ShapeDtypeStruct((Bp, T, D_MODEL), F32),
        compiler_params=_cparams(2),
        name="conv_proj",
    )(y, vec(ln_g), vec(ln_b), w_out, vec(b_out), x, g_norm, mod)


def _dwconv(full_ref, wb_ref, bdw_ref, y_ref, S, L, R):
    def body(cb, c):
        cols = pl.ds(pl.multiple_of(cb * LANES, LANES), LANES)
        bias = bdw_ref[:, cols]
        for s in range(S):
            for r in range(L // R):
                win = full_ref[s, r * R:r * R + R + CONV_HIST, cols]
                y_ref[s * L + r * R:s * L + (r + 1) * R, cols] = _dwconv_block(win, wb_ref, bias, cols, R)
        return c

    lax.fori_loop(0, D_MODEL // LANES, body, 0)


def _ln_silu_value(y, lng_ref, lnb_ref):
    mu = jnp.mean(y, axis=-1, keepdims=True)
    yc = y - mu
    var = jnp.mean(yc * yc, axis=-1, keepdims=True)
    z = yc * lax.rsqrt(var + EPS) * lng_ref[...] + lnb_ref[...]
    return z * jax.nn.sigmoid(z)


def _ln_silu(y_ref, lng_ref, lnb_ref, S, L):
    _row_loop(S, L, [y_ref], y_ref, lambda s, y: _ln_silu_value(y, lng_ref, lnb_ref))


def _conv_out_kernel(u_ref, hist_ref, wdw_ref, bdw_ref, lng_ref, lnb_ref, wout_ref, bout_ref, x_ref, g_ref, ga_ref,
                     x1_ref, full_ref, wb_ref, y_ref, *, S, L, R):
    _fill_tap_table(wdw_ref, wb_ref)
    pad = CONV_HIST - (CONV_K - 1)
    full_ref[:, 0:pad, :] = jnp.zeros((S, pad, D_MODEL), F32)
    full_ref[:, pad:CONV_HIST, :] = hist_ref[...]
    full_ref[:, CONV_HIST:CONV_HIST + L, :] = u_ref[...].reshape(S, L, D_MODEL)
    _dwconv(full_ref, wb_ref, bdw_ref, y_ref, S, L, R)
    _ln_silu(y_ref, lng_ref, lnb_ref, S, L)
    x1_ref[...] = _dot(y_ref[...], wout_ref[...]) + bout_ref[...]
    _residual(x1_ref, x_ref, g_ref, ga_ref, x1_ref, S, L)


def _conv_out(u, hist, x, mod, g_norm, w_dw, b_dw, ln_g, ln_b, w_out, b_out, *, layer, S, L, tm, R):
    Bp, T, _ = x.shape
    assert Bp == 1 and T == tm
    vec = lambda a: a.reshape(-1, 1, D_MODEL)
    return pl.pallas_call(
        functools.partial(_conv_out_kernel, S=S, L=L, R=R),
        grid=(Bp, T // tm),
        in_specs=[_row_spec(tm), pl.BlockSpec((S, CONV_K - 1, D_MODEL), lambda b, t: (0, 0, 0)),
                  pl.BlockSpec((None, CONV_K, D_MODEL), lambda b, t: (0, 0, 0)),
                  _vec_spec(0), _vec_spec(0), _vec_spec(0),
                  pl.BlockSpec((None, D_MODEL, D_MODEL), lambda b, t: (0, 0, 0), pipeline_mode=pl.Buffered(1)),
                  _vec_spec(0), _row_spec(tm), _vec_spec(4 * layer + 1), _mod_spec(S, layer, 2)],
        out_specs=_row_spec(tm),
        out_shape=jax.ShapeDtypeStruct((Bp, T, D_MODEL), F32),
        scratch_shapes=[pltpu.VMEM((S, CONV_HIST + L, D_MODEL), F32),
                        pltpu.VMEM((CONV_K, SUBLANES, D_MODEL), F32),
                        pltpu.VMEM((tm, D_MODEL), F32)],
        compiler_params=_cparams(2),
        name="conv_out",
    )(u, hist, w_dw, vec(b_dw), vec(ln_g), vec(ln_b), w_out, vec(b_out), x, g_norm, mod)


def _ffn_kernel(*refs, S, L, nb, prompt):
    if prompt:
        (x_ref, g2_ref, sc_ref, sh_ref, wg_ref, wv_ref, wdw_ref, bdw_ref, wd_ref, g3_ref, ga_ref,
         x2_ref, st_ref, h_ref, carry_ref) = refs
    else:
        (x_ref, g2_ref, sc_ref, sh_ref, wg_ref, wv_ref, wdw_ref, bdw_ref, wd_ref, g3_ref, ga_ref, hist_ref,
         x2_ref, st_ref, h_ref) = refs
    t = pl.program_id(1)
    j = pl.program_id(2)
    last_j = pl.num_programs(2) - 1
    tm = x_ref.shape[0]
    rb = tm // nb
    seg = min(L, rb)
    pad = SUBLANES
    inline = S == 1

    mm_dtype = h_ref.dtype

    def step(first, last):
        wg = wg_ref[...].astype(mm_dtype)
        wv = wv_ref[...].astype(mm_dtype)
        wd = wd_ref[...].astype(mm_dtype)
        w0 = wdw_ref[0:1, :]
        w1 = wdw_ref[1:2, :]
        w2 = wdw_ref[2:3, :]
        bd = bdw_ref[...]
        tail = None
        for blk in range(nb):
            rows = slice(blk * rb, (blk + 1) * rb)
            if first:
                x = x_ref[rows, :]
                inv = lax.rsqrt(jnp.mean(x * x, axis=-1, keepdims=True) + EPS)
                hb = (x * inv * (g2_ref[...] * (1.0 + sc_ref[0:1, :])) + sh_ref[0:1, :]).astype(mm_dtype)
                h_ref[rows, :] = hb
            else:
                hb = h_ref[rows, :]
            g = _dot(hb, wg)
            v = _dot(hb, wv)
            p1s, p2s = [], []
            for q in range(rb // seg):
                gq = g[q * seg:(q + 1) * seg, :]
                if not prompt:
                    s = blk * (rb // seg) + q
                    hist = hist_ref[s]
                    st_ref[s] = gq[seg - (FFN_K - 1):, :]
                elif blk == 0:
                    hist = jnp.where(t == 0, 0.0, carry_ref[j])
                else:
                    hist = tail
                gfull = jnp.concatenate([hist, gq], axis=0)
                p1s.append(pltpu.roll(gfull, 1, axis=0)[pad:, :])
                p2s.append(pltpu.roll(gfull, 2, axis=0)[pad:, :])
                tail = gq[seg - pad:, :]
            p1 = p1s[0] if len(p1s) == 1 else jnp.concatenate(p1s, axis=0)
            p2 = p2s[0] if len(p2s) == 1 else jnp.concatenate(p2s, axis=0)
            gc = g * w2 + p1 * w1 + p2 * w0 + bd
            act = (jax.nn.gelu(gc) * v).astype(mm_dtype)
            acc = _dot(act, wd)
            if not first:
                acc = x2_ref[rows, :] + acc
            if last:
                acc = _residual_value(acc, x_ref[rows, :], g3_ref, ga_ref[0:1, :])
            x2_ref[rows, :] = acc
        if prompt:
            carry_ref[j] = tail
            st_ref[0] = tail[pad - (FFN_K - 1):, :]

    if inline:
        pl.when(j == 0)(lambda: step(True, False))
        pl.when((j > 0) & (j < last_j))(lambda: step(False, False))
        pl.when(j == last_j)(lambda: step(False, True))
    else:
        @pl.when(j == 0)
        def _():
            _norm_mod(x_ref, h_ref, g2_ref, sc_ref, sh_ref, S, L)
            x2_ref[...] = jnp.zeros(x2_ref.shape, F32)

        step(False, False)

        @pl.when(j == last_j)
        def _():
            _residual(x2_ref, x_ref, g3_ref, ga_ref, x2_ref, S, L)


def _ffn(x, hist, mod, g_norm, w_up, w_dw, b_dw, w_down, *, layer, S, L, tm, tf, nb):
    Bp, T, _ = x.shape
    J = D_FF // tf
    nT = T // tm
    prompt = hist is None
    x_spec = pl.BlockSpec((None, tm, D_MODEL), lambda b, t, j: (b, t, 0), pipeline_mode=pl.Buffered(1))
    in_specs = [x_spec, _vec_spec(4 * layer + 2), _mod_spec(S, layer, 4), _mod_spec(S, layer, 3),
                pl.BlockSpec((None, D_MODEL, tf), lambda b, t, j: (layer, 0, j)),
                pl.BlockSpec((None, D_MODEL, tf), lambda b, t, j: (layer, 0, j + J)),
                pl.BlockSpec((None, FFN_K, tf), lambda b, t, j: (layer, 0, j)),
                pl.BlockSpec((None, 1, tf), lambda b, t, j: (layer, 0, j)),
                pl.BlockSpec((None, tf, D_MODEL), lambda b, t, j: (layer, j, 0)),
                _vec_spec(4 * layer + 3), _mod_spec(S, layer, 5)]
    args = [x, g_norm, mod, mod, w_up, w_up, w_dw, b_dw.reshape(DEPTH, 1, D_FF), w_down, g_norm, mod]
    scratch = [pltpu.VMEM((tm, D_MODEL), BF16 if prompt else F32)]
    if prompt:
        scratch.append(pltpu.VMEM((J, SUBLANES, tf), F32))
    else:
        in_specs.append(pl.BlockSpec((S, SUBLANES, tf), lambda b, t, j: (0, 0, j)))
        args.append(hist)
    return pl.pallas_call(
        functools.partial(_ffn_kernel, S=S, L=L, nb=nb, prompt=prompt),
        grid=(Bp, nT, J),
        in_specs=in_specs,
        out_specs=[_row_spec(tm),
                   pl.BlockSpec((None, None, S, FFN_K - 1, tf), lambda b, t, j: (b, t, 0, 0, j))],
        out_shape=[jax.ShapeDtypeStruct((Bp, T, D_MODEL), F32),
                   jax.ShapeDtypeStruct((Bp, nT, S, FFN_K - 1, D_FF), F32)],
        scratch_shapes=scratch,
        compiler_params=pltpu.CompilerParams(dimension_semantics=("arbitrary",) * 3,
                                             vmem_limit_bytes=VMEM_LIMIT_FFN),
        name="ffn",
    )(*args)


def _qkv_kernel(x_ref, g_ref, sc_ref, sh_ref, wq_ref, wk_ref, wv_ref, q_ref, k_ref, v_ref, h_ref, *, S, L):
    tm = x_ref.shape[0]
    if S > 1:
        _norm_mod(x_ref, h_ref, g_ref, sc_ref, sh_ref, S, L)
        blocks = [(slice(0, tm), h_ref[...])]
    else:
        rb = tm // PROJ_BLOCKS
        mul = g_ref[...] * (1.0 + sc_ref[0:1, :])
        blocks = []
        for blk in range(PROJ_BLOCKS):
            rows = slice(blk * rb, (blk + 1) * rb)
            x = x_ref[rows, :]
            inv = lax.rsqrt(jnp.mean(x * x, axis=-1, keepdims=True) + EPS)
            blocks.append((rows, x * inv * mul + sh_ref[0:1, :]))
    wq = wq_ref[...]
    wk = wk_ref[...]
    wv = wv_ref[...]
    for rows, h in blocks:
        q_ref[rows, :] = _dot(h, wq)
        k_ref[rows, :] = _dot(h, wk)
        v_ref[rows, :] = _dot(h, wv)


def _qkv(x, mod, g_norm, w_q, w_k, w_v, *, layer, S, L, tm):
    Bp, T, _ = x.shape
    kv_spec = pl.BlockSpec((None, tm, KV_DIM), lambda b, t: (b, t, 0))
    resident = lambda n: pl.BlockSpec((None, D_MODEL, n), lambda b, t: (0, 0, 0), pipeline_mode=pl.Buffered(1))
    return pl.pallas_call(
        functools.partial(_qkv_kernel, S=S, L=L),
        grid=(Bp, T // tm),
        in_specs=[_row_spec(tm), _vec_spec(4 * layer + 0), _mod_spec(S, layer, 1), _mod_spec(S, layer, 0),
                  resident(D_MODEL), resident(KV_DIM), resident(KV_DIM)],
        out_specs=[_row_spec(tm), kv_spec, kv_spec],
        out_shape=[jax.ShapeDtypeStruct((Bp, T, D_MODEL), F32),
                   jax.ShapeDtypeStruct((Bp, T, KV_DIM), F32),
                   jax.ShapeDtypeStruct((Bp, T, KV_DIM), F32)],
        scratch_shapes=[pltpu.VMEM((tm, D_MODEL), F32)],
        compiler_params=_cparams(2),
        name="qkv",
    )(x, g_norm, mod, mod, w_q, w_k, w_v)


REL_SPAN = 3 * LANES


def _bias_kernel(rb_ref, prof_ref, valid_ref, o_ref, *, Qn):
    prof = prof_ref[...]
    valid = valid_ref[...] != 0
    early = lax.broadcasted_iota(jnp.int32, valid.shape, 0) < WINDOW
    for hh in range(N_HEADS):
        f = jnp.zeros(prof.shape, F32)
        for b in range(N_BUCKETS):
            f = jnp.where(prof == b, rb_ref[b, hh], f)
        h, g = divmod(hh, GROUP)
        off = (g * Qn) % LANES
        x = jnp.broadcast_to(f[0:1, :], (KEY_WIN, REL_SPAN))
        r = pltpu.roll(x, WINDOW + off, axis=1, stride=1, stride_axis=0)
        tab = jnp.where(valid, r[:, off:off + Qn], NEG_INF)
        o_ref[0, h, :, g * Qn:(g + 1) * Qn] = tab
        if o_ref.shape[0] > 1:
            o_ref[1, h, :, g * Qn:(g + 1) * Qn] = jnp.where(early, NEG_INF, tab)


def _t5_bucket(rel):
    half = N_BUCKETS // 2
    max_exact = half // 2
    n = jnp.abs(rel)
    ret = jnp.where(rel > 0, half, 0)
    nf = jnp.maximum(n, 1).astype(F32)
    large = max_exact + (jnp.log(nf / max_exact) / math.log(MAX_DISTANCE / max_exact)
                         * (half - max_exact)).astype(jnp.int32)
    large = jnp.minimum(large, half - 1)
    return ret + jnp.where(n < max_exact, n, large)


def _bias_table(rel_bias, Qn, lo, hi, variants):
    assert KEY_WIN + Qn - 1 <= REL_SPAN
    k = jnp.arange(KEY_WIN)[:, None]
    q = jnp.arange(Qn)[None, :]
    valid = ((k >= lo(q)) & (k < hi(q))).astype(jnp.int32)
    prof = jnp.broadcast_to(_t5_bucket(WINDOW - jnp.arange(REL_SPAN)).astype(jnp.int32)[None, :], (SUBLANES, REL_SPAN))
    return pl.pallas_call(
        functools.partial(_bias_kernel, Qn=Qn),
        in_specs=[pl.BlockSpec(memory_space=pltpu.SMEM),
                  pl.BlockSpec((SUBLANES, REL_SPAN), lambda: (0, 0)),
                  pl.BlockSpec((KEY_WIN, Qn), lambda: (0, 0))],
        out_specs=pl.BlockSpec((variants, N_KV_HEADS, KEY_WIN, GROUP * Qn), lambda: (0, 0, 0, 0)),
        out_shape=jax.ShapeDtypeStruct((variants, N_KV_HEADS, KEY_WIN, GROUP * Qn), F32),
        name="rel_bias_table",
    )(rel_bias, prof, valid)


def _attend(q_ref, r0, Qn, kw, vw, bias_ref, sel, sink_ref, o_ref, qs_ref):
    rows = pl.ds(r0, Qn)
    for g in range(GROUP):
        for h in range(N_KV_HEADS):
            src = (h * GROUP + g) * HEAD_DIM
            qs_ref[g * Qn:(g + 1) * Qn, h * HEAD_DIM:(h + 1) * HEAD_DIM] = q_ref[rows, src:src + HEAD_DIM] * ATTN_SCALE
    qt = qs_ref[...].T
    vt = vw.T
    outs = []
    for h in range(N_KV_HEADS):
        hd = slice(h * HEAD_DIM, (h + 1) * HEAD_DIM)
        s = _dot(kw[:, hd], qt[hd, :]) + bias_ref[sel, h]
        sk = sink_ref[h]
        mx = jnp.maximum(jnp.max(s, axis=0, keepdims=True), sk)
        p = jnp.exp(s - mx)
        den = jnp.sum(p, axis=0, keepdims=True) + jnp.exp(sk - mx)
        outs.append(_dot(vt[hd, :], p) * (1.0 / den))
    o2 = jnp.concatenate(outs, axis=0).T
    for g in range(GROUP):
        for h in range(N_KV_HEADS):
            dst = (h * GROUP + g) * HEAD_DIM
            o_ref[rows, dst:dst + HEAD_DIM] = o2[g * Qn:(g + 1) * Qn, h * HEAD_DIM:(h + 1) * HEAD_DIM]


def _attn_prompt_kernel(q_ref, kp_ref, kc_ref, vp_ref, vc_ref, bias_ref, sink_ref, o_ref, kw_ref, vw_ref, qs_ref,
                        *, tq, Qn):
    t = pl.program_id(1)
    kw_ref[0:WINDOW, :] = kp_ref[...]
    kw_ref[WINDOW:WINDOW + tq, :] = kc_ref[...]
    vw_ref[0:WINDOW, :] = vp_ref[...]
    vw_ref[WINDOW:WINDOW + tq, :] = vc_ref[...]

    def body(mp, carry):
        for u in range(ATTN_UNITS_PER_ITER):
            m = mp * ATTN_UNITS_PER_ITER + u
            r0 = pl.multiple_of(m * Qn, Qn)
            kw = kw_ref[pl.ds(r0, KEY_WIN), :]
            vw = vw_ref[pl.ds(r0, KEY_WIN), :]
            sel = ((t == 0) & (m == 0)).astype(jnp.int32)
            _attend(q_ref, r0, Qn, kw, vw, bias_ref, sel, sink_ref, o_ref, qs_ref.at[u])
        return carry

    lax.fori_loop(0, tq // (Qn * ATTN_UNITS_PER_ITER), body, 0)


def _attn_sample_kernel(q_ref, ck_ref, kn_ref, cv_ref, vn_ref, bias_ref, sink_ref, o_ref, kw_ref, vw_ref, qs_ref,
                        *, S, L):
    P = ck_ref.shape[1]
    kw_ref[:, 0:P, :] = ck_ref[...]
    kw_ref[:, P:P + L, :] = kn_ref[...].reshape(S, L, KV_DIM)
    kw_ref[:, P + L:, :] = jnp.zeros((S, KEY_WIN - P - L, KV_DIM), F32)
    vw_ref[:, 0:P, :] = cv_ref[...]
    vw_ref[:, P:P + L, :] = vn_ref[...].reshape(S, L, KV_DIM)
    vw_ref[:, P + L:, :] = jnp.zeros((S, KEY_WIN - P - L, KV_DIM), F32)

    def body(sp, carry):
        for u in range(ATTN_UNITS_PER_ITER):
            s = sp * ATTN_UNITS_PER_ITER + u
            r0 = pl.multiple_of(s * L, L)
            _attend(q_ref, r0, L, kw_ref[s], vw_ref[s], bias_ref, 0, sink_ref, o_ref, qs_ref.at[u])
        return carry

    lax.fori_loop(0, S // ATTN_UNITS_PER_ITER, body, 0)


def _attn_prompt(q, k, v, bias, sink, *, tq):
    Bp, T, _ = q.shape
    Qn = 2 * CHUNK
    per = tq // WINDOW
    prev_spec = pl.BlockSpec((None, WINDOW, KV_DIM), lambda b, t: (b, jnp.maximum(t * per - 1, 0), 0))
    cur_spec = pl.BlockSpec((None, tq, KV_DIM), lambda b, t: (b, t, 0))
    return pl.pallas_call(
        functools.partial(_attn_prompt_kernel, tq=tq, Qn=Qn),
        grid=(Bp, T // tq),
        in_specs=[_row_spec(tq), prev_spec, cur_spec, prev_spec, cur_spec,
                  pl.BlockSpec((2, N_KV_HEADS, KEY_WIN, GROUP * Qn), lambda b, t: (0, 0, 0, 0),
                               pipeline_mode=pl.Buffered(1)),
                  pl.BlockSpec((N_KV_HEADS, 1, GROUP * Qn), lambda b, t: (0, 0, 0))],
        out_specs=_row_spec(tq),
        out_shape=jax.ShapeDtypeStruct((Bp, T, D_MODEL), F32),
        scratch_shapes=[pltpu.VMEM((WINDOW + tq, KV_DIM), F32), pltpu.VMEM((WINDOW + tq, KV_DIM), F32),
                        pltpu.VMEM((ATTN_UNITS_PER_ITER, GROUP * Qn, KV_DIM), F32)],
        compiler_params=_cparams(2),
        name="attn_prompt",
    )(q, k, k, v, v, bias, sink)


def _attn_sample(q, k, v, cache_k, cache_v, bias, sink, *, S, L):
    return pl.pallas_call(
        functools.partial(_attn_sample_kernel, S=S, L=L),
        out_shape=jax.ShapeDtypeStruct((S * L, D_MODEL), F32),
        scratch_shapes=[pltpu.VMEM((S, KEY_WIN, KV_DIM), F32), pltpu.VMEM((S, KEY_WIN, KV_DIM), F32),
                        pltpu.VMEM((ATTN_UNITS_PER_ITER, GROUP * L, KV_DIM), F32)],
        compiler_params=pltpu.CompilerParams(vmem_limit_bytes=VMEM_LIMIT),
        name="attn_sample",
    )(q, cache_k, k, cache_v, v, bias, sink)


def _residual_value(m, x, g_ref, ga):
    inv = lax.rsqrt(jnp.mean(m * m, axis=-1, keepdims=True) + EPS)
    return x + m * inv * (g_ref[...] * ga)


def _attn_out_kernel(o_ref, w_ref, x_ref, g_ref, ga_ref, x1_ref, *, S, L):
    tm = o_ref.shape[0]
    if S > 1:
        x1_ref[...] = _dot(o_ref[...], w_ref[...])
        _residual(x1_ref, x_ref, g_ref, ga_ref, x1_ref, S, L)
        return
    rb = tm // PROJ_BLOCKS
    w = w_ref[...]
    for blk in range(PROJ_BLOCKS):
        rows = slice(blk * rb, (blk + 1) * rb)
        x1_ref[rows, :] = _residual_value(_dot(o_ref[rows, :], w), x_ref[rows, :], g_ref, ga_ref[0:1, :])


def _attn_out(o, x, mod, g_norm, w_o, *, layer, S, L, tm):
    Bp, T, _ = x.shape
    return pl.pallas_call(
        functools.partial(_attn_out_kernel, S=S, L=L),
        grid=(Bp, T // tm),
        in_specs=[_row_spec(tm),
                  pl.BlockSpec((None, D_MODEL, D_MODEL), lambda b, t: (0, 0, 0), pipeline_mode=pl.Buffered(1)),
                  _row_spec(tm), _vec_spec(4 * layer + 1), _mod_spec(S, layer, 2)],
        out_specs=_row_spec(tm),
        out_shape=jax.ShapeDtypeStruct((Bp, T, D_MODEL), F32),
        compiler_params=_cparams(2),
        name="attn_out",
    )(o, w_o, x, g_norm, mod)


def kernel(x_prompt, x_sample, c_prompt, c_sample, cache_conv, cache_k, cache_v, cache_ffn, w_mod, b_mod, g_norm, conv_w_in, conv_b_in, conv_w_dw, conv_b_dw, conv_ln_g, conv_ln_b, conv_w_out, conv_b_out, attn_w_q, attn_w_k, attn_w_v, attn_w_o, attn_sinks, rel_bias, ffn_w_up, ffn_w_dw, ffn_b_dw, ffn_w_down):
    B, T, D = x_prompt.shape
    SB, SL, _ = x_sample.shape
    assert SB == MOD_ROWS_SAMPLE
    c_all = jnp.concatenate(
        [c_sample, jnp.pad(c_prompt[:, None, :], ((0, 0), (0, SUBLANES - 1), (0, 0))).reshape(B * SUBLANES, D)], axis=0)
    mod_head = _modulation_head(c_all, w_mod, b_mod)
    gn = g_norm.reshape(DEPTH * 4, 1, D)

    geo_p = dict(S=1, L=STREAM_ROWS, tm=STREAM_ROWS)
    geo_p_small = dict(S=1, L=RESIDENT_ROWS, tm=RESIDENT_ROWS)
    geo_s = dict(S=SB, L=SL, tm=SB * SL)

    xp = x_prompt
    xs = x_sample.reshape(1, SB * SL, D)

    yp, tail_p, mod = _conv_mix(xp, mod_head, gn, conv_w_in, conv_b_in, conv_w_dw, conv_b_dw, c_all, w_mod, b_mod,
                                layer=0, tm=STREAM_ROWS, R=MIX_CONV_ROWS, nb=MIX_ROW_BLOCKS)
    xp = _conv_proj(yp, xp, mod, gn, conv_ln_g, conv_ln_b, conv_w_out, conv_b_out, layer=0, tm=RESIDENT_ROWS)
    us = _conv_in(xs, mod_head, gn, conv_w_in, conv_b_in, layer=0, **geo_s)
    conv_args = (conv_w_dw, conv_b_dw, conv_ln_g, conv_ln_b, conv_w_out, conv_b_out)
    xs = _conv_out(us, cache_conv[0], xs, mod, gn, *conv_args, layer=0, R=SL, **geo_s)
    conv_state_p = tail_p[:, -1, CONV_HIST - (CONV_K - 1):, :][None]
    conv_state_s = us.reshape(SB, SL, D)[:, SL - (CONV_K - 1):, :][None]

    ffn_p, ffn_s = [], []
    ffn_hist = jnp.pad(cache_ffn, ((0, 0), (0, 0), (SUBLANES - (FFN_K - 1), 0), (0, 0)))

    def run_ffn(i, xp, xs):
        w = (ffn_w_up, ffn_w_dw, ffn_b_dw, ffn_w_down)
        xp, st_p = _ffn(xp, None, mod, gn, *w, layer=i, tf=FFN_CHUNK, nb=FFN_ROW_BLOCKS, **geo_p)
        xs, st_s = _ffn(xs, ffn_hist[i], mod, gn, *w, layer=i, tf=FFN_CHUNK, nb=FFN_ROW_BLOCKS, **geo_s)
        ffn_p.append(st_p[:, -1, 0])
        ffn_s.append(st_s[0, 0])
        return xp, xs

    xp, xs = run_ffn(0, xp, xs)

    qp, kp, vp = _qkv(xp, mod, gn, attn_w_q, attn_w_k, attn_w_v, layer=1, **geo_p_small)
    qs, ks, vs = _qkv(xs, mod, gn, attn_w_q, attn_w_k, attn_w_v, layer=1, **geo_s)
    P = cache_k.shape[2]
    assert P == WINDOW and P + SL <= KEY_WIN
    bias_p = _bias_table(rel_bias, 2 * CHUNK,
                         lambda q: (q // CHUNK) * CHUNK, lambda q: (q // CHUNK) * CHUNK + WINDOW + CHUNK, 2)
    bias_s = _bias_table(rel_bias, SL, lambda q: 0 * q, lambda q: 0 * q + P + SL, 1)
    sink_p = jnp.repeat(attn_sinks[0], 2 * CHUNK).reshape(N_KV_HEADS, 1, GROUP * 2 * CHUNK)
    sink_s = jnp.repeat(attn_sinks[0], SL).reshape(N_KV_HEADS, 1, GROUP * SL)
    op = _attn_prompt(qp, kp, vp, bias_p, sink_p, tq=RESIDENT_ROWS)
    ck = cache_k[0].reshape(SB, P, KV_DIM)
    cv = cache_v[0].reshape(SB, P, KV_DIM)
    os_ = _attn_sample(qs[0], ks[0], vs[0], ck, cv, bias_s, sink_s, S=SB, L=SL)[None]
    xp = _attn_out(op, xp, mod, gn, attn_w_o, layer=1, **geo_p_small)
    xs = _attn_out(os_, xs, mod, gn, attn_w_o, layer=1, **geo_s)
    keep = min(WINDOW, T)
    k_state_p = kp[:, T - keep:].reshape(B, keep, N_KV_HEADS, HEAD_DIM)[None]
    v_state_p = vp[:, T - keep:].reshape(B, keep, N_KV_HEADS, HEAD_DIM)[None]
    k_state_s = jnp.concatenate([ck, ks.reshape(SB, SL, KV_DIM)], axis=1)[:, SL:]
    v_state_s = jnp.concatenate([cv, vs.reshape(SB, SL, KV_DIM)], axis=1)[:, SL:]
    k_state_s = k_state_s.reshape(SB, P, N_KV_HEADS, HEAD_DIM)[None]
    v_state_s = v_state_s.reshape(SB, P, N_KV_HEADS, HEAD_DIM)[None]

    xp, xs = run_ffn(1, xp, xs)

    return (xp, xs.reshape(SB, SL, D), conv_state_p, conv_state_s,
            k_state_p, v_state_p, k_state_s, v_state_s,
            jnp.stack(ffn_p), jnp.stack(ffn_s))
```

```python
import functools
import math

import jax
import jax.numpy as jnp
from jax import lax
from jax.experimental import pallas as pl
from jax.experimental.pallas import tpu as pltpu

F32 = jnp.float32
BF16 = jnp.bfloat16

D_MODEL = 2048
D_FF = 5632
DEPTH = 2
CONV_K = 31
FFN_K = 3
N_HEADS = 32
N_KV_HEADS = 4
GROUP = N_HEADS // N_KV_HEADS
HEAD_DIM = 64
KV_DIM = N_KV_HEADS * HEAD_DIM
CHUNK = 64
WINDOW = 128
N_BUCKETS = 32
MAX_DISTANCE = 128
EPS = 1e-6
NEG_INF = -1e30
ATTN_SCALE = HEAD_DIM ** -0.5

SUBLANES = 8
LANES = 128
CONV_HIST = 32
KEY_WIN = 2 * WINDOW
VMEM_LIMIT = 58 * 1024 * 1024
VMEM_LIMIT_FFN = 62 * 1024 * 1024
MOD_ROWS_SAMPLE = 16


def _cparams(n_grid):
    return pltpu.CompilerParams(dimension_semantics=("arbitrary",) * n_grid,
                                vmem_limit_bytes=VMEM_LIMIT)


def _dot(a, b):
    return jnp.dot(a, b, preferred_element_type=F32)


ROW_SLAB = 64
ROW_GROUP = 2
PROJ_BLOCKS = 2

STREAM_ROWS = 1024
RESIDENT_ROWS = 512
FFN_CHUNK = 512
FFN_ROW_BLOCKS = 1
MIX_CHUNK = 256
MIX_ROW_BLOCKS = 2
MIX_CONV_ROWS = 64
IN_PROJ_CHUNK = 512
MOD_HEAD_CHUNK = 1024
ATTN_ROWS = 1024
ATTN_UNITS_PER_ITER = 4


def _row_loop(S, L, srcs, dst, fn):
    slab = min(L, ROW_SLAB)
    n = L // slab
    group = ROW_GROUP if n % ROW_GROUP == 0 else 1
    for s in range(S):

        def body(r, c, s=s):
            rows = [pl.ds(pl.multiple_of(s * L + (r * group + i) * slab, slab), slab) for i in range(group)]
            vals = [[src[rw, :] for src in srcs] for rw in rows]
            outs = [fn(s, *v) for v in vals]
            for rw, out in zip(rows, outs):
                dst[rw, :] = out.astype(dst.dtype)
            return c

        lax.fori_loop(0, n // group, body, 0)


def _norm_mod(x_ref, dst_ref, g_ref, sc_ref, sh_ref, S, L):
    def fn(s, x):
        inv = lax.rsqrt(jnp.mean(x * x, axis=-1, keepdims=True) + EPS)
        mul = g_ref[...] * (1.0 + sc_ref[s:s + 1, :])
        return x * inv * mul + sh_ref[s:s + 1, :]

    _row_loop(S, L, [x_ref], dst_ref, fn)


def _residual(m_ref, x_ref, g_ref, ga_ref, out_ref, S, L):
    def fn(s, m, x):
        inv = lax.rsqrt(jnp.mean(m * m, axis=-1, keepdims=True) + EPS)
        return x + m * inv * (g_ref[...] * ga_ref[s:s + 1, :])

    _row_loop(S, L, [m_ref, x_ref], out_ref, fn)


def _row_spec(tm):
    return pl.BlockSpec((None, tm, D_MODEL), lambda b, t, *_: (b, t, 0))


def _vec_spec(index, n=D_MODEL):
    return pl.BlockSpec((None, 1, n), lambda *_: (index, 0, 0))


def _mod_spec(S, layer, which):
    if S == 1:
        first = MOD_ROWS_SAMPLE // SUBLANES
        return pl.BlockSpec((None, None, SUBLANES, D_MODEL), lambda b, t, *_: (layer, which, first + b, 0))
    return pl.BlockSpec((None, None, S, D_MODEL), lambda b, t, *_: (layer, which, 0, 0))


def _mod_kernel(c_ref, w_ref, b_ref, o_ref):
    c = c_ref[...]
    o_ref[...] = _dot(c * jax.nn.sigmoid(c), w_ref[...]) + b_ref[...]


MOD_HEAD_VECS = 2
MOD_BLOCK = 512


def _modulation_head(c_all, w_mod, b_mod):
    rows = c_all.shape[0]
    tn = MOD_HEAD_CHUNK
    per = D_MODEL // tn
    return pl.pallas_call(
        _mod_kernel,
        grid=(MOD_HEAD_VECS * per,),
        in_specs=[pl.BlockSpec((rows, D_MODEL), lambda j: (0, 0)),
                  pl.BlockSpec((None, D_MODEL, tn), lambda j: (0, 0, j)),
                  pl.BlockSpec((None, 1, tn), lambda j: (0, 0, j))],
        out_specs=pl.BlockSpec((None, None, rows, tn), lambda j: (0, j // per, 0, j % per)),
        out_shape=jax.ShapeDtypeStruct((1, MOD_HEAD_VECS, rows, D_MODEL), F32),
        compiler_params=_cparams(1),
        name="modulation_head",
    )(c_all, w_mod, b_mod.reshape(DEPTH, 1, 6 * D_MODEL))


def _conv_in_kernel(x_ref, g_ref, sc_ref, sh_ref, wa_ref, wg_ref, ba_ref, bg_ref, u_ref, h_ref, *, S, L):
    @pl.when(pl.program_id(2) == 0)
    def _():
        _norm_mod(x_ref, h_ref, g_ref, sc_ref, sh_ref, S, L)

    h = h_ref[...]
    a = _dot(h, wa_ref[...]) + ba_ref[...]
    g = _dot(h, wg_ref[...]) + bg_ref[...]
    u_ref[...] = a * jax.nn.sigmoid(g)


def _conv_in(x, mod, g_norm, w_in, b_in, *, layer, S, L, tm):
    Bp, T, _ = x.shape
    tn = IN_PROJ_CHUNK
    J = D_MODEL // tn
    b2 = b_in.reshape(-1, 1, 2 * D_MODEL)
    return pl.pallas_call(
        functools.partial(_conv_in_kernel, S=S, L=L),
        grid=(Bp, T // tm, J),
        in_specs=[_row_spec(tm), _vec_spec(4 * layer + 0), _mod_spec(S, layer, 1), _mod_spec(S, layer, 0),
                  pl.BlockSpec((None, D_MODEL, tn), lambda b, t, j: (0, 0, j)),
                  pl.BlockSpec((None, D_MODEL, tn), lambda b, t, j: (0, 0, j + J)),
                  pl.BlockSpec((None, 1, tn), lambda b, t, j: (0, 0, j)),
                  pl.BlockSpec((None, 1, tn), lambda b, t, j: (0, 0, j + J))],
        out_specs=pl.BlockSpec((None, tm, tn), lambda b, t, j: (b, t, j)),
        out_shape=jax.ShapeDtypeStruct((Bp, T, D_MODEL), F32),
        scratch_shapes=[pltpu.VMEM((tm, D_MODEL), F32)],
        compiler_params=_cparams(3),
        name="conv_in",
    )(x, g_norm, mod, mod, w_in, w_in, b2, b2)


def _dwconv_block(win, wb_ref, bias, cols, R):
    n_groups = (CONV_K + 1 + SUBLANES) // SUBLANES
    win_rows = R + CONV_HIST
    acc = jnp.broadcast_to(bias, (R // SUBLANES, SUBLANES, LANES))
    for sh in range(SUBLANES):
        wsh = win if sh == 0 else pltpu.roll(win, win_rows - sh, axis=0)
        for a in range(n_groups):
            k = SUBLANES * a + sh - 2
            if 0 <= k < CONV_K:
                tap = wsh[SUBLANES * a:SUBLANES * a + R, :]
                acc = acc + tap.reshape(R // SUBLANES, SUBLANES, LANES) * wb_ref[k, :, cols]
    return acc.reshape(R, LANES)


def _fill_tap_table(wdw_ref, wb_ref):
    for k in range(CONV_K):
        wb_ref[k] = jnp.broadcast_to(wdw_ref[k:k + 1, :], (SUBLANES, D_MODEL))


def _conv_mix_kernel(x_ref, g_ref, sc_ref, sh_ref, wa_ref, wg_ref, ba_ref, bg_ref, wdw_ref, bdw_ref,
                     c_ref, wm_ref, bm_ref,
                     y_ref, tail_ref, mod_ref, h_ref, wb_ref, carry_ref, *, L, R, nb, n_mod):
    t = pl.program_id(1)
    j = pl.program_id(2)
    tn = wa_ref.shape[1]
    rb = L // nb
    step = (pl.program_id(0) * pl.num_programs(1) + t) * pl.num_programs(2) + j

    @pl.when(step < n_mod)
    def _():
        c = c_ref[...]
        mod_ref[...] = _dot(c * jax.nn.sigmoid(c), wm_ref[...]) + bm_ref[...]

    @pl.when((pl.program_id(0) == 0) & (t == 0) & (j == 0))
    def _():
        _fill_tap_table(wdw_ref, wb_ref)
        carry_ref[...] = jnp.zeros(carry_ref.shape, F32)

    @pl.when(j == 0)
    def _():
        _norm_mod(x_ref, h_ref, g_ref, sc_ref, sh_ref, 1, L)

    wa = wa_ref[...].astype(BF16)
    wg = wg_ref[...].astype(BF16)
    tail = jnp.where(t == 0, 0.0, carry_ref[j])
    for blk in range(nb):
        hb = h_ref[blk * rb:(blk + 1) * rb, :]
        a = _dot(hb, wa) + ba_ref[...]
        g = _dot(hb, wg) + bg_ref[...]
        u = a * jax.nn.sigmoid(g)
        ufull = jnp.concatenate([tail, u], axis=0)
        for cb in range(tn // LANES):
            gcols = pl.ds(pl.multiple_of(j * tn + cb * LANES, LANES), LANES)
            lcols = slice(cb * LANES, (cb + 1) * LANES)
            bias = bdw_ref[:, gcols]
            for r in range(rb // R):
                win = ufull[r * R:r * R + R + CONV_HIST, lcols]
                y_ref[blk * rb + r * R:blk * rb + (r + 1) * R, lcols] = _dwconv_block(win, wb_ref, bias, gcols, R)
        tail = u[rb - CONV_HIST:, :]
    carry_ref[j] = tail
    tail_ref[...] = tail


def _conv_mix(x, mod_head, g_norm, w_in, b_in, w_dw, b_dw, c_all, w_mod, b_mod, *, layer, tm, R, nb):
    Bp, T, _ = x.shape
    tn = MIX_CHUNK
    J = D_MODEL // tn
    nT = T // tm
    b2 = b_in.reshape(-1, 1, 2 * D_MODEL)
    rows = c_all.shape[0]
    per_vec = D_MODEL // MOD_BLOCK
    per_layer = 6 * per_vec
    n_mod = DEPTH * per_layer
    assert n_mod <= Bp * nT * J

    def mod_block(b, t, j):
        sb = jnp.minimum((b * nT + t) * J + j, n_mod - 1)
        return sb // per_layer, sb % per_layer

    def wm_map(b, t, j):
        layer_, cb = mod_block(b, t, j)
        return layer_, 0, cb

    def mod_out_map(b, t, j):
        layer_, cb = mod_block(b, t, j)
        return layer_, cb // per_vec, 0, cb % per_vec

    return pl.pallas_call(
        functools.partial(_conv_mix_kernel, L=tm, R=R, nb=nb, n_mod=n_mod),
        grid=(Bp, nT, J),
        in_specs=[_row_spec(tm), _vec_spec(4 * layer + 0), _mod_spec(1, layer, 1), _mod_spec(1, layer, 0),
                  pl.BlockSpec((None, D_MODEL, tn), lambda b, t, j: (0, 0, j)),
                  pl.BlockSpec((None, D_MODEL, tn), lambda b, t, j: (0, 0, j + J)),
                  pl.BlockSpec((None, 1, tn), lambda b, t, j: (0, 0, j)),
                  pl.BlockSpec((None, 1, tn), lambda b, t, j: (0, 0, j + J)),
                  pl.BlockSpec((None, CONV_K, D_MODEL), lambda b, t, j: (0, 0, 0)),
                  _vec_spec(0),
                  pl.BlockSpec((rows, D_MODEL), lambda b, t, j: (0, 0)),
                  pl.BlockSpec((None, D_MODEL, MOD_BLOCK), wm_map),
                  pl.BlockSpec((None, 1, MOD_BLOCK), wm_map)],
        out_specs=[pl.BlockSpec((None, tm, tn), lambda b, t, j: (b, t, j)),
                   pl.BlockSpec((None, None, CONV_HIST, tn), lambda b, t, j: (b, t, 0, j)),
                   pl.BlockSpec((None, None, rows, MOD_BLOCK), mod_out_map)],
        out_shape=[jax.ShapeDtypeStruct((Bp, T, D_MODEL), F32),
                   jax.ShapeDtypeStruct((Bp, nT, CONV_HIST, D_MODEL), F32),
                   jax.ShapeDtypeStruct((DEPTH, 6, rows, D_MODEL), F32)],
        scratch_shapes=[pltpu.VMEM((tm, D_MODEL), BF16),
                        pltpu.VMEM((CONV_K, SUBLANES, D_MODEL), F32),
                        pltpu.VMEM((J, CONV_HIST, tn), F32)],
        compiler_params=_cparams(3),
        name="conv_mix",
    )(x, g_norm, mod_head, mod_head, w_in, w_in, b2, b2, w_dw, b_dw.reshape(-1, 1, D_MODEL),
      c_all, w_mod, b_mod.reshape(DEPTH, 1, 6 * D_MODEL))


def _conv_proj_kernel(y_ref, lng_ref, lnb_ref, wout_ref, bout_ref, x_ref, g_ref, ga_ref, x1_ref, *, L):
    rb = L // PROJ_BLOCKS
    w = wout_ref[...]
    for blk in range(PROJ_BLOCKS):
        rows = slice(blk * rb, (blk + 1) * rb)
        a = _ln_silu_value(y_ref[rows, :], lng_ref, lnb_ref)
        m = _dot(a, w) + bout_ref[...]
        x1_ref[rows, :] = _residual_value(m, x_ref[rows, :], g_ref, ga_ref[0:1, :])


def _conv_proj(y, x, mod, g_norm, ln_g, ln_b, w_out, b_out, *, layer, tm):
    Bp, T, _ = x.shape
    vec = lambda a: a.reshape(-1, 1, D_MODEL)
    return pl.pallas_call(
        functools.partial(_conv_proj_kernel, L=tm),
        grid=(Bp, T // tm),
        in_specs=[_row_spec(tm), _vec_spec(0), _vec_spec(0),
                  pl.BlockSpec((None, D_MODEL, D_MODEL), lambda b, t: (0, 0, 0), pipeline_mode=pl.Buffered(1)),
                  _vec_spec(0), _row_spec(tm), _vec_spec(4 * layer + 1), _mod_spec(1, layer, 2)],
        out_specs=_row_spec(tm),
        out_shape=jax.ShapeDtypeStruct((Bp, T, D_MODEL), F32),
        compiler_params=_cparams(2),
        name="conv_proj",
    )(y, vec(ln_g), vec(ln_b), w_out, vec(b_out), x, g_norm, mod)


def _dwconv(full_ref, wb_ref, bdw_ref, y_ref, S, L, R):
    def body(cb, c):
        cols = pl.ds(pl.multiple_of(cb * LANES, LANES), LANES)
        bias = bdw_ref[:, cols]
        for s in range(S):
            for r in range(L // R):
                win = full_ref[s, r * R:r * R + R + CONV_HIST, cols]
                y_ref[s * L + r * R:s * L + (r + 1) * R, cols] = _dwconv_block(win, wb_ref, bias, cols, R)
        return c

    lax.fori_loop(0, D_MODEL // LANES, body, 0)


def _ln_silu_value(y, lng_ref, lnb_ref):
    mu = jnp.mean(y, axis=-1, keepdims=True)
    yc = y - mu
    var = jnp.mean(yc * yc, axis=-1, keepdims=True)
    z = yc * lax.rsqrt(var + EPS) * lng_ref[...] + lnb_ref[...]
    return z * jax.nn.sigmoid(z)


def _ln_silu(y_ref, lng_ref, lnb_ref, S, L):
    _row_loop(S, L, [y_ref], y_ref, lambda s, y: _ln_silu_value(y, lng_ref, lnb_ref))


def _conv_out_kernel(u_ref, hist_ref, wdw_ref, bdw_ref, lng_ref, lnb_ref, wout_ref, bout_ref, x_ref, g_ref, ga_ref,
                     x1_ref, full_ref, wb_ref, y_ref, *, S, L, R):
    _fill_tap_table(wdw_ref, wb_ref)
    pad = CONV_HIST - (CONV_K - 1)
    full_ref[:, 0:pad, :] = jnp.zeros((S, pad, D_MODEL), F32)
    full_ref[:, pad:CONV_HIST, :] = hist_ref[...]
    full_ref[:, CONV_HIST:CONV_HIST + L, :] = u_ref[...].reshape(S, L, D_MODEL)
    _dwconv(full_ref, wb_ref, bdw_ref, y_ref, S, L, R)
    _ln_silu(y_ref, lng_ref, lnb_ref, S, L)
    x1_ref[...] = _dot(y_ref[...], wout_ref[...]) + bout_ref[...]
    _residual(x1_ref, x_ref, g_ref, ga_ref, x1_ref, S, L)


def _conv_out(u, hist, x, mod, g_norm, w_dw, b_dw, ln_g, ln_b, w_out, b_out, *, layer, S, L, tm, R):
    Bp, T, _ = x.shape
    assert Bp == 1 and T == tm
    vec = lambda a: a.reshape(-1, 1, D_MODEL)
    return pl.pallas_call(
        functools.partial(_conv_out_kernel, S=S, L=L, R=R),
        grid=(Bp, T // tm),
        in_specs=[_row_spec(tm), pl.BlockSpec((S, CONV_K - 1, D_MODEL), lambda b, t: (0, 0, 0)),
                  pl.BlockSpec((None, CONV_K, D_MODEL), lambda b, t: (0, 0, 0)),
                  _vec_spec(0), _vec_spec(0), _vec_spec(0),
                  pl.BlockSpec((None, D_MODEL, D_MODEL), lambda b, t: (0, 0, 0), pipeline_mode=pl.Buffered(1)),
                  _vec_spec(0), _row_spec(tm), _vec_spec(4 * layer + 1), _mod_spec(S, layer, 2)],
        out_specs=_row_spec(tm),
        out_shape=jax.ShapeDtypeStruct((Bp, T, D_MODEL), F32),
        scratch_shapes=[pltpu.VMEM((S, CONV_HIST + L, D_MODEL), F32),
                        pltpu.VMEM((CONV_K, SUBLANES, D_MODEL), F32),
                        pltpu.VMEM((tm, D_MODEL), F32)],
        compiler_params=_cparams(2),
        name="conv_out",
    )(u, hist, w_dw, vec(b_dw), vec(ln_g), vec(ln_b), w_out, vec(b_out), x, g_norm, mod)


def _ffn_kernel(*refs, S, L, nb, prompt):
    if prompt:
        (x_ref, g2_ref, sc_ref, sh_ref, wg_ref, wv_ref, wdw_ref, bdw_ref, wd_ref, g3_ref, ga_ref,
         x2_ref, st_ref, h_ref, carry_ref) = refs
    else:
        (x_ref, g2_ref, sc_ref, sh_ref, wg_ref, wv_ref, wdw_ref, bdw_ref, wd_ref, g3_ref, ga_ref, hist_ref,
         x2_ref, st_ref, h_ref) = refs
    t = pl.program_id(1)
    j = pl.program_id(2)
    last_j = pl.num_programs(2) - 1
    tm = x_ref.shape[0]
    rb = tm // nb
    seg = min(L, rb)
    pad = SUBLANES
    inline = S == 1

    mm_dtype = h_ref.dtype

    def step(first, last):
        wg = wg_ref[...].astype(mm_dtype)
        wv = wv_ref[...].astype(mm_dtype)
        wd = wd_ref[...].astype(mm_dtype)
        w0 = wdw_ref[0:1, :]
        w1 = wdw_ref[1:2, :]
        w2 = wdw_ref[2:3, :]
        bd = bdw_ref[...]
        tail = None
        for blk in range(nb):
            rows = slice(blk * rb, (blk + 1) * rb)
            if first:
                x = x_ref[rows, :]
                inv = lax.rsqrt(jnp.mean(x * x, axis=-1, keepdims=True) + EPS)
                hb = (x * inv * (g2_ref[...] * (1.0 + sc_ref[0:1, :])) + sh_ref[0:1, :]).astype(mm_dtype)
                h_ref[rows, :] = hb
            else:
                hb = h_ref[rows, :]
            g = _dot(hb, wg)
            v = _dot(hb, wv)
            p1s, p2s = [], []
            for q in range(rb // seg):
                gq = g[q * seg:(q + 1) * seg, :]
                if not prompt:
                    s = blk * (rb // seg) + q
                    hist = hist_ref[s]
                    st_ref[s] = gq[seg - (FFN_K - 1):, :]
                elif blk == 0:
                    hist = jnp.where(t == 0, 0.0, carry_ref[j])
                else:
                    hist = tail
                gfull = jnp.concatenate([hist, gq], axis=0)
                p1s.append(pltpu.roll(gfull, 1, axis=0)[pad:, :])
                p2s.append(pltpu.roll(gfull, 2, axis=0)[pad:, :])
                tail = gq[seg - pad:, :]
            p1 = p1s[0] if len(p1s) == 1 else jnp.concatenate(p1s, axis=0)
            p2 = p2s[0] if len(p2s) == 1 else jnp.concatenate(p2s, axis=0)
            gc = g * w2 + p1 * w1 + p2 * w0 + bd
            act = (jax.nn.gelu(gc) * v).astype(mm_dtype)
            acc = _dot(act, wd)
            if not first:
                acc = x2_ref[rows, :] + acc
            if last:
                acc = _residual_value(acc, x_ref[rows, :], g3_ref, ga_ref[0:1, :])
            x2_ref[rows, :] = acc
        if prompt:
            carry_ref[j] = tail
            st_ref[0] = tail[pad - (FFN_K - 1):, :]

    if inline:
        pl.when(j == 0)(lambda: step(True, False))
        pl.when((j > 0) & (j < last_j))(lambda: step(False, False))
        pl.when(j == last_j)(lambda: step(False, True))
    else:
        @pl.when(j == 0)
        def _():
            _norm_mod(x_ref, h_ref, g2_ref, sc_ref, sh_ref, S, L)
            x2_ref[...] = jnp.zeros(x2_ref.shape, F32)

        step(False, False)

        @pl.when(j == last_j)
        def _():
            _residual(x2_ref, x_ref, g3_ref, ga_ref, x2_ref, S, L)


def _ffn(x, hist, mod, g_norm, w_up, w_dw, b_dw, w_down, *, layer, S, L, tm, tf, nb):
    Bp, T, _ = x.shape
    J = D_FF // tf
    nT = T // tm
    prompt = hist is None
    x_spec = pl.BlockSpec((None, tm, D_MODEL), lambda b, t, j: (b, t, 0), pipeline_mode=pl.Buffered(1))
    in_specs = [x_spec, _vec_spec(4 * layer + 2), _mod_spec(S, layer, 4), _mod_spec(S, layer, 3),
                pl.BlockSpec((None, D_MODEL, tf), lambda b, t, j: (layer, 0, j)),
                pl.BlockSpec((None, D_MODEL, tf), lambda b, t, j: (layer, 0, j + J)),
                pl.BlockSpec((None, FFN_K, tf), lambda b, t, j: (layer, 0, j)),
                pl.BlockSpec((None, 1, tf), lambda b, t, j: (layer, 0, j)),
                pl.BlockSpec((None, tf, D_MODEL), lambda b, t, j: (layer, j, 0)),
                _vec_spec(4 * layer + 3), _mod_spec(S, layer, 5)]
    args = [x, g_norm, mod, mod, w_up, w_up, w_dw, b_dw.reshape(DEPTH, 1, D_FF), w_down, g_norm, mod]
    scratch = [pltpu.VMEM((tm, D_MODEL), BF16 if prompt else F32)]
    if prompt:
        scratch.append(pltpu.VMEM((J, SUBLANES, tf), F32))
    else:
        in_specs.append(pl.BlockSpec((S, SUBLANES, tf), lambda b, t, j: (0, 0, j)))
        args.append(hist)
    return pl.pallas_call(
        functools.partial(_ffn_kernel, S=S, L=L, nb=nb, prompt=prompt),
        grid=(Bp, nT, J),
        in_specs=in_specs,
        out_specs=[_row_spec(tm),
                   pl.BlockSpec((None, None, S, FFN_K - 1, tf), lambda b, t, j: (b, t, 0, 0, j))],
        out_shape=[jax.ShapeDtypeStruct((Bp, T, D_MODEL), F32),
                   jax.ShapeDtypeStruct((Bp, nT, S, FFN_K - 1, D_FF), F32)],
        scratch_shapes=scratch,
        compiler_params=pltpu.CompilerParams(dimension_semantics=("arbitrary",) * 3,
                                             vmem_limit_bytes=VMEM_LIMIT_FFN),
        name="ffn",
    )(*args)


def _qkv_kernel(x_ref, g_ref, sc_ref, sh_ref, wq_ref, wk_ref, wv_ref, q_ref, k_ref, v_ref, h_ref, *, S, L):
    tm = x_ref.shape[0]
    if S > 1:
        _norm_mod(x_ref, h_ref, g_ref, sc_ref, sh_ref, S, L)
        blocks = [(slice(0, tm), h_ref[...])]
    else:
        rb = tm // PROJ_BLOCKS
        mul = g_ref[...] * (1.0 + sc_ref[0:1, :])
        blocks = []
        for blk in range(PROJ_BLOCKS):
            rows = slice(blk * rb, (blk + 1) * rb)
            x = x_ref[rows, :]
            inv = lax.rsqrt(jnp.mean(x * x, axis=-1, keepdims=True) + EPS)
            blocks.append((rows, x * inv * mul + sh_ref[0:1, :]))
    wq = wq_ref[...]
    wk = wk_ref[...]
    wv = wv_ref[...]
    for rows, h in blocks:
        q_ref[rows, :] = _dot(h, wq)
        k_ref[rows, :] = _dot(h, wk)
        v_ref[rows, :] = _dot(h, wv)


def _qkv(x, mod, g_norm, w_q, w_k, w_v, *, layer, S, L, tm):
    Bp, T, _ = x.shape
    kv_spec = pl.BlockSpec((None, tm, KV_DIM), lambda b, t: (b, t, 0))
    resident = lambda n: pl.BlockSpec((None, D_MODEL, n), lambda b, t: (0, 0, 0), pipeline_mode=pl.Buffered(1))
    return pl.pallas_call(
        functools.partial(_qkv_kernel, S=S, L=L),
        grid=(Bp, T // tm),
        in_specs=[_row_spec(tm), _vec_spec(4 * layer + 0), _mod_spec(S, layer, 1), _mod_spec(S, layer, 0),
                  resident(D_MODEL), resident(KV_DIM), resident(KV_DIM)],
        out_specs=[_row_spec(tm), kv_spec, kv_spec],
        out_shape=[jax.ShapeDtypeStruct((Bp, T, D_MODEL), F32),
                   jax.ShapeDtypeStruct((Bp, T, KV_DIM), F32),
                   jax.ShapeDtypeStruct((Bp, T, KV_DIM), F32)],
        scratch_shapes=[pltpu.VMEM((tm, D_MODEL), F32)],
        compiler_params=_cparams(2),
        name="qkv",
    )(x, g_norm, mod, mod, w_q, w_k, w_v)


REL_SPAN = 3 * LANES


def _bias_kernel(rb_ref, prof_ref, valid_ref, o_ref, *, Qn):
    prof = prof_ref[...]
    valid = valid_ref[...] != 0
    early = lax.broadcasted_iota(jnp.int32, valid.shape, 0) < WINDOW
    for hh in range(N_HEADS):
        f = jnp.zeros(prof.shape, F32)
        for b in range(N_BUCKETS):
            f = jnp.where(prof == b, rb_ref[b, hh], f)
        h, g = divmod(hh, GROUP)
        off = (g * Qn) % LANES
        x = jnp.broadcast_to(f[0:1, :], (KEY_WIN, REL_SPAN))
        r = pltpu.roll(x, WINDOW + off, axis=1, stride=1, stride_axis=0)
        tab = jnp.where(valid, r[:, off:off + Qn], NEG_INF)
        o_ref[0, h, :, g * Qn:(g + 1) * Qn] = tab
        if o_ref.shape[0] > 1:
            o_ref[1, h, :, g * Qn:(g + 1) * Qn] = jnp.where(early, NEG_INF, tab)


def _t5_bucket(rel):
    half = N_BUCKETS // 2
    max_exact = half // 2
    n = jnp.abs(rel)
    ret = jnp.where(rel > 0, half, 0)
    nf = jnp.maximum(n, 1).astype(F32)
    large = max_exact + (jnp.log(nf / max_exact) / math.log(MAX_DISTANCE / max_exact)
                         * (half - max_exact)).astype(jnp.int32)
    large = jnp.minimum(large, half - 1)
    return ret + jnp.where(n < max_exact, n, large)


def _bias_table(rel_bias, Qn, lo, hi, variants):
    assert KEY_WIN + Qn - 1 <= REL_SPAN
    k = jnp.arange(KEY_WIN)[:, None]
    q = jnp.arange(Qn)[None, :]
    valid = ((k >= lo(q)) & (k < hi(q))).astype(jnp.int32)
    prof = jnp.broadcast_to(_t5_bucket(WINDOW - jnp.arange(REL_SPAN)).astype(jnp.int32)[None, :], (SUBLANES, REL_SPAN))
    return pl.pallas_call(
        functools.partial(_bias_kernel, Qn=Qn),
        in_specs=[pl.BlockSpec(memory_space=pltpu.SMEM),
                  pl.BlockSpec((SUBLANES, REL_SPAN), lambda: (0, 0)),
                  pl.BlockSpec((KEY_WIN, Qn), lambda: (0, 0))],
        out_specs=pl.BlockSpec((variants, N_KV_HEADS, KEY_WIN, GROUP * Qn), lambda: (0, 0, 0, 0)),
        out_shape=jax.ShapeDtypeStruct((variants, N_KV_HEADS, KEY_WIN, GROUP * Qn), F32),
        name="rel_bias_table",
    )(rel_bias, prof, valid)


def _attend(q_ref, r0, Qn, kw, vw, bias_ref, sel, sink_ref, o_ref, qs_ref):
    rows = pl.ds(r0, Qn)
    for g in range(GROUP):
        for h in range(N_KV_HEADS):
            src = (h * GROUP + g) * HEAD_DIM
            qs_ref[g * Qn:(g + 1) * Qn, h * HEAD_DIM:(h + 1) * HEAD_DIM] = q_ref[rows, src:src + HEAD_DIM] * ATTN_SCALE
    qt = qs_ref[...].T
    vt = vw.T
    outs = []
    for h in range(N_KV_HEADS):
        hd = slice(h * HEAD_DIM, (h + 1) * HEAD_DIM)
        s = _dot(kw[:, hd], qt[hd, :]) + bias_ref[sel, h]
        sk = sink_ref[h]
        mx = jnp.maximum(jnp.max(s, axis=0, keepdims=True), sk)
        p = jnp.exp(s - mx)
        den = jnp.sum(p, axis=0, keepdims=True) + jnp.exp(sk - mx)
        outs.append(_dot(vt[hd, :], p) * (1.0 / den))
    o2 = jnp.concatenate(outs, axis=0).T
    for g in range(GROUP):
        for h in range(N_KV_HEADS):
            dst = (h * GROUP + g) * HEAD_DIM
            o_ref[rows, dst:dst + HEAD_DIM] = o2[g * Qn:(g + 1) * Qn, h * HEAD_DIM:(h + 1) * HEAD_DIM]


def _attn_prompt_kernel(q_ref, kp_ref, kc_ref, vp_ref, vc_ref, bias_ref, sink_ref, o_ref, kw_ref, vw_ref, qs_ref,
                        *, tq, Qn):
    t = pl.program_id(1)
    kw_ref[0:WINDOW, :] = kp_ref[...]
    kw_ref[WINDOW:WINDOW + tq, :] = kc_ref[...]
    vw_ref[0:WINDOW, :] = vp_ref[...]
    vw_ref[WINDOW:WINDOW + tq, :] = vc_ref[...]

    def body(mp, carry):
        for u in range(ATTN_UNITS_PER_ITER):
            m = mp * ATTN_UNITS_PER_ITER + u
            r0 = pl.multiple_of(m * Qn, Qn)
            kw = kw_ref[pl.ds(r0, KEY_WIN), :]
            vw = vw_ref[pl.ds(r0, KEY_WIN), :]
            sel = ((t == 0) & (m == 0)).astype(jnp.int32)
            _attend(q_ref, r0, Qn, kw, vw, bias_ref, sel, sink_ref, o_ref, qs_ref.at[u])
        return carry

    lax.fori_loop(0, tq // (Qn * ATTN_UNITS_PER_ITER), body, 0)


def _attn_sample_kernel(q_ref, ck_ref, kn_ref, cv_ref, vn_ref, bias_ref, sink_ref, o_ref, kw_ref, vw_ref, qs_ref,
                        *, S, L):
    P = ck_ref.shape[1]
    kw_ref[:, 0:P, :] = ck_ref[...]
    kw_ref[:, P:P + L, :] = kn_ref[...].reshape(S, L, KV_DIM)
    kw_ref[:, P + L:, :] = jnp.zeros((S, KEY_WIN - P - L, KV_DIM), F32)
    vw_ref[:, 0:P, :] = cv_ref[...]
    vw_ref[:, P:P + L, :] = vn_ref[...].reshape(S, L, KV_DIM)
    vw_ref[:, P + L:, :] = jnp.zeros((S, KEY_WIN - P - L, KV_DIM), F32)

    def body(sp, carry):
        for u in range(ATTN_UNITS_PER_ITER):
            s = sp * ATTN_UNITS_PER_ITER + u
            r0 = pl.multiple_of(s * L, L)
            _attend(q_ref, r0, L, kw_ref[s], vw_ref[s], bias_ref, 0, sink_ref, o_ref, qs_ref.at[u])
        return carry

    lax.fori_loop(0, S // ATTN_UNITS_PER_ITER, body, 0)


def _attn_prompt(q, k, v, bias, sink, *, tq):
    Bp, T, _ = q.shape
    Qn = 2 * CHUNK
    per = tq // WINDOW
    prev_spec = pl.BlockSpec((None, WINDOW, KV_DIM), lambda b, t: (b, jnp.maximum(t * per - 1, 0), 0))
    cur_spec = pl.BlockSpec((None, tq, KV_DIM), lambda b, t: (b, t, 0))
    return pl.pallas_call(
        functools.partial(_attn_prompt_kernel, tq=tq, Qn=Qn),
        grid=(Bp, T // tq),
        in_specs=[_row_spec(tq), prev_spec, cur_spec, prev_spec, cur_spec,
                  pl.BlockSpec((2, N_KV_HEADS, KEY_WIN, GROUP * Qn), lambda b, t: (0, 0, 0, 0),
                               pipeline_mode=pl.Buffered(1)),
                  pl.BlockSpec((N_KV_HEADS, 1, GROUP * Qn), lambda b, t: (0, 0, 0))],
        out_specs=_row_spec(tq),
        out_shape=jax.ShapeDtypeStruct((Bp, T, D_MODEL), F32),
        scratch_shapes=[pltpu.VMEM((WINDOW + tq, KV_DIM), F32), pltpu.VMEM((WINDOW + tq, KV_DIM), F32),
                        pltpu.VMEM((ATTN_UNITS_PER_ITER, GROUP * Qn, KV_DIM), F32)],
        compiler_params=_cparams(2),
        name="attn_prompt",
    )(q, k, k, v, v, bias, sink)


def _attn_sample(q, k, v, cache_k, cache_v, bias, sink, *, S, L):
    return pl.pallas_call(
        functools.partial(_attn_sample_kernel, S=S, L=L),
        out_shape=jax.ShapeDtypeStruct((S * L, D_MODEL), F32),
        scratch_shapes=[pltpu.VMEM((S, KEY_WIN, KV_DIM), F32), pltpu.VMEM((S, KEY_WIN, KV_DIM), F32),
                        pltpu.VMEM((ATTN_UNITS_PER_ITER, GROUP * L, KV_DIM), F32)],
        compiler_params=pltpu.CompilerParams(vmem_limit_bytes=VMEM_LIMIT),
        name="attn_sample",
    )(q, cache_k, k, cache_v, v, bias, sink)


def _residual_value(m, x, g_ref, ga):
    inv = lax.rsqrt(jnp.mean(m * m, axis=-1, keepdims=True) + EPS)
    return x + m * inv * (g_ref[...] * ga)


def _attn_out_kernel(o_ref, w_ref, x_ref, g_ref, ga_ref, x1_ref, *, S, L):
    tm = o_ref.shape[0]
    if S > 1:
        x1_ref[...] = _dot(o_ref[...], w_ref[...])
        _residual(x1_ref, x_ref, g_ref, ga_ref, x1_ref, S, L)
        return
    rb = tm // PROJ_BLOCKS
    w = w_ref[...]
    for blk in range(PROJ_BLOCKS):
        rows = slice(blk * rb, (blk + 1) * rb)
        x1_ref[rows, :] = _residual_value(_dot(o_ref[rows, :], w), x_ref[rows, :], g_ref, ga_ref[0:1, :])


def _attn_out(o, x, mod, g_norm, w_o, *, layer, S, L, tm):
    Bp, T, _ = x.shape
    return pl.pallas_call(
        functools.partial(_attn_out_kernel, S=S, L=L),
        grid=(Bp, T // tm),
        in_specs=[_row_spec(tm),
                  pl.BlockSpec((None, D_MODEL, D_MODEL), lambda b, t: (0, 0, 0), pipeline_mode=pl.Buffered(1)),
                  _row_spec(tm), _vec_spec(4 * layer + 1), _mod_spec(S, layer, 2)],
        out_specs=_row_spec(tm),
        out_shape=jax.ShapeDtypeStruct((Bp, T, D_MODEL), F32),
        compiler_params=_cparams(2),
        name="attn_out",
    )(o, w_o, x, g_norm, mod)


def kernel(x_prompt, x_sample, c_prompt, c_sample, cache_conv, cache_k, cache_v, cache_ffn, w_mod, b_mod, g_norm, conv_w_in, conv_b_in, conv_w_dw, conv_b_dw, conv_ln_g, conv_ln_b, conv_w_out, conv_b_out, attn_w_q, attn_w_k, attn_w_v, attn_w_o, attn_sinks, rel_bias, ffn_w_up, ffn_w_dw, ffn_b_dw, ffn_w_down):
    B, T, D = x_prompt.shape
    SB, SL, _ = x_sample.shape
    assert SB == MOD_ROWS_SAMPLE
    c_all = jnp.concatenate(
        [c_sample, jnp.pad(c_prompt[:, None, :], ((0, 0), (0, SUBLANES - 1), (0, 0))).reshape(B * SUBLANES, D)], axis=0)
    mod_head = _modulation_head(c_all, w_mod, b_mod)
    gn = g_norm.reshape(DEPTH * 4, 1, D)

    geo_p = dict(S=1, L=STREAM_ROWS, tm=STREAM_ROWS)
    geo_p_small = dict(S=1, L=RESIDENT_ROWS, tm=RESIDENT_ROWS)
    geo_s = dict(S=SB, L=SL, tm=SB * SL)

    xp = x_prompt
    xs = x_sample.reshape(1, SB * SL, D)

    yp, tail_p, mod = _conv_mix(xp, mod_head, gn, conv_w_in, conv_b_in, conv_w_dw, conv_b_dw, c_all, w_mod, b_mod,
                                layer=0, tm=STREAM_ROWS, R=MIX_CONV_ROWS, nb=MIX_ROW_BLOCKS)
    xp = _conv_proj(yp, xp, mod, gn, conv_ln_g, conv_ln_b, conv_w_out, conv_b_out, layer=0, tm=RESIDENT_ROWS)
    us = _conv_in(xs, mod_head, gn, conv_w_in, conv_b_in, layer=0, **geo_s)
    conv_args = (conv_w_dw, conv_b_dw, conv_ln_g, conv_ln_b, conv_w_out, conv_b_out)
    xs = _conv_out(us, cache_conv[0], xs, mod, gn, *conv_args, layer=0, R=SL, **geo_s)
    conv_state_p = tail_p[:, -1, CONV_HIST - (CONV_K - 1):, :][None]
    conv_state_s = us.reshape(SB, SL, D)[:, SL - (CONV_K - 1):, :][None]

    ffn_p, ffn_s = [], []
    ffn_hist = jnp.pad(cache_ffn, ((0, 0), (0, 0), (SUBLANES - (FFN_K - 1), 0), (0, 0)))

    def run_ffn(i, xp, xs):
        w = (ffn_w_up, ffn_w_dw, ffn_b_dw, ffn_w_down)
        xp, st_p = _ffn(xp, None, mod, gn, *w, layer=i, tf=FFN_CHUNK, nb=FFN_ROW_BLOCKS, **geo_p)
        xs, st_s = _ffn(xs, ffn_hist[i], mod, gn, *w, layer=i, tf=FFN_CHUNK, nb=FFN_ROW_BLOCKS, **geo_s)
        ffn_p.append(st_p[:, -1, 0])
        ffn_s.append(st_s[0, 0])
        return xp, xs

    xp, xs = run_ffn(0, xp, xs)

    qp, kp, vp = _qkv(xp, mod, gn, attn_w_q, attn_w_k, attn_w_v, layer=1, **geo_p_small)
    qs, ks, vs = _qkv(xs, mod, gn, attn_w_q, attn_w_k, attn_w_v, layer=1, **geo_s)
    P = cache_k.shape[2]
    assert P == WINDOW and P + SL <= KEY_WIN
    bias_p = _bias_table(rel_bias, 2 * CHUNK,
                         lambda q: (q // CHUNK) * CHUNK, lambda q: (q // CHUNK) * CHUNK + WINDOW + CHUNK, 2)
    bias_s = _bias_table(rel_bias, SL, lambda q: 0 * q, lambda q: 0 * q + P + SL, 1)
    sink_p = jnp.repeat(attn_sinks[0], 2 * CHUNK).reshape(N_KV_HEADS, 1, GROUP * 2 * CHUNK)
    sink_s = jnp.repeat(attn_sinks[0], SL).reshape(N_KV_HEADS, 1, GROUP * SL)
    op = _attn_prompt(qp, kp, vp, bias_p, sink_p, tq=ATTN_ROWS)
    ck = cache_k[0].reshape(SB, P, KV_DIM)
    cv = cache_v[0].reshape(SB, P, KV_DIM)
    os_ = _attn_sample(qs[0], ks[0], vs[0], ck, cv, bias_s, sink_s, S=SB, L=SL)[None]
    xp = _attn_out(op, xp, mod, gn, attn_w_o, layer=1, **geo_p_small)
    xs = _attn_out(os_, xs, mod, gn, attn_w_o, layer=1, **geo_s)
    keep = min(WINDOW, T)
    k_state_p = kp[:, T - keep:].reshape(B, keep, N_KV_HEADS, HEAD_DIM)[None]
    v_state_p = vp[:, T - keep:].reshape(B, keep, N_KV_HEADS, HEAD_DIM)[None]
    k_state_s = jnp.concatenate([ck, ks.reshape(SB, SL, KV_DIM)], axis=1)[:, SL:]
    v_state_s = jnp.concatenate([cv, vs.reshape(SB, SL, KV_DIM)], axis=1)[:, SL:]
    k_state_s = k_state_s.reshape(SB, P, N_KV_HEADS, HEAD_DIM)[None]
    v_state_s = v_state_s.reshape(SB, P, N_KV_HEADS, HEAD_DIM)[None]

    xp, xs = run_ffn(1, xp, xs)

    return (xp, xs.reshape(SB, SL, D), conv_state_p, conv_state_s,
            k_state_p, v_state_p, k_state_s, v_state_s,
            jnp.stack(ffn_p), jnp.stack(ffn_s))
```

```python
import functools
import math

import jax
import jax.numpy as jnp
from jax import lax
from jax.experimental import pallas as pl
from jax.experimental.pallas import tpu as pltpu

F32 = jnp.float32
BF16 = jnp.bfloat16

D_MODEL = 2048
D_FF = 5632
DEPTH = 2
CONV_K = 31
FFN_K = 3
N_HEADS = 32
N_KV_HEADS = 4
GROUP = N_HEADS // N_KV_HEADS
HEAD_DIM = 64
KV_DIM = N_KV_HEADS * HEAD_DIM
CHUNK = 64
WINDOW = 128
N_BUCKETS = 32
MAX_DISTANCE = 128
EPS = 1e-6
NEG_INF = -1e30
ATTN_SCALE = HEAD_DIM ** -0.5

SUBLANES = 8
LANES = 128
CONV_HIST = 32
KEY_WIN = 2 * WINDOW
VMEM_LIMIT = 58 * 1024 * 1024
VMEM_LIMIT_FFN = 62 * 1024 * 1024
MOD_ROWS_SAMPLE = 16


def _cparams(n_grid):
    return pltpu.CompilerParams(dimension_semantics=("arbitrary",) * n_grid,
                                vmem_limit_bytes=VMEM_LIMIT)


def _dot(a, b):
    return jnp.dot(a, b, preferred_element_type=F32)


ROW_SLAB = 64
ROW_GROUP = 2
PROJ_BLOCKS = 2

STREAM_ROWS = 1024
RESIDENT_ROWS = 512
FFN_CHUNK = 512
FFN_ROW_BLOCKS = 1
MIX_CHUNK = 256
MIX_ROW_BLOCKS = 2
MIX_CONV_ROWS = 64
IN_PROJ_CHUNK = 512
MOD_HEAD_CHUNK = 1024
ATTN_ROWS = 512
ATTN_UNITS_PER_ITER = 4


def _row_loop(S, L, srcs, dst, fn):
    slab = min(L, ROW_SLAB)
    n = L // slab
    group = ROW_GROUP if n % ROW_GROUP == 0 else 1
    for s in range(S):

        def body(r, c, s=s):
            rows = [pl.ds(pl.multiple_of(s * L + (r * group + i) * slab, slab), slab) for i in range(group)]
            vals = [[src[rw, :] for src in srcs] for rw in rows]
            outs = [fn(s, *v) for v in vals]
            for rw, out in zip(rows, outs):
                dst[rw, :] = out.astype(dst.dtype)
            return c

        lax.fori_loop(0, n // group, body, 0)


def _norm_mod(x_ref, dst_ref, g_ref, sc_ref, sh_ref, S, L):
    def fn(s, x):
        inv = lax.rsqrt(jnp.mean(x * x, axis=-1, keepdims=True) + EPS)
        mul = g_ref[...] * (1.0 + sc_ref[s:s + 1, :])
        return x * inv * mul + sh_ref[s:s + 1, :]

    _row_loop(S, L, [x_ref], dst_ref, fn)


def _residual(m_ref, x_ref, g_ref, ga_ref, out_ref, S, L):
    def fn(s, m, x):
        inv = lax.rsqrt(jnp.mean(m * m, axis=-1, keepdims=True) + EPS)
        return x + m * inv * (g_ref[...] * ga_ref[s:s + 1, :])

    _row_loop(S, L, [m_ref, x_ref], out_ref, fn)


def _row_spec(tm):
    return pl.BlockSpec((None, tm, D_MODEL), lambda b, t, *_: (b, t, 0))


def _vec_spec(index, n=D_MODEL):
    return pl.BlockSpec((None, 1, n), lambda *_: (index, 0, 0))


def _mod_spec(S, layer, which):
    if S == 1:
        first = MOD_ROWS_SAMPLE // SUBLANES
        return pl.BlockSpec((None, None, SUBLANES, D_MODEL), lambda b, t, *_: (layer, which, first + b, 0))
    return pl.BlockSpec((None, None, S, D_MODEL), lambda b, t, *_: (layer, which, 0, 0))


def _mod_kernel(c_ref, w_ref, b_ref, o_ref):
    c = c_ref[...]
    o_ref[...] = _dot(c * jax.nn.sigmoid(c), w_ref[...]) + b_ref[...]


MOD_HEAD_VECS = 2
MOD_BLOCK = 512


def _modulation_head(c_all, w_mod, b_mod):
    rows = c_all.shape[0]
    tn = MOD_HEAD_CHUNK
    per = D_MODEL // tn
    return pl.pallas_call(
        _mod_kernel,
        grid=(MOD_HEAD_VECS * per,),
        in_specs=[pl.BlockSpec((rows, D_MODEL), lambda j: (0, 0)),
                  pl.BlockSpec((None, D_MODEL, tn), lambda j: (0, 0, j)),
                  pl.BlockSpec((None, 1, tn), lambda j: (0, 0, j))],
        out_specs=pl.BlockSpec((None, None, rows, tn), lambda j: (0, j // per, 0, j % per)),
        out_shape=jax.ShapeDtypeStruct((1, MOD_HEAD_VECS, rows, D_MODEL), F32),
        compiler_params=_cparams(1),
        name="modulation_head",
    )(c_all, w_mod, b_mod.reshape(DEPTH, 1, 6 * D_MODEL))


def _conv_in_kernel(x_ref, g_ref, sc_ref, sh_ref, wa_ref, wg_ref, ba_ref, bg_ref, u_ref, h_ref, *, S, L):
    @pl.when(pl.program_id(2) == 0)
    def _():
        _norm_mod(x_ref, h_ref, g_ref, sc_ref, sh_ref, S, L)

    h = h_ref[...]
    a = _dot(h, wa_ref[...]) + ba_ref[...]
    g = _dot(h, wg_ref[...]) + bg_ref[...]
    u_ref[...] = a * jax.nn.sigmoid(g)


def _conv_in(x, mod, g_norm, w_in, b_in, *, layer, S, L, tm):
    Bp, T, _ = x.shape
    tn = IN_PROJ_CHUNK
    J = D_MODEL // tn
    b2 = b_in.reshape(-1, 1, 2 * D_MODEL)
    return pl.pallas_call(
        functools.partial(_conv_in_kernel, S=S, L=L),
        grid=(Bp, T // tm, J),
        in_specs=[_row_spec(tm), _vec_spec(4 * layer + 0), _mod_spec(S, layer, 1), _mod_spec(S, layer, 0),
                  pl.BlockSpec((None, D_MODEL, tn), lambda b, t, j: (0, 0, j)),
                  pl.BlockSpec((None, D_MODEL, tn), lambda b, t, j: (0, 0, j + J)),
                  pl.BlockSpec((None, 1, tn), lambda b, t, j: (0, 0, j)),
                  pl.BlockSpec((None, 1, tn), lambda b, t, j: (0, 0, j + J))],
        out_specs=pl.BlockSpec((None, tm, tn), lambda b, t, j: (b, t, j)),
        out_shape=jax.ShapeDtypeStruct((Bp, T, D_MODEL), F32),
        scratch_shapes=[pltpu.VMEM((tm, D_MODEL), F32)],
        compiler_params=_cparams(3),
        name="conv_in",
    )(x, g_norm, mod, mod, w_in, w_in, b2, b2)


def _dwconv_block(win, wb_ref, bias, cols, R):
    n_groups = (CONV_K + 1 + SUBLANES) // SUBLANES
    win_rows = R + CONV_HIST
    acc = jnp.broadcast_to(bias, (R // SUBLANES, SUBLANES, LANES))
    for sh in range(SUBLANES):
        wsh = win if sh == 0 else pltpu.roll(win, win_rows - sh, axis=0)
        for a in range(n_groups):
            k = SUBLANES * a + sh - 2
            if 0 <= k < CONV_K:
                tap = wsh[SUBLANES * a:SUBLANES * a + R, :]
                acc = acc + tap.reshape(R // SUBLANES, SUBLANES, LANES) * wb_ref[k, :, cols]
    return acc.reshape(R, LANES)


def _fill_tap_table(wdw_ref, wb_ref):
    for k in range(CONV_K):
        wb_ref[k] = jnp.broadcast_to(wdw_ref[k:k + 1, :], (SUBLANES, D_MODEL))


def _conv_mix_kernel(x_ref, g_ref, sc_ref, sh_ref, wa_ref, wg_ref, ba_ref, bg_ref, wdw_ref, bdw_ref,
                     c_ref, wm_ref, bm_ref,
                     y_ref, tail_ref, mod_ref, h_ref, wb_ref, carry_ref, *, L, R, nb, n_mod):
    t = pl.program_id(1)
    j = pl.program_id(2)
    tn = wa_ref.shape[1]
    rb = L // nb
    step = (pl.program_id(0) * pl.num_programs(1) + t) * pl.num_programs(2) + j

    @pl.when(step < n_mod)
    def _():
        c = c_ref[...]
        mod_ref[...] = _dot(c * jax.nn.sigmoid(c), wm_ref[...]) + bm_ref[...]

    @pl.when((pl.program_id(0) == 0) & (t == 0) & (j == 0))
    def _():
        _fill_tap_table(wdw_ref, wb_ref)
        carry_ref[...] = jnp.zeros(carry_ref.shape, F32)

    @pl.when(j == 0)
    def _():
        _norm_mod(x_ref, h_ref, g_ref, sc_ref, sh_ref, 1, L)

    wa = wa_ref[...].astype(BF16)
    wg = wg_ref[...].astype(BF16)
    tail = jnp.where(t == 0, 0.0, carry_ref[j])
    for blk in range(nb):
        hb = h_ref[blk * rb:(blk + 1) * rb, :]
        a = _dot(hb, wa) + ba_ref[...]
        g = _dot(hb, wg) + bg_ref[...]
        u = a * jax.nn.sigmoid(g)
        ufull = jnp.concatenate([tail, u], axis=0)
        for cb in range(tn // LANES):
            gcols = pl.ds(pl.multiple_of(j * tn + cb * LANES, LANES), LANES)
            lcols = slice(cb * LANES, (cb + 1) * LANES)
            bias = bdw_ref[:, gcols]
            for r in range(rb // R):
                win = ufull[r * R:r * R + R + CONV_HIST, lcols]
                y_ref[blk * rb + r * R:blk * rb + (r + 1) * R, lcols] = _dwconv_block(win, wb_ref, bias, gcols, R)
        tail = u[rb - CONV_HIST:, :]
    carry_ref[j] = tail
    tail_ref[...] = tail


def _conv_mix(x, mod_head, g_norm, w_in, b_in, w_dw, b_dw, c_all, w_mod, b_mod, *, layer, tm, R, nb):
    Bp, T, _ = x.shape
    tn = MIX_CHUNK
    J = D_MODEL // tn
    nT = T // tm
    b2 = b_in.reshape(-1, 1, 2 * D_MODEL)
    rows = c_all.shape[0]
    per_vec = D_MODEL // MOD_BLOCK
    per_layer = 6 * per_vec
    n_mod = DEPTH * per_layer
    assert n_mod <= Bp * nT * J

    def mod_block(b, t, j):
        sb = jnp.minimum((b * nT + t) * J + j, n_mod - 1)
        return sb // per_layer, sb % per_layer

    def wm_map(b, t, j):
        layer_, cb = mod_block(b, t, j)
        return layer_, 0, cb

    def mod_out_map(b, t, j):
        layer_, cb = mod_block(b, t, j)
        return layer_, cb // per_vec, 0, cb % per_vec

    return pl.pallas_call(
        functools.partial(_conv_mix_kernel, L=tm, R=R, nb=nb, n_mod=n_mod),
        grid=(Bp, nT, J),
        in_specs=[_row_spec(tm), _vec_spec(4 * layer + 0), _mod_spec(1, layer, 1), _mod_spec(1, layer, 0),
                  pl.BlockSpec((None, D_MODEL, tn), lambda b, t, j: (0, 0, j)),
                  pl.BlockSpec((None, D_MODEL, tn), lambda b, t, j: (0, 0, j + J)),
                  pl.BlockSpec((None, 1, tn), lambda b, t, j: (0, 0, j)),
                  pl.BlockSpec((None, 1, tn), lambda b, t, j: (0, 0, j + J)),
                  pl.BlockSpec((None, CONV_K, D_MODEL), lambda b, t, j: (0, 0, 0)),
                  _vec_spec(0),
                  pl.BlockSpec((rows, D_MODEL), lambda b, t, j: (0, 0)),
                  pl.BlockSpec((None, D_MODEL, MOD_BLOCK), wm_map),
                  pl.BlockSpec((None, 1, MOD_BLOCK), wm_map)],
        out_specs=[pl.BlockSpec((None, tm, tn), lambda b, t, j: (b, t, j)),
                   pl.BlockSpec((None, None, CONV_HIST, tn), lambda b, t, j: (b, t, 0, j)),
                   pl.BlockSpec((None, None, rows, MOD_BLOCK), mod_out_map)],
        out_shape=[jax.ShapeDtypeStruct((Bp, T, D_MODEL), F32),
                   jax.ShapeDtypeStruct((Bp, nT, CONV_HIST, D_MODEL), F32),
                   jax.ShapeDtypeStruct((DEPTH, 6, rows, D_MODEL), F32)],
        scratch_shapes=[pltpu.VMEM((tm, D_MODEL), BF16),
                        pltpu.VMEM((CONV_K, SUBLANES, D_MODEL), F32),
                        pltpu.VMEM((J, CONV_HIST, tn), F32)],
        compiler_params=_cparams(3),
        name="conv_mix",
    )(x, g_norm, mod_head, mod_head, w_in, w_in, b2, b2, w_dw, b_dw.reshape(-1, 1, D_MODEL),
      c_all, w_mod, b_mod.reshape(DEPTH, 1, 6 * D_MODEL))


def _conv_proj_kernel(y_ref, lng_ref, lnb_ref, wout_ref, bout_ref, x_ref, g_ref, ga_ref, x1_ref, *, L):
    rb = L // PROJ_BLOCKS
    w = wout_ref[...]
    for blk in range(PROJ_BLOCKS):
        rows = slice(blk * rb, (blk + 1) * rb)
        a = _ln_silu_value(y_ref[rows, :], lng_ref, lnb_ref)
        m = _dot(a, w) + bout_ref[...]
        x1_ref[rows, :] = _residual_value(m, x_ref[rows, :], g_ref, ga_ref[0:1, :])


def _conv_proj(y, x, mod, g_norm, ln_g, ln_b, w_out, b_out, *, layer, tm):
    Bp, T, _ = x.shape
    vec = lambda a: a.reshape(-1, 1, D_MODEL)
    return pl.pallas_call(
        functools.partial(_conv_proj_kernel, L=tm),
        grid=(Bp, T // tm),
        in_specs=[_row_spec(tm), _vec_spec(0), _vec_spec(0),
                  pl.BlockSpec((None, D_MODEL, D_MODEL), lambda b, t: (0, 0, 0), pipeline_mode=pl.Buffered(1)),
                  _vec_spec(0), _row_spec(tm), _vec_spec(4 * layer + 1), _mod_spec(1, layer, 2)],
        out_specs=_row_spec(tm),
        out_shape=jax.ShapeDtypeStruct((Bp, T, D_MODEL), F32),
        compiler_params=_cparams(2),
        name="conv_proj",
    )(y, vec(ln_g), vec(ln_b), w_out, vec(b_out), x, g_norm, mod)


def _dwconv(full_ref, wb_ref, bdw_ref, y_ref, S, L, R):
    def body(cb, c):
        cols = pl.ds(pl.multiple_of(cb * LANES, LANES), LANES)
        bias = bdw_ref[:, cols]
        for s in range(S):
            for r in range(L // R):
                win = full_ref[s, r * R:r * R + R + CONV_HIST, cols]
                y_ref[s * L + r * R:s * L + (r + 1) * R, cols] = _dwconv_block(win, wb_ref, bias, cols, R)
        return c

    lax.fori_loop(0, D_MODEL // LANES, body, 0)


def _ln_silu_value(y, lng_ref, lnb_ref):
    mu = jnp.mean(y, axis=-1, keepdims=True)
    yc = y - mu
    var = jnp.mean(yc * yc, axis=-1, keepdims=True)
    z = yc * lax.rsqrt(var + EPS) * lng_ref[...] + lnb_ref[...]
    return z * jax.nn.sigmoid(z)


def _ln_silu(y_ref, lng_ref, lnb_ref, S, L):
    _row_loop(S, L, [y_ref], y_ref, lambda s, y: _ln_silu_value(y, lng_ref, lnb_ref))


def _conv_out_kernel(u_ref, hist_ref, wdw_ref, bdw_ref, lng_ref, lnb_ref, wout_ref, bout_ref, x_ref, g_ref, ga_ref,
                     x1_ref, full_ref, wb_ref, y_ref, *, S, L, R):
    _fill_tap_table(wdw_ref, wb_ref)
    pad = CONV_HIST - (CONV_K - 1)
    full_ref[:, 0:pad, :] = jnp.zeros((S, pad, D_MODEL), F32)
    full_ref[:, pad:CONV_HIST, :] = hist_ref[...]
    full_ref[:, CONV_HIST:CONV_HIST + L, :] = u_ref[...].reshape(S, L, D_MODEL)
    _dwconv(full_ref, wb_ref, bdw_ref, y_ref, S, L, R)
    _ln_silu(y_ref, lng_ref, lnb_ref, S, L)
    x1_ref[...] = _dot(y_ref[...], wout_ref[...]) + bout_ref[...]
    _residual(x1_ref, x_ref, g_ref, ga_ref, x1_ref, S, L)


def _conv_out(u, hist, x, mod, g_norm, w_dw, b_dw, ln_g, ln_b, w_out, b_out, *, layer, S, L, tm, R):
    Bp, T, _ = x.shape
    assert Bp == 1 and T == tm
    vec = lambda a: a.reshape(-1, 1, D_MODEL)
    return pl.pallas_call(
        functools.partial(_conv_out_kernel, S=S, L=L, R=R),
        grid=(Bp, T // tm),
        in_specs=[_row_spec(tm), pl.BlockSpec((S, CONV_K - 1, D_MODEL), lambda b, t: (0, 0, 0)),
                  pl.BlockSpec((None, CONV_K, D_MODEL), lambda b, t: (0, 0, 0)),
                  _vec_spec(0), _vec_spec(0), _vec_spec(0),
                  pl.BlockSpec((None, D_MODEL, D_MODEL), lambda b, t: (0, 0, 0), pipeline_mode=pl.Buffered(1)),
                  _vec_spec(0), _row_spec(tm), _vec_spec(4 * layer + 1), _mod_spec(S, layer, 2)],
        out_specs=_row_spec(tm),
        out_shape=jax.ShapeDtypeStruct((Bp, T, D_MODEL), F32),
        scratch_shapes=[pltpu.VMEM((S, CONV_HIST + L, D_MODEL), F32),
                        pltpu.VMEM((CONV_K, SUBLANES, D_MODEL), F32),
                        pltpu.VMEM((tm, D_MODEL), F32)],
        compiler_params=_cparams(2),
        name="conv_out",
    )(u, hist, w_dw, vec(b_dw), vec(ln_g), vec(ln_b), w_out, vec(b_out), x, g_norm, mod)


def _ffn_kernel(*refs, S, L, nb, prompt):
    if prompt:
        (x_ref, g2_ref, sc_ref, sh_ref, wg_ref, wv_ref, wdw_ref, bdw_ref, wd_ref, g3_ref, ga_ref,
         x2_ref, st_ref, h_ref, carry_ref) = refs
    else:
        (x_ref, g2_ref, sc_ref, sh_ref, wg_ref, wv_ref, wdw_ref, bdw_ref, wd_ref, g3_ref, ga_ref, hist_ref,
         x2_ref, st_ref, h_ref) = refs
    t = pl.program_id(1)
    j = pl.program_id(2)
    last_j = pl.num_programs(2) - 1
    tm = x_ref.shape[0]
    rb = tm // nb
    seg = min(L, rb)
    pad = SUBLANES
    inline = S == 1

    mm_dtype = h_ref.dtype

    def step(first, last):
        wg = wg_ref[...].astype(mm_dtype)
        wv = wv_ref[...].astype(mm_dtype)
        wd = wd_ref[...].astype(mm_dtype)
        w0 = wdw_ref[0:1, :]
        w1 = wdw_ref[1:2, :]
        w2 = wdw_ref[2:3, :]
        bd = bdw_ref[...]
        tail = None
        for blk in range(nb):
            rows = slice(blk * rb, (blk + 1) * rb)
            if first:
                x = x_ref[rows, :]
                inv = lax.rsqrt(jnp.mean(x * x, axis=-1, keepdims=True) + EPS)
                hb = (x * inv * (g2_ref[...] * (1.0 + sc_ref[0:1, :])) + sh_ref[0:1, :]).astype(mm_dtype)
                h_ref[rows, :] = hb
            else:
                hb = h_ref[rows, :]
            g = _dot(hb, wg)
            v = _dot(hb, wv)
            p1s, p2s = [], []
            for q in range(rb // seg):
                gq = g[q * seg:(q + 1) * seg, :]
                if not prompt:
                    s = blk * (rb // seg) + q
                    hist = hist_ref[s]
                    st_ref[s] = gq[seg - (FFN_K - 1):, :]
                elif blk == 0:
                    hist = jnp.where(t == 0, 0.0, carry_ref[j])
                else:
                    hist = tail
                gfull = jnp.concatenate([hist, gq], axis=0)
                p1s.append(pltpu.roll(gfull, 1, axis=0)[pad:, :])
                p2s.append(pltpu.roll(gfull, 2, axis=0)[pad:, :])
                tail = gq[seg - pad:, :]
            p1 = p1s[0] if len(p1s) == 1 else jnp.concatenate(p1s, axis=0)
            p2 = p2s[0] if len(p2s) == 1 else jnp.concatenate(p2s, axis=0)
            gc = g * w2 + p1 * w1 + p2 * w0 + bd
            act = (jax.nn.gelu(gc) * v).astype(mm_dtype)
            acc = _dot(act, wd)
            if not first:
                acc = x2_ref[rows, :] + acc
            if last:
                acc = _residual_value(acc, x_ref[rows, :], g3_ref, ga_ref[0:1, :])
            x2_ref[rows, :] = acc
        if prompt:
            carry_ref[j] = tail
            st_ref[0] = tail[pad - (FFN_K - 1):, :]

    if inline:
        pl.when(j == 0)(lambda: step(True, False))
        pl.when((j > 0) & (j < last_j))(lambda: step(False, False))
        pl.when(j == last_j)(lambda: step(False, True))
    else:
        @pl.when(j == 0)
        def _():
            _norm_mod(x_ref, h_ref, g2_ref, sc_ref, sh_ref, S, L)
            x2_ref[...] = jnp.zeros(x2_ref.shape, F32)

        step(False, False)

        @pl.when(j == last_j)
        def _():
            _residual(x2_ref, x_ref, g3_ref, ga_ref, x2_ref, S, L)


def _ffn(x, hist, mod, g_norm, w_up, w_dw, b_dw, w_down, *, layer, S, L, tm, tf, nb):
    Bp, T, _ = x.shape
    J = D_FF // tf
    nT = T // tm
    prompt = hist is None
    x_spec = pl.BlockSpec((None, tm, D_MODEL), lambda b, t, j: (b, t, 0), pipeline_mode=pl.Buffered(1))
    in_specs = [x_spec, _vec_spec(4 * layer + 2), _mod_spec(S, layer, 4), _mod_spec(S, layer, 3),
                pl.BlockSpec((None, D_MODEL, tf), lambda b, t, j: (layer, 0, j)),
                pl.BlockSpec((None, D_MODEL, tf), lambda b, t, j: (layer, 0, j + J)),
                pl.BlockSpec((None, FFN_K, tf), lambda b, t, j: (layer, 0, j)),
                pl.BlockSpec((None, 1, tf), lambda b, t, j: (layer, 0, j)),
                pl.BlockSpec((None, tf, D_MODEL), lambda b, t, j: (layer, j, 0)),
                _vec_spec(4 * layer + 3), _mod_spec(S, layer, 5)]
    args = [x, g_norm, mod, mod, w_up, w_up, w_dw, b_dw.reshape(DEPTH, 1, D_FF), w_down, g_norm, mod]
    scratch = [pltpu.VMEM((tm, D_MODEL), BF16 if prompt else F32)]
    if prompt:
        scratch.append(pltpu.VMEM((J, SUBLANES, tf), F32))
    else:
        in_specs.append(pl.BlockSpec((S, SUBLANES, tf), lambda b, t, j: (0, 0, j)))
        args.append(hist)
    return pl.pallas_call(
        functools.partial(_ffn_kernel, S=S, L=L, nb=nb, prompt=prompt),
        grid=(Bp, nT, J),
        in_specs=in_specs,
        out_specs=[_row_spec(tm),
                   pl.BlockSpec((None, None, S, FFN_K - 1, tf), lambda b, t, j: (b, t, 0, 0, j))],
        out_shape=[jax.ShapeDtypeStruct((Bp, T, D_MODEL), F32),
                   jax.ShapeDtypeStruct((Bp, nT, S, FFN_K - 1, D_FF), F32)],
        scratch_shapes=scratch,
        compiler_params=pltpu.CompilerParams(dimension_semantics=("arbitrary",) * 3,
                                             vmem_limit_bytes=VMEM_LIMIT_FFN),
        name="ffn",
    )(*args)


def _qkv_kernel(x_ref, g_ref, sc_ref, sh_ref, wq_ref, wk_ref, wv_ref, q_ref, k_ref, v_ref, h_ref, *, S, L):
    tm = x_ref.shape[0]
    if S > 1:
        _norm_mod(x_ref, h_ref, g_ref, sc_ref, sh_ref, S, L)
        blocks = [(slice(0, tm), h_ref[...])]
    else:
        rb = tm // PROJ_BLOCKS
        mul = g_ref[...] * (1.0 + sc_ref[0:1, :])
        blocks = []
        for blk in range(PROJ_BLOCKS):
            rows = slice(blk * rb, (blk + 1) * rb)
            x = x_ref[rows, :]
            inv = lax.rsqrt(jnp.mean(x * x, axis=-1, keepdims=True) + EPS)
            blocks.append((rows, x * inv * mul + sh_ref[0:1, :]))
    wq = wq_ref[...]
    wk = wk_ref[...]
    wv = wv_ref[...]
    for rows, h in blocks:
        q_ref[rows, :] = _dot(h, wq)
        k_ref[rows, :] = _dot(h, wk)
        v_ref[rows, :] = _dot(h, wv)


def _qkv(x, mod, g_norm, w_q, w_k, w_v, *, layer, S, L, tm):
    Bp, T, _ = x.shape
    kv_spec = pl.BlockSpec((None, tm, KV_DIM), lambda b, t: (b, t, 0))
    resident = lambda n: pl.BlockSpec((None, D_MODEL, n), lambda b, t: (0, 0, 0), pipeline_mode=pl.Buffered(1))
    return pl.pallas_call(
        functools.partial(_qkv_kernel, S=S, L=L),
        grid=(Bp, T // tm),
        in_specs=[_row_spec(tm), _vec_spec(4 * layer + 0), _mod_spec(S, layer, 1), _mod_spec(S, layer, 0),
                  resident(D_MODEL), resident(KV_DIM), resident(KV_DIM)],
        out_specs=[_row_spec(tm), kv_spec, kv_spec],
        out_shape=[jax.ShapeDtypeStruct((Bp, T, D_MODEL), F32),
                   jax.ShapeDtypeStruct((Bp, T, KV_DIM), F32),
                   jax.ShapeDtypeStruct((Bp, T, KV_DIM), F32)],
        scratch_shapes=[pltpu.VMEM((tm, D_MODEL), F32)],
        compiler_params=_cparams(2),
        name="qkv",
    )(x, g_norm, mod, mod, w_q, w_k, w_v)


REL_SPAN = 3 * LANES


def _bias_kernel(rb_ref, prof_ref, valid_ref, o_ref, *, Qn):
    prof = prof_ref[...]
    valid = valid_ref[...] != 0
    early = lax.broadcasted_iota(jnp.int32, valid.shape, 0) < WINDOW
    for hh in range(N_HEADS):
        f = jnp.zeros(prof.shape, F32)
        for b in range(N_BUCKETS):
            f = jnp.where(prof == b, rb_ref[b, hh], f)
        h, g = divmod(hh, GROUP)
        off = (g * Qn) % LANES
        x = jnp.broadcast_to(f[0:1, :], (KEY_WIN, REL_SPAN))
        r = pltpu.roll(x, WINDOW + off, axis=1, stride=1, stride_axis=0)
        tab = jnp.where(valid, r[:, off:off + Qn], NEG_INF)
        o_ref[0, h, :, g * Qn:(g + 1) * Qn] = tab
        if o_ref.shape[0] > 1:
            o_ref[1, h, :, g * Qn:(g + 1) * Qn] = jnp.where(early, NEG_INF, tab)


def _t5_bucket(rel):
    half = N_BUCKETS // 2
    max_exact = half // 2
    n = jnp.abs(rel)
    ret = jnp.where(rel > 0, half, 0)
    nf = jnp.maximum(n, 1).astype(F32)
    large = max_exact + (jnp.log(nf / max_exact) / math.log(MAX_DISTANCE / max_exact)
                         * (half - max_exact)).astype(jnp.int32)
    large = jnp.minimum(large, half - 1)
    return ret + jnp.where(n < max_exact, n, large)


def _bias_table(rel_bias, Qn, lo, hi, variants):
    assert KEY_WIN + Qn - 1 <= REL_SPAN
    k = jnp.arange(KEY_WIN)[:, None]
    q = jnp.arange(Qn)[None, :]
    valid = ((k >= lo(q)) & (k < hi(q))).astype(jnp.int32)
    prof = jnp.broadcast_to(_t5_bucket(WINDOW - jnp.arange(REL_SPAN)).astype(jnp.int32)[None, :], (SUBLANES, REL_SPAN))
    return pl.pallas_call(
        functools.partial(_bias_kernel, Qn=Qn),
        in_specs=[pl.BlockSpec(memory_space=pltpu.SMEM),
                  pl.BlockSpec((SUBLANES, REL_SPAN), lambda: (0, 0)),
                  pl.BlockSpec((KEY_WIN, Qn), lambda: (0, 0))],
        out_specs=pl.BlockSpec((variants, N_KV_HEADS, KEY_WIN, GROUP * Qn), lambda: (0, 0, 0, 0)),
        out_shape=jax.ShapeDtypeStruct((variants, N_KV_HEADS, KEY_WIN, GROUP * Qn), F32),
        name="rel_bias_table",
    )(rel_bias, prof, valid)


def _attend(q_ref, r0, Qn, kw, vw, bias_ref, sel, sink_ref, o_ref, qs_ref):
    rows = pl.ds(r0, Qn)
    paired = Qn == LANES and 2 * HEAD_DIM == LANES and GROUP % 2 == 0
    if paired:
        qts = []
        for h in range(N_KV_HEADS):
            pieces = []
            for gp in range(GROUP // 2):
                c0 = (h * GROUP + 2 * gp) * HEAD_DIM
                t2 = (q_ref[rows, c0:c0 + LANES] * ATTN_SCALE).T
                pieces += [t2[0:HEAD_DIM, :], t2[HEAD_DIM:, :]]
            qts.append(jnp.concatenate(pieces, axis=1))
    else:
        for g in range(GROUP):
            for h in range(N_KV_HEADS):
                src = (h * GROUP + g) * HEAD_DIM
                qs_ref[g * Qn:(g + 1) * Qn, h * HEAD_DIM:(h + 1) * HEAD_DIM] = (
                    q_ref[rows, src:src + HEAD_DIM] * ATTN_SCALE)
        qt = qs_ref[...].T
        qts = [qt[h * HEAD_DIM:(h + 1) * HEAD_DIM, :] for h in range(N_KV_HEADS)]
    vt = vw.T
    outs = []
    for h in range(N_KV_HEADS):
        hd = slice(h * HEAD_DIM, (h + 1) * HEAD_DIM)
        s = _dot(kw[:, hd], qts[h]) + bias_ref[sel, h]
        sk = sink_ref[h]
        mx = jnp.maximum(jnp.max(s, axis=0, keepdims=True), sk)
        p = jnp.exp(s - mx)
        den = jnp.sum(p, axis=0, keepdims=True) + jnp.exp(sk - mx)
        ot = _dot(vt[hd, :], p) * (1.0 / den)
        if paired:
            for gp in range(GROUP // 2):
                c0 = (h * GROUP + 2 * gp) * HEAD_DIM
                two = jnp.concatenate([ot[:, (2 * gp) * Qn:(2 * gp + 1) * Qn],
                                       ot[:, (2 * gp + 1) * Qn:(2 * gp + 2) * Qn]], axis=0)
                o_ref[rows, c0:c0 + LANES] = two.T
        else:
            outs.append(ot)
    if not paired:
        o2 = jnp.concatenate(outs, axis=0).T
        for g in range(GROUP):
            for h in range(N_KV_HEADS):
                dst = (h * GROUP + g) * HEAD_DIM
                o_ref[rows, dst:dst + HEAD_DIM] = o2[g * Qn:(g + 1) * Qn, h * HEAD_DIM:(h + 1) * HEAD_DIM]


def _attn_prompt_kernel(q_ref, kp_ref, kc_ref, vp_ref, vc_ref, bias_ref, sink_ref, o_ref, kw_ref, vw_ref, qs_ref,
                        *, tq, Qn):
    t = pl.program_id(1)
    kw_ref[0:WINDOW, :] = kp_ref[...]
    kw_ref[WINDOW:WINDOW + tq, :] = kc_ref[...]
    vw_ref[0:WINDOW, :] = vp_ref[...]
    vw_ref[WINDOW:WINDOW + tq, :] = vc_ref[...]

    def body(mp, carry):
        for u in range(ATTN_UNITS_PER_ITER):
            m = mp * ATTN_UNITS_PER_ITER + u
            r0 = pl.multiple_of(m * Qn, Qn)
            kw = kw_ref[pl.ds(r0, KEY_WIN), :]
            vw = vw_ref[pl.ds(r0, KEY_WIN), :]
            sel = ((t == 0) & (m == 0)).astype(jnp.int32)
            _attend(q_ref, r0, Qn, kw, vw, bias_ref, sel, sink_ref, o_ref, qs_ref.at[u])
        return carry

    lax.fori_loop(0, tq // (Qn * ATTN_UNITS_PER_ITER), body, 0)


def _attn_sample_kernel(q_ref, ck_ref, kn_ref, cv_ref, vn_ref, bias_ref, sink_ref, o_ref, kw_ref, vw_ref, qs_ref,
                        *, S, L):
    P = ck_ref.shape[1]
    kw_ref[:, 0:P, :] = ck_ref[...]
    kw_ref[:, P:P + L, :] = kn_ref[...].reshape(S, L, KV_DIM)
    kw_ref[:, P + L:, :] = jnp.zeros((S, KEY_WIN - P - L, KV_DIM), F32)
    vw_ref[:, 0:P, :] = cv_ref[...]
    vw_ref[:, P:P + L, :] = vn_ref[...].reshape(S, L, KV_DIM)
    vw_ref[:, P + L:, :] = jnp.zeros((S, KEY_WIN - P - L, KV_DIM), F32)

    def body(sp, carry):
        for u in range(ATTN_UNITS_PER_ITER):
            s = sp * ATTN_UNITS_PER_ITER + u
            r0 = pl.multiple_of(s * L, L)
            _attend(q_ref, r0, L, kw_ref[s], vw_ref[s], bias_ref, 0, sink_ref, o_ref, qs_ref.at[u])
        return carry

    lax.fori_loop(0, S // ATTN_UNITS_PER_ITER, body, 0)


def _attn_prompt(q, k, v, bias, sink, *, tq):
    Bp, T, _ = q.shape
    Qn = 2 * CHUNK
    per = tq // WINDOW
    prev_spec = pl.BlockSpec((None, WINDOW, KV_DIM), lambda b, t: (b, jnp.maximum(t * per - 1, 0), 0))
    cur_spec = pl.BlockSpec((None, tq, KV_DIM), lambda b, t: (b, t, 0))
    return pl.pallas_call(
        functools.partial(_attn_prompt_kernel, tq=tq, Qn=Qn),
        grid=(Bp, T // tq),
        in_specs=[_row_spec(tq), prev_spec, cur_spec, prev_spec, cur_spec,
                  pl.BlockSpec((2, N_KV_HEADS, KEY_WIN, GROUP * Qn), lambda b, t: (0, 0, 0, 0),
                               pipeline_mode=pl.Buffered(1)),
                  pl.BlockSpec((N_KV_HEADS, 1, GROUP * Qn), lambda b, t: (0, 0, 0))],
        out_specs=_row_spec(tq),
        out_shape=jax.ShapeDtypeStruct((Bp, T, D_MODEL), F32),
        scratch_shapes=[pltpu.VMEM((WINDOW + tq, KV_DIM), F32), pltpu.VMEM((WINDOW + tq, KV_DIM), F32),
                        pltpu.VMEM((ATTN_UNITS_PER_ITER, GROUP * Qn, KV_DIM), F32)],
        compiler_params=_cparams(2),
        name="attn_prompt",
    )(q, k, k, v, v, bias, sink)


def _attn_sample(q, k, v, cache_k, cache_v, bias, sink, *, S, L):
    return pl.pallas_call(
        functools.partial(_attn_sample_kernel, S=S, L=L),
        out_shape=jax.ShapeDtypeStruct((S * L, D_MODEL), F32),
        scratch_shapes=[pltpu.VMEM((S, KEY_WIN, KV_DIM), F32), pltpu.VMEM((S, KEY_WIN, KV_DIM), F32),
                        pltpu.VMEM((ATTN_UNITS_PER_ITER, GROUP * L, KV_DIM), F32)],
        compiler_params=pltpu.CompilerParams(vmem_limit_bytes=VMEM_LIMIT),
        name="attn_sample",
    )(q, cache_k, k, cache_v, v, bias, sink)


def _residual_value(m, x, g_ref, ga):
    inv = lax.rsqrt(jnp.mean(m * m, axis=-1, keepdims=True) + EPS)
    return x + m * inv * (g_ref[...] * ga)


def _attn_out_kernel(o_ref, w_ref, x_ref, g_ref, ga_ref, x1_ref, *, S, L):
    tm = o_ref.shape[0]
    if S > 1:
        x1_ref[...] = _dot(o_ref[...], w_ref[...])
        _residual(x1_ref, x_ref, g_ref, ga_ref, x1_ref, S, L)
        return
    rb = tm // PROJ_BLOCKS
    w = w_ref[...]
    for blk in range(PROJ_BLOCKS):
        rows = slice(blk * rb, (blk + 1) * rb)
        x1_ref[rows, :] = _residual_value(_dot(o_ref[rows, :], w), x_ref[rows, :], g_ref, ga_ref[0:1, :])


def _attn_out(o, x, mod, g_norm, w_o, *, layer, S, L, tm):
    Bp, T, _ = x.shape
    return pl.pallas_call(
        functools.partial(_attn_out_kernel, S=S, L=L),
        grid=(Bp, T // tm),
        in_specs=[_row_spec(tm),
                  pl.BlockSpec((None, D_MODEL, D_MODEL), lambda b, t: (0, 0, 0), pipeline_mode=pl.Buffered(1)),
                  _row_spec(tm), _vec_spec(4 * layer + 1), _mod_spec(S, layer, 2)],
        out_specs=_row_spec(tm),
        out_shape=jax.ShapeDtypeStruct((Bp, T, D_MODEL), F32),
        compiler_params=_cparams(2),
        name="attn_out",
    )(o, w_o, x, g_norm, mod)


def kernel(x_prompt, x_sample, c_prompt, c_sample, cache_conv, cache_k, cache_v, cache_ffn, w_mod, b_mod, g_norm, conv_w_in, conv_b_in, conv_w_dw, conv_b_dw, conv_ln_g, conv_ln_b, conv_w_out, conv_b_out, attn_w_q, attn_w_k, attn_w_v, attn_w_o, attn_sinks, rel_bias, ffn_w_up, ffn_w_dw, ffn_b_dw, ffn_w_down):
    B, T, D = x_prompt.shape
    SB, SL, _ = x_sample.shape
    assert SB == MOD_ROWS_SAMPLE
    c_all = jnp.concatenate(
        [c_sample, jnp.pad(c_prompt[:, None, :], ((0, 0), (0, SUBLANES - 1), (0, 0))).reshape(B * SUBLANES, D)], axis=0)
    mod_head = _modulation_head(c_all, w_mod, b_mod)
    gn = g_norm.reshape(DEPTH * 4, 1, D)

    geo_p = dict(S=1, L=STREAM_ROWS, tm=STREAM_ROWS)
    geo_p_small = dict(S=1, L=RESIDENT_ROWS, tm=RESIDENT_ROWS)
    geo_s = dict(S=SB, L=SL, tm=SB * SL)

    xp = x_prompt
    xs = x_sample.reshape(1, SB * SL, D)

    yp, tail_p, mod = _conv_mix(xp, mod_head, gn, conv_w_in, conv_b_in, conv_w_dw, conv_b_dw, c_all, w_mod, b_mod,
                                layer=0, tm=STREAM_ROWS, R=MIX_CONV_ROWS, nb=MIX_ROW_BLOCKS)
    xp = _conv_proj(yp, xp, mod, gn, conv_ln_g, conv_ln_b, conv_w_out, conv_b_out, layer=0, tm=RESIDENT_ROWS)
    us = _conv_in(xs, mod_head, gn, conv_w_in, conv_b_in, layer=0, **geo_s)
    conv_args = (conv_w_dw, conv_b_dw, conv_ln_g, conv_ln_b, conv_w_out, conv_b_out)
    xs = _conv_out(us, cache_conv[0], xs, mod, gn, *conv_args, layer=0, R=SL, **geo_s)
    conv_state_p = tail_p[:, -1, CONV_HIST - (CONV_K - 1):, :][None]
    conv_state_s = us.reshape(SB, SL, D)[:, SL - (CONV_K - 1):, :][None]

    ffn_p, ffn_s = [], []
    ffn_hist = jnp.pad(cache_ffn, ((0, 0), (0, 0), (SUBLANES - (FFN_K - 1), 0), (0, 0)))

    def run_ffn(i, xp, xs):
        w = (ffn_w_up, ffn_w_dw, ffn_b_dw, ffn_w_down)
        xp, st_p = _ffn(xp, None, mod, gn, *w, layer=i, tf=FFN_CHUNK, nb=FFN_ROW_BLOCKS, **geo_p)
        xs, st_s = _ffn(xs, ffn_hist[i], mod, gn, *w, layer=i, tf=FFN_CHUNK, nb=FFN_ROW_BLOCKS, **geo_s)
        ffn_p.append(st_p[:, -1, 0])
        ffn_s.append(st_s[0, 0])
        return xp, xs

    xp, xs = run_ffn(0, xp, xs)

    qp, kp, vp = _qkv(xp, mod, gn, attn_w_q, attn_w_k, attn_w_v, layer=1, **geo_p_small)
    qs, ks, vs = _qkv(xs, mod, gn, attn_w_q, attn_w_k, attn_w_v, layer=1, **geo_s)
    P = cache_k.shape[2]
    assert P == WINDOW and P + SL <= KEY_WIN
    bias_p = _bias_table(rel_bias, 2 * CHUNK,
                         lambda q: (q // CHUNK) * CHUNK, lambda q: (q // CHUNK) * CHUNK + WINDOW + CHUNK, 2)
    bias_s = _bias_table(rel_bias, SL, lambda q: 0 * q, lambda q: 0 * q + P + SL, 1)
    sink_p = jnp.repeat(attn_sinks[0], 2 * CHUNK).reshape(N_KV_HEADS, 1, GROUP * 2 * CHUNK)
    sink_s = jnp.repeat(attn_sinks[0], SL).reshape(N_KV_HEADS, 1, GROUP * SL)
    op = _attn_prompt(qp, kp, vp, bias_p, sink_p, tq=ATTN_ROWS)
    ck = cache_k[0].reshape(SB, P, KV_DIM)
    cv = cache_v[0].reshape(SB, P, KV_DIM)
    os_ = _attn_sample(qs[0], ks[0], vs[0], ck, cv, bias_s, sink_s, S=SB, L=SL)[None]
    xp = _attn_out(op, xp, mod, gn, attn_w_o, layer=1, **geo_p_small)
    xs = _attn_out(os_, xs, mod, gn, attn_w_o, layer=1, **geo_s)
    keep = min(WINDOW, T)
    k_state_p = kp[:, T - keep:].reshape(B, keep, N_KV_HEADS, HEAD_DIM)[None]
    v_state_p = vp[:, T - keep:].reshape(B, keep, N_KV_HEADS, HEAD_DIM)[None]
    k_state_s = jnp.concatenate([ck, ks.reshape(SB, SL, KV_DIM)], axis=1)[:, SL:]
    v_state_s = jnp.concatenate([cv, vs.reshape(SB, SL, KV_DIM)], axis=1)[:, SL:]
    k_state_s = k_state_s.reshape(SB, P, N_KV_HEADS, HEAD_DIM)[None]
    v_state_s = v_state_s.reshape(SB, P, N_KV_HEADS, HEAD_DIM)[None]

    xp, xs = run_ffn(1, xp, xs)

    return (xp, xs.reshape(SB, SL, D), conv_state_p, conv_state_s,
            k_state_p, v_state_p, k_state_s, v_state_s,
            jnp.stack(ffn_p), jnp.stack(ffn_s))
```

```python
import functools
import math

import jax
import jax.numpy as jnp
from jax import lax
from jax.experimental import pallas as pl
from jax.experimental.pallas import tpu as pltpu

F32 = jnp.float32
BF16 = jnp.bfloat16

D_MODEL = 2048
D_FF = 5632
DEPTH = 2
CONV_K = 31
FFN_K = 3
N_HEADS = 32
N_KV_HEADS = 4
GROUP = N_HEADS // N_KV_HEADS
HEAD_DIM = 64
KV_DIM = N_KV_HEADS * HEAD_DIM
CHUNK = 64
WINDOW = 128
N_BUCKETS = 32
MAX_DISTANCE = 128
EPS = 1e-6
NEG_INF = -1e30
ATTN_SCALE = HEAD_DIM ** -0.5

SUBLANES = 8
LANES = 128
CONV_HIST = 32
KEY_WIN = 2 * WINDOW
VMEM_LIMIT = 58 * 1024 * 1024
VMEM_LIMIT_FFN = 62 * 1024 * 1024
MOD_ROWS_SAMPLE = 16


def _cparams(n_grid):
    return pltpu.CompilerParams(dimension_semantics=("arbitrary",) * n_grid,
                                vmem_limit_bytes=VMEM_LIMIT)


def _dot(a, b):
    return jnp.dot(a, b, preferred_element_type=F32)


ROW_SLAB = 64
ROW_GROUP = 2
PROJ_BLOCKS = 2

STREAM_ROWS = 1024
RESIDENT_ROWS = 512
FFN_CHUNK = 512
FFN_ROW_BLOCKS = 1
MIX_CHUNK = 256
MIX_ROW_BLOCKS = 2
MIX_CONV_ROWS = 64
IN_PROJ_CHUNK = 512
MOD_HEAD_CHUNK = 1024
ATTN_ROWS = 512
ATTN_UNITS_PER_ITER = 4


def _row_loop(S, L, srcs, dst, fn):
    slab = min(L, ROW_SLAB)
    n = L // slab
    group = ROW_GROUP if n % ROW_GROUP == 0 else 1
    for s in range(S):

        def body(r, c, s=s):
            rows = [pl.ds(pl.multiple_of(s * L + (r * group + i) * slab, slab), slab) for i in range(group)]
            vals = [[src[rw, :] for src in srcs] for rw in rows]
            outs = [fn(s, *v) for v in vals]
            for rw, out in zip(rows, outs):
                dst[rw, :] = out.astype(dst.dtype)
            return c

        lax.fori_loop(0, n // group, body, 0)


def _norm_mod(x_ref, dst_ref, g_ref, sc_ref, sh_ref, S, L):
    def fn(s, x):
        inv = lax.rsqrt(jnp.mean(x * x, axis=-1, keepdims=True) + EPS)
        mul = g_ref[...] * (1.0 + sc_ref[s:s + 1, :])
        return x * inv * mul + sh_ref[s:s + 1, :]

    _row_loop(S, L, [x_ref], dst_ref, fn)


def _residual(m_ref, x_ref, g_ref, ga_ref, out_ref, S, L):
    def fn(s, m, x):
        inv = lax.rsqrt(jnp.mean(m * m, axis=-1, keepdims=True) + EPS)
        return x + m * inv * (g_ref[...] * ga_ref[s:s + 1, :])

    _row_loop(S, L, [m_ref, x_ref], out_ref, fn)


def _row_spec(tm):
    return pl.BlockSpec((None, tm, D_MODEL), lambda b, t, *_: (b, t, 0))


def _vec_spec(index, n=D_MODEL):
    return pl.BlockSpec((None, 1, n), lambda *_: (index, 0, 0))


def _mod_spec(S, layer, which):
    if S == 1:
        first = MOD_ROWS_SAMPLE // SUBLANES
        return pl.BlockSpec((None, None, SUBLANES, D_MODEL), lambda b, t, *_: (layer, which, first + b, 0))
    return pl.BlockSpec((None, None, S, D_MODEL), lambda b, t, *_: (layer, which, 0, 0))


def _mod_kernel(c_ref, w_ref, b_ref, o_ref):
    c = c_ref[...]
    o_ref[...] = _dot(c * jax.nn.sigmoid(c), w_ref[...]) + b_ref[...]


MOD_HEAD_VECS = 2
MOD_BLOCK = 512


def _modulation_head(c_all, w_mod, b_mod):
    rows = c_all.shape[0]
    tn = MOD_HEAD_CHUNK
    per = D_MODEL // tn
    return pl.pallas_call(
        _mod_kernel,
        grid=(MOD_HEAD_VECS * per,),
        in_specs=[pl.BlockSpec((rows, D_MODEL), lambda j: (0, 0)),
                  pl.BlockSpec((None, D_MODEL, tn), lambda j: (0, 0, j)),
                  pl.BlockSpec((None, 1, tn), lambda j: (0, 0, j))],
        out_specs=pl.BlockSpec((None, None, rows, tn), lambda j: (0, j // per, 0, j % per)),
        out_shape=jax.ShapeDtypeStruct((1, MOD_HEAD_VECS, rows, D_MODEL), F32),
        compiler_params=_cparams(1),
        name="modulation_head",
    )(c_all, w_mod, b_mod.reshape(DEPTH, 1, 6 * D_MODEL))


def _conv_in_kernel(x_ref, g_ref, sc_ref, sh_ref, wa_ref, wg_ref, ba_ref, bg_ref, u_ref, h_ref, *, S, L):
    @pl.when(pl.program_id(2) == 0)
    def _():
        _norm_mod(x_ref, h_ref, g_ref, sc_ref, sh_ref, S, L)

    h = h_ref[...]
    a = _dot(h, wa_ref[...]) + ba_ref[...]
    g = _dot(h, wg_ref[...]) + bg_ref[...]
    u_ref[...] = a * jax.nn.sigmoid(g)


def _conv_in(x, mod, g_norm, w_in, b_in, *, layer, S, L, tm):
    Bp, T, _ = x.shape
    tn = IN_PROJ_CHUNK
    J = D_MODEL // tn
    b2 = b_in.reshape(-1, 1, 2 * D_MODEL)
    return pl.pallas_call(
        functools.partial(_conv_in_kernel, S=S, L=L),
        grid=(Bp, T // tm, J),
        in_specs=[_row_spec(tm), _vec_spec(4 * layer + 0), _mod_spec(S, layer, 1), _mod_spec(S, layer, 0),
                  pl.BlockSpec((None, D_MODEL, tn), lambda b, t, j: (0, 0, j)),
                  pl.BlockSpec((None, D_MODEL, tn), lambda b, t, j: (0, 0, j + J)),
                  pl.BlockSpec((None, 1, tn), lambda b, t, j: (0, 0, j)),
                  pl.BlockSpec((None, 1, tn), lambda b, t, j: (0, 0, j + J))],
        out_specs=pl.BlockSpec((None, tm, tn), lambda b, t, j: (b, t, j)),
        out_shape=jax.ShapeDtypeStruct((Bp, T, D_MODEL), F32),
        scratch_shapes=[pltpu.VMEM((tm, D_MODEL), F32)],
        compiler_params=_cparams(3),
        name="conv_in",
    )(x, g_norm, mod, mod, w_in, w_in, b2, b2)


def _dwconv_block(win, wb_ref, bias, cols, R):
    n_groups = (CONV_K + 1 + SUBLANES) // SUBLANES
    win_rows = R + CONV_HIST
    acc = jnp.broadcast_to(bias, (R // SUBLANES, SUBLANES, LANES))
    for sh in range(SUBLANES):
        wsh = win if sh == 0 else pltpu.roll(win, win_rows - sh, axis=0)
        for a in range(n_groups):
            k = SUBLANES * a + sh - 2
            if 0 <= k < CONV_K:
                tap = wsh[SUBLANES * a:SUBLANES * a + R, :]
                acc = acc + tap.reshape(R // SUBLANES, SUBLANES, LANES) * wb_ref[k, :, cols]
    return acc.reshape(R, LANES)


def _fill_tap_table(wdw_ref, wb_ref):
    for k in range(CONV_K):
        wb_ref[k] = jnp.broadcast_to(wdw_ref[k:k + 1, :], (SUBLANES, D_MODEL))


def _conv_mix_kernel(x_ref, g_ref, sc_ref, sh_ref, wa_ref, wg_ref, ba_ref, bg_ref, wdw_ref, bdw_ref,
                     c_ref, wm_ref, bm_ref,
                     y_ref, tail_ref, mod_ref, h_ref, wb_ref, carry_ref, *, L, R, nb, n_mod):
    t = pl.program_id(1)
    j = pl.program_id(2)
    tn = wa_ref.shape[1]
    rb = L // nb
    step = (pl.program_id(0) * pl.num_programs(1) + t) * pl.num_programs(2) + j

    @pl.when(step < n_mod)
    def _():
        c = c_ref[...]
        mod_ref[...] = _dot(c * jax.nn.sigmoid(c), wm_ref[...]) + bm_ref[...]

    @pl.when((pl.program_id(0) == 0) & (t == 0) & (j == 0))
    def _():
        _fill_tap_table(wdw_ref, wb_ref)
        carry_ref[...] = jnp.zeros(carry_ref.shape, F32)

    @pl.when(j == 0)
    def _():
        _norm_mod(x_ref, h_ref, g_ref, sc_ref, sh_ref, 1, L)

    wa = wa_ref[...].astype(BF16)
    wg = wg_ref[...].astype(BF16)
    tail = jnp.where(t == 0, 0.0, carry_ref[j])
    for blk in range(nb):
        hb = h_ref[blk * rb:(blk + 1) * rb, :]
        a = _dot(hb, wa) + ba_ref[...]
        g = _dot(hb, wg) + bg_ref[...]
        u = a * jax.nn.sigmoid(g)
        ufull = jnp.concatenate([tail, u], axis=0)
        for cb in range(tn // LANES):
            gcols = pl.ds(pl.multiple_of(j * tn + cb * LANES, LANES), LANES)
            lcols = slice(cb * LANES, (cb + 1) * LANES)
            bias = bdw_ref[:, gcols]
            for r in range(rb // R):
                win = ufull[r * R:r * R + R + CONV_HIST, lcols]
                y_ref[blk * rb + r * R:blk * rb + (r + 1) * R, lcols] = _dwconv_block(win, wb_ref, bias, gcols, R)
        tail = u[rb - CONV_HIST:, :]
    carry_ref[j] = tail
    tail_ref[...] = tail


def _conv_mix(x, mod_head, g_norm, w_in, b_in, w_dw, b_dw, c_all, w_mod, b_mod, *, layer, tm, R, nb):
    Bp, T, _ = x.shape
    tn = MIX_CHUNK
    J = D_MODEL // tn
    nT = T // tm
    b2 = b_in.reshape(-1, 1, 2 * D_MODEL)
    rows = c_all.shape[0]
    per_vec = D_MODEL // MOD_BLOCK
    per_layer = 6 * per_vec
    n_mod = DEPTH * per_layer
    assert n_mod <= Bp * nT * J

    def mod_block(b, t, j):
        sb = jnp.minimum((b * nT + t) * J + j, n_mod - 1)
        return sb // per_layer, sb % per_layer

    def wm_map(b, t, j):
        layer_, cb = mod_block(b, t, j)
        return layer_, 0, cb

    def mod_out_map(b, t, j):
        layer_, cb = mod_block(b, t, j)
        return layer_, cb // per_vec, 0, cb % per_vec

    return pl.pallas_call(
        functools.partial(_conv_mix_kernel, L=tm, R=R, nb=nb, n_mod=n_mod),
        grid=(Bp, nT, J),
        in_specs=[_row_spec(tm), _vec_spec(4 * layer + 0), _mod_spec(1, layer, 1), _mod_spec(1, layer, 0),
                  pl.BlockSpec((None, D_MODEL, tn), lambda b, t, j: (0, 0, j)),
                  pl.BlockSpec((None, D_MODEL, tn), lambda b, t, j: (0, 0, j + J)),
                  pl.BlockSpec((None, 1, tn), lambda b, t, j: (0, 0, j)),
                  pl.BlockSpec((None, 1, tn), lambda b, t, j: (0, 0, j + J)),
                  pl.BlockSpec((None, CONV_K, D_MODEL), lambda b, t, j: (0, 0, 0)),
                  _vec_spec(0),
                  pl.BlockSpec((rows, D_MODEL), lambda b, t, j: (0, 0)),
                  pl.BlockSpec((None, D_MODEL, MOD_BLOCK), wm_map),
                  pl.BlockSpec((None, 1, MOD_BLOCK), wm_map)],
        out_specs=[pl.BlockSpec((None, tm, tn), lambda b, t, j: (b, t, j)),
                   pl.BlockSpec((None, None, CONV_HIST, tn), lambda b, t, j: (b, t, 0, j)),
                   pl.BlockSpec((None, None, rows, MOD_BLOCK), mod_out_map)],
        out_shape=[jax.ShapeDtypeStruct((Bp, T, D_MODEL), F32),
                   jax.ShapeDtypeStruct((Bp, nT, CONV_HIST, D_MODEL), F32),
                   jax.ShapeDtypeStruct((DEPTH, 6, rows, D_MODEL), F32)],
        scratch_shapes=[pltpu.VMEM((tm, D_MODEL), BF16),
                        pltpu.VMEM((CONV_K, SUBLANES, D_MODEL), F32),
                        pltpu.VMEM((J, CONV_HIST, tn), F32)],
        compiler_params=_cparams(3),
        name="conv_mix",
    )(x, g_norm, mod_head, mod_head, w_in, w_in, b2, b2, w_dw, b_dw.reshape(-1, 1, D_MODEL),
      c_all, w_mod, b_mod.reshape(DEPTH, 1, 6 * D_MODEL))


def _conv_proj_kernel(y_ref, lng_ref, lnb_ref, wout_ref, bout_ref, x_ref, g_ref, ga_ref, x1_ref, *, L):
    rb = L // PROJ_BLOCKS
    w = wout_ref[...]
    for blk in range(PROJ_BLOCKS):
        rows = slice(blk * rb, (blk + 1) * rb)
        a = _ln_silu_value(y_ref[rows, :], lng_ref, lnb_ref)
        m = _dot(a, w) + bout_ref[...]
        x1_ref[rows, :] = _residual_value(m, x_ref[rows, :], g_ref, ga_ref[0:1, :])


def _conv_proj(y, x, mod, g_norm, ln_g, ln_b, w_out, b_out, *, layer, tm):
    Bp, T, _ = x.shape
    vec = lambda a: a.reshape(-1, 1, D_MODEL)
    return pl.pallas_call(
        functools.partial(_conv_proj_kernel, L=tm),
        grid=(Bp, T // tm),
        in_specs=[_row_spec(tm), _vec_spec(0), _vec_spec(0),
                  pl.BlockSpec((None, D_MODEL, D_MODEL), lambda b, t: (0, 0, 0), pipeline_mode=pl.Buffered(1)),
                  _vec_spec(0), _row_spec(tm), _vec_spec(4 * layer + 1), _mod_spec(1, layer, 2)],
        out_specs=_row_spec(tm),
        out_shape=jax.ShapeDtypeStruct((Bp, T, D_MODEL), F32),
        compiler_params=_cparams(2),
        name="conv_proj",
    )(y, vec(ln_g), vec(ln_b), w_out, vec(b_out), x, g_norm, mod)


def _dwconv(full_ref, wb_ref, bdw_ref, y_ref, S, L, R):
    def body(cb, c):
        cols = pl.ds(pl.multiple_of(cb * LANES, LANES), LANES)
        bias = bdw_ref[:, cols]
        for s in range(S):
            for r in range(L // R):
                win = full_ref[s, r * R:r * R + R + CONV_HIST, cols]
                y_ref[s * L + r * R:s * L + (r + 1) * R, cols] = _dwconv_block(win, wb_ref, bias, cols, R)
        return c

    lax.fori_loop(0, D_MODEL // LANES, body, 0)


def _ln_silu_value(y, lng_ref, lnb_ref):
    mu = jnp.mean(y, axis=-1, keepdims=True)
    yc = y - mu
    var = jnp.mean(yc * yc, axis=-1, keepdims=True)
    z = yc * lax.rsqrt(var + EPS) * lng_ref[...] + lnb_ref[...]
    return z * jax.nn.sigmoid(z)


def _ln_silu(y_ref, lng_ref, lnb_ref, S, L):
    _row_loop(S, L, [y_ref], y_ref, lambda s, y: _ln_silu_value(y, lng_ref, lnb_ref))


def _conv_out_kernel(u_ref, hist_ref, wdw_ref, bdw_ref, lng_ref, lnb_ref, wout_ref, bout_ref, x_ref, g_ref, ga_ref,
                     x1_ref, full_ref, wb_ref, y_ref, *, S, L, R):
    _fill_tap_table(wdw_ref, wb_ref)
    pad = CONV_HIST - (CONV_K - 1)
    full_ref[:, 0:pad, :] = jnp.zeros((S, pad, D_MODEL), F32)
    full_ref[:, pad:CONV_HIST, :] = hist_ref[...]
    full_ref[:, CONV_HIST:CONV_HIST + L, :] = u_ref[...].reshape(S, L, D_MODEL)
    _dwconv(full_ref, wb_ref, bdw_ref, y_ref, S, L, R)
    _ln_silu(y_ref, lng_ref, lnb_ref, S, L)
    x1_ref[...] = _dot(y_ref[...], wout_ref[...]) + bout_ref[...]
    _residual(x1_ref, x_ref, g_ref, ga_ref, x1_ref, S, L)


def _conv_out(u, hist, x, mod, g_norm, w_dw, b_dw, ln_g, ln_b, w_out, b_out, *, layer, S, L, tm, R):
    Bp, T, _ = x.shape
    assert Bp == 1 and T == tm
    vec = lambda a: a.reshape(-1, 1, D_MODEL)
    return pl.pallas_call(
        functools.partial(_conv_out_kernel, S=S, L=L, R=R),
        grid=(Bp, T // tm),
        in_specs=[_row_spec(tm), pl.BlockSpec((S, CONV_K - 1, D_MODEL), lambda b, t: (0, 0, 0)),
                  pl.BlockSpec((None, CONV_K, D_MODEL), lambda b, t: (0, 0, 0)),
                  _vec_spec(0), _vec_spec(0), _vec_spec(0),
                  pl.BlockSpec((None, D_MODEL, D_MODEL), lambda b, t: (0, 0, 0), pipeline_mode=pl.Buffered(1)),
                  _vec_spec(0), _row_spec(tm), _vec_spec(4 * layer + 1), _mod_spec(S, layer, 2)],
        out_specs=_row_spec(tm),
        out_shape=jax.ShapeDtypeStruct((Bp, T, D_MODEL), F32),
        scratch_shapes=[pltpu.VMEM((S, CONV_HIST + L, D_MODEL), F32),
                        pltpu.VMEM((CONV_K, SUBLANES, D_MODEL), F32),
                        pltpu.VMEM((tm, D_MODEL), F32)],
        compiler_params=_cparams(2),
        name="conv_out",
    )(u, hist, w_dw, vec(b_dw), vec(ln_g), vec(ln_b), w_out, vec(b_out), x, g_norm, mod)


def _ffn_kernel(*refs, S, L, nb, prompt):
    if prompt:
        (x_ref, g2_ref, sc_ref, sh_ref, wg_ref, wv_ref, wdw_ref, bdw_ref, wd_ref, g3_ref, ga_ref,
         x2_ref, st_ref, h_ref, carry_ref) = refs
    else:
        (x_ref, g2_ref, sc_ref, sh_ref, wg_ref, wv_ref, wdw_ref, bdw_ref, wd_ref, g3_ref, ga_ref, hist_ref,
         x2_ref, st_ref, h_ref) = refs
    t = pl.program_id(1)
    j = pl.program_id(2)
    last_j = pl.num_programs(2) - 1
    tm = x_ref.shape[0]
    rb = tm // nb
    seg = min(L, rb)
    pad = SUBLANES
    inline = S == 1

    mm_dtype = h_ref.dtype

    def step(first, last):
        wg = wg_ref[...].astype(mm_dtype)
        wv = wv_ref[...].astype(mm_dtype)
        wd = wd_ref[...].astype(mm_dtype)
        w0 = wdw_ref[0:1, :]
        w1 = wdw_ref[1:2, :]
        w2 = wdw_ref[2:3, :]
        bd = bdw_ref[...]
        tail = None
        for blk in range(nb):
            rows = slice(blk * rb, (blk + 1) * rb)
            if first:
                x = x_ref[rows, :]
                inv = lax.rsqrt(jnp.mean(x * x, axis=-1, keepdims=True) + EPS)
                hb = (x * inv * (g2_ref[...] * (1.0 + sc_ref[0:1, :])) + sh_ref[0:1, :]).astype(mm_dtype)
                h_ref[rows, :] = hb
            else:
                hb = h_ref[rows, :]
            g = _dot(hb, wg)
            v = _dot(hb, wv)
            p1s, p2s = [], []
            for q in range(rb // seg):
                gq = g[q * seg:(q + 1) * seg, :]
                if not prompt:
                    s = blk * (rb // seg) + q
                    hist = hist_ref[s]
                    st_ref[s] = gq[seg - (FFN_K - 1):, :]
                elif blk == 0:
                    hist = jnp.where(t == 0, 0.0, carry_ref[j])
                else:
                    hist = tail
                gfull = jnp.concatenate([hist, gq], axis=0)
                p1s.append(pltpu.roll(gfull, 1, axis=0)[pad:, :])
                p2s.append(pltpu.roll(gfull, 2, axis=0)[pad:, :])
                tail = gq[seg - pad:, :]
            p1 = p1s[0] if len(p1s) == 1 else jnp.concatenate(p1s, axis=0)
            p2 = p2s[0] if len(p2s) == 1 else jnp.concatenate(p2s, axis=0)
            gc = g * w2 + p1 * w1 + p2 * w0 + bd
            act = (jax.nn.gelu(gc) * v).astype(mm_dtype)
            acc = _dot(act, wd)
            if not first:
                acc = x2_ref[rows, :] + acc
            if last:
                acc = _residual_value(acc, x_ref[rows, :], g3_ref, ga_ref[0:1, :])
            x2_ref[rows, :] = acc
        if prompt:
            carry_ref[j] = tail
            st_ref[0] = tail[pad - (FFN_K - 1):, :]

    if inline:
        pl.when(j == 0)(lambda: step(True, False))
        pl.when((j > 0) & (j < last_j))(lambda: step(False, False))
        pl.when(j == last_j)(lambda: step(False, True))
    else:
        @pl.when(j == 0)
        def _():
            _norm_mod(x_ref, h_ref, g2_ref, sc_ref, sh_ref, S, L)
            x2_ref[...] = jnp.zeros(x2_ref.shape, F32)

        step(False, False)

        @pl.when(j == last_j)
        def _():
            _residual(x2_ref, x_ref, g3_ref, ga_ref, x2_ref, S, L)


def _ffn(x, hist, mod, g_norm, w_up, w_dw, b_dw, w_down, *, layer, S, L, tm, tf, nb):
    Bp, T, _ = x.shape
    J = D_FF // tf
    nT = T // tm
    prompt = hist is None
    x_spec = pl.BlockSpec((None, tm, D_MODEL), lambda b, t, j: (b, t, 0), pipeline_mode=pl.Buffered(1))
    in_specs = [x_spec, _vec_spec(4 * layer + 2), _mod_spec(S, layer, 4), _mod_spec(S, layer, 3),
                pl.BlockSpec((None, D_MODEL, tf), lambda b, t, j: (layer, 0, j)),
                pl.BlockSpec((None, D_MODEL, tf), lambda b, t, j: (layer, 0, j + J)),
                pl.BlockSpec((None, FFN_K, tf), lambda b, t, j: (layer, 0, j)),
                pl.BlockSpec((None, 1, tf), lambda b, t, j: (layer, 0, j)),
                pl.BlockSpec((None, tf, D_MODEL), lambda b, t, j: (layer, j, 0)),
                _vec_spec(4 * layer + 3), _mod_spec(S, layer, 5)]
    args = [x, g_norm, mod, mod, w_up, w_up, w_dw, b_dw.reshape(DEPTH, 1, D_FF), w_down, g_norm, mod]
    scratch = [pltpu.VMEM((tm, D_MODEL), BF16 if prompt else F32)]
    if prompt:
        scratch.append(pltpu.VMEM((J, SUBLANES, tf), F32))
    else:
        in_specs.append(pl.BlockSpec((S, SUBLANES, tf), lambda b, t, j: (0, 0, j)))
        args.append(hist)
    return pl.pallas_call(
        functools.partial(_ffn_kernel, S=S, L=L, nb=nb, prompt=prompt),
        grid=(Bp, nT, J),
        in_specs=in_specs,
        out_specs=[_row_spec(tm),
                   pl.BlockSpec((None, None, S, FFN_K - 1, tf), lambda b, t, j: (b, t, 0, 0, j))],
        out_shape=[jax.ShapeDtypeStruct((Bp, T, D_MODEL), F32),
                   jax.ShapeDtypeStruct((Bp, nT, S, FFN_K - 1, D_FF), F32)],
        scratch_shapes=scratch,
        compiler_params=pltpu.CompilerParams(dimension_semantics=("arbitrary",) * 3,
                                             vmem_limit_bytes=VMEM_LIMIT_FFN),
        name="ffn",
    )(*args)


def _qkv_kernel(x_ref, g_ref, sc_ref, sh_ref, wq_ref, wk_ref, wv_ref, q_ref, k_ref, v_ref, h_ref, *, S, L):
    tm = x_ref.shape[0]
    if S > 1:
        _norm_mod(x_ref, h_ref, g_ref, sc_ref, sh_ref, S, L)
        blocks = [(slice(0, tm), h_ref[...])]
    else:
        rb = tm // PROJ_BLOCKS
        mul = g_ref[...] * (1.0 + sc_ref[0:1, :])
        blocks = []
        for blk in range(PROJ_BLOCKS):
            rows = slice(blk * rb, (blk + 1) * rb)
            x = x_ref[rows, :]
            inv = lax.rsqrt(jnp.mean(x * x, axis=-1, keepdims=True) + EPS)
            blocks.append((rows, x * inv * mul + sh_ref[0:1, :]))
    wq = wq_ref[...]
    wk = wk_ref[...]
    wv = wv_ref[...]
    for rows, h in blocks:
        q_ref[rows, :] = _dot(h, wq)
        k_ref[rows, :] = _dot(h, wk)
        v_ref[rows, :] = _dot(h, wv)


def _qkv(x, mod, g_norm, w_q, w_k, w_v, *, layer, S, L, tm):
    Bp, T, _ = x.shape
    kv_spec = pl.BlockSpec((None, tm, KV_DIM), lambda b, t: (b, t, 0))
    resident = lambda n: pl.BlockSpec((None, D_MODEL, n), lambda b, t: (0, 0, 0), pipeline_mode=pl.Buffered(1))
    return pl.pallas_call(
        functools.partial(_qkv_kernel, S=S, L=L),
        grid=(Bp, T // tm),
        in_specs=[_row_spec(tm), _vec_spec(4 * layer + 0), _mod_spec(S, layer, 1), _mod_spec(S, layer, 0),
                  resident(D_MODEL), resident(KV_DIM), resident(KV_DIM)],
        out_specs=[_row_spec(tm), kv_spec, kv_spec],
        out_shape=[jax.ShapeDtypeStruct((Bp, T, D_MODEL), F32),
                   jax.ShapeDtypeStruct((Bp, T, KV_DIM), F32),
                   jax.ShapeDtypeStruct((Bp, T, KV_DIM), F32)],
        scratch_shapes=[pltpu.VMEM((tm, D_MODEL), F32)],
        compiler_params=_cparams(2),
        name="qkv",
    )(x, g_norm, mod, mod, w_q, w_k, w_v)


REL_SPAN = 3 * LANES


def _bias_kernel(rb_ref, prof_ref, valid_ref, o_ref, *, Qn):
    prof = prof_ref[...]
    valid = valid_ref[...] != 0
    early = lax.broadcasted_iota(jnp.int32, valid.shape, 0) < WINDOW
    for hh in range(N_HEADS):
        f = jnp.zeros(prof.shape, F32)
        for b in range(N_BUCKETS):
            f = jnp.where(prof == b, rb_ref[b, hh], f)
        h, g = divmod(hh, GROUP)
        off = (g * Qn) % LANES
        x = jnp.broadcast_to(f[0:1, :], (KEY_WIN, REL_SPAN))
        r = pltpu.roll(x, WINDOW + off, axis=1, stride=1, stride_axis=0)
        tab = jnp.where(valid, r[:, off:off + Qn], NEG_INF)
        o_ref[0, h, :, g * Qn:(g + 1) * Qn] = tab
        if o_ref.shape[0] > 1:
            o_ref[1, h, :, g * Qn:(g + 1) * Qn] = jnp.where(early, NEG_INF, tab)


def _t5_bucket(rel):
    half = N_BUCKETS // 2
    max_exact = half // 2
    n = jnp.abs(rel)
    ret = jnp.where(rel > 0, half, 0)
    nf = jnp.maximum(n, 1).astype(F32)
    large = max_exact + (jnp.log(nf / max_exact) / math.log(MAX_DISTANCE / max_exact)
                         * (half - max_exact)).astype(jnp.int32)
    large = jnp.minimum(large, half - 1)
    return ret + jnp.where(n < max_exact, n, large)


def _bias_table(rel_bias, Qn, lo, hi, variants):
    assert KEY_WIN + Qn - 1 <= REL_SPAN
    k = jnp.arange(KEY_WIN)[:, None]
    q = jnp.arange(Qn)[None, :]
    valid = ((k >= lo(q)) & (k < hi(q))).astype(jnp.int32)
    prof = jnp.broadcast_to(_t5_bucket(WINDOW - jnp.arange(REL_SPAN)).astype(jnp.int32)[None, :], (SUBLANES, REL_SPAN))
    return pl.pallas_call(
        functools.partial(_bias_kernel, Qn=Qn),
        in_specs=[pl.BlockSpec(memory_space=pltpu.SMEM),
                  pl.BlockSpec((SUBLANES, REL_SPAN), lambda: (0, 0)),
                  pl.BlockSpec((KEY_WIN, Qn), lambda: (0, 0))],
        out_specs=pl.BlockSpec((variants, N_KV_HEADS, KEY_WIN, GROUP * Qn), lambda: (0, 0, 0, 0)),
        out_shape=jax.ShapeDtypeStruct((variants, N_KV_HEADS, KEY_WIN, GROUP * Qn), F32),
        name="rel_bias_table",
    )(rel_bias, prof, valid)


def _attend(q_ref, r0, Qn, kw, vw, bias_ref, sel, sink_ref, o_ref, qs_ref):
    rows = pl.ds(r0, Qn)
    paired = Qn == LANES and 2 * HEAD_DIM == LANES and GROUP % 2 == 0
    assert paired or qs_ref is not None
    if paired:
        qts = []
        for h in range(N_KV_HEADS):
            pieces = []
            for gp in range(GROUP // 2):
                c0 = (h * GROUP + 2 * gp) * HEAD_DIM
                t2 = (q_ref[rows, c0:c0 + LANES] * ATTN_SCALE).T
                pieces += [t2[0:HEAD_DIM, :], t2[HEAD_DIM:, :]]
            qts.append(jnp.concatenate(pieces, axis=1))
    else:
        for g in range(GROUP):
            for h in range(N_KV_HEADS):
                src = (h * GROUP + g) * HEAD_DIM
                qs_ref[g * Qn:(g + 1) * Qn, h * HEAD_DIM:(h + 1) * HEAD_DIM] = (
                    q_ref[rows, src:src + HEAD_DIM] * ATTN_SCALE)
        qt = qs_ref[...].T
        qts = [qt[h * HEAD_DIM:(h + 1) * HEAD_DIM, :] for h in range(N_KV_HEADS)]
    vt = vw.T
    outs = []
    for h in range(N_KV_HEADS):
        hd = slice(h * HEAD_DIM, (h + 1) * HEAD_DIM)
        s = _dot(kw[:, hd], qts[h]) + bias_ref[sel, h]
        sk = sink_ref[h]
        mx = jnp.maximum(jnp.max(s, axis=0, keepdims=True), sk)
        p = jnp.exp(s - mx)
        den = jnp.sum(p, axis=0, keepdims=True) + jnp.exp(sk - mx)
        ot = _dot(vt[hd, :], p) * (1.0 / den)
        if paired:
            for gp in range(GROUP // 2):
                c0 = (h * GROUP + 2 * gp) * HEAD_DIM
                two = jnp.concatenate([ot[:, (2 * gp) * Qn:(2 * gp + 1) * Qn],
                                       ot[:, (2 * gp + 1) * Qn:(2 * gp + 2) * Qn]], axis=0)
                o_ref[rows, c0:c0 + LANES] = two.T
        else:
            outs.append(ot)
    if not paired:
        o2 = jnp.concatenate(outs, axis=0).T
        for g in range(GROUP):
            for h in range(N_KV_HEADS):
                dst = (h * GROUP + g) * HEAD_DIM
                o_ref[rows, dst:dst + HEAD_DIM] = o2[g * Qn:(g + 1) * Qn, h * HEAD_DIM:(h + 1) * HEAD_DIM]


def _attn_prompt_kernel(q_ref, kp_ref, kc_ref, vp_ref, vc_ref, bias_ref, sink_ref, o_ref, kw_ref, vw_ref,
                        *, tq, Qn):
    t = pl.program_id(1)
    kw_ref[0:WINDOW, :] = kp_ref[...]
    kw_ref[WINDOW:WINDOW + tq, :] = kc_ref[...]
    vw_ref[0:WINDOW, :] = vp_ref[...]
    vw_ref[WINDOW:WINDOW + tq, :] = vc_ref[...]

    def body(mp, carry):
        for u in range(ATTN_UNITS_PER_ITER):
            m = mp * ATTN_UNITS_PER_ITER + u
            r0 = pl.multiple_of(m * Qn, Qn)
            kw = kw_ref[pl.ds(r0, KEY_WIN), :]
            vw = vw_ref[pl.ds(r0, KEY_WIN), :]
            sel = ((t == 0) & (m == 0)).astype(jnp.int32)
            _attend(q_ref, r0, Qn, kw, vw, bias_ref, sel, sink_ref, o_ref, None)
        return carry

    lax.fori_loop(0, tq // (Qn * ATTN_UNITS_PER_ITER), body, 0)


def _attn_sample_kernel(q_ref, ck_ref, kn_ref, cv_ref, vn_ref, bias_ref, sink_ref, o_ref, kw_ref, vw_ref, qs_ref,
                        *, S, L):
    P = ck_ref.shape[1]
    kw_ref[:, 0:P, :] = ck_ref[...]
    kw_ref[:, P:P + L, :] = kn_ref[...].reshape(S, L, KV_DIM)
    kw_ref[:, P + L:, :] = jnp.zeros((S, KEY_WIN - P - L, KV_DIM), F32)
    vw_ref[:, 0:P, :] = cv_ref[...]
    vw_ref[:, P:P + L, :] = vn_ref[...].reshape(S, L, KV_DIM)
    vw_ref[:, P + L:, :] = jnp.zeros((S, KEY_WIN - P - L, KV_DIM), F32)

    def body(sp, carry):
        for u in range(ATTN_UNITS_PER_ITER):
            s = sp * ATTN_UNITS_PER_ITER + u
            r0 = pl.multiple_of(s * L, L)
            _attend(q_ref, r0, L, kw_ref[s], vw_ref[s], bias_ref, 0, sink_ref, o_ref, qs_ref.at[u])
        return carry

    lax.fori_loop(0, S // ATTN_UNITS_PER_ITER, body, 0)


def _attn_prompt(q, k, v, bias, sink, *, tq):
    Bp, T, _ = q.shape
    Qn = 2 * CHUNK
    per = tq // WINDOW
    prev_spec = pl.BlockSpec((None, WINDOW, KV_DIM), lambda b, t: (b, jnp.maximum(t * per - 1, 0), 0))
    cur_spec = pl.BlockSpec((None, tq, KV_DIM), lambda b, t: (b, t, 0))
    return pl.pallas_call(
        functools.partial(_attn_prompt_kernel, tq=tq, Qn=Qn),
        grid=(Bp, T // tq),
        in_specs=[_row_spec(tq), prev_spec, cur_spec, prev_spec, cur_spec,
                  pl.BlockSpec((2, N_KV_HEADS, KEY_WIN, GROUP * Qn), lambda b, t: (0, 0, 0, 0),
                               pipeline_mode=pl.Buffered(1)),
                  pl.BlockSpec((N_KV_HEADS, 1, GROUP * Qn), lambda b, t: (0, 0, 0))],
        out_specs=_row_spec(tq),
        out_shape=jax.ShapeDtypeStruct((Bp, T, D_MODEL), F32),
        scratch_shapes=[pltpu.VMEM((WINDOW + tq, KV_DIM), F32), pltpu.VMEM((WINDOW + tq, KV_DIM), F32)],
        compiler_params=_cparams(2),
        name="attn_prompt",
    )(q, k, k, v, v, bias, sink)


def _attn_sample(q, k, v, cache_k, cache_v, bias, sink, *, S, L):
    return pl.pallas_call(
        functools.partial(_attn_sample_kernel, S=S, L=L),
        out_shape=jax.ShapeDtypeStruct((S * L, D_MODEL), F32),
        scratch_shapes=[pltpu.VMEM((S, KEY_WIN, KV_DIM), F32), pltpu.VMEM((S, KEY_WIN, KV_DIM), F32),
                        pltpu.VMEM((ATTN_UNITS_PER_ITER, GROUP * L, KV_DIM), F32)],
        compiler_params=pltpu.CompilerParams(vmem_limit_bytes=VMEM_LIMIT),
        name="attn_sample",
    )(q, cache_k, k, cache_v, v, bias, sink)


def _residual_value(m, x, g_ref, ga):
    inv = lax.rsqrt(jnp.mean(m * m, axis=-1, keepdims=True) + EPS)
    return x + m * inv * (g_ref[...] * ga)


def _attn_out_kernel(o_ref, w_ref, x_ref, g_ref, ga_ref, x1_ref, *, S, L):
    tm = o_ref.shape[0]
    if S > 1:
        x1_ref[...] = _dot(o_ref[...], w_ref[...])
        _residual(x1_ref, x_ref, g_ref, ga_ref, x1_ref, S, L)
        return
    rb = tm // PROJ_BLOCKS
    w = w_ref[...]
    for blk in range(PROJ_BLOCKS):
        rows = slice(blk * rb, (blk + 1) * rb)
        x1_ref[rows, :] = _residual_value(_dot(o_ref[rows, :], w), x_ref[rows, :], g_ref, ga_ref[0:1, :])


def _attn_out(o, x, mod, g_norm, w_o, *, layer, S, L, tm):
    Bp, T, _ = x.shape
    return pl.pallas_call(
        functools.partial(_attn_out_kernel, S=S, L=L),
        grid=(Bp, T // tm),
        in_specs=[_row_spec(tm),
                  pl.BlockSpec((None, D_MODEL, D_MODEL), lambda b, t: (0, 0, 0), pipeline_mode=pl.Buffered(1)),
                  _row_spec(tm), _vec_spec(4 * layer + 1), _mod_spec(S, layer, 2)],
        out_specs=_row_spec(tm),
        out_shape=jax.ShapeDtypeStruct((Bp, T, D_MODEL), F32),
        compiler_params=_cparams(2),
        name="attn_out",
    )(o, w_o, x, g_norm, mod)


def kernel(x_prompt, x_sample, c_prompt, c_sample, cache_conv, cache_k, cache_v, cache_ffn, w_mod, b_mod, g_norm, conv_w_in, conv_b_in, conv_w_dw, conv_b_dw, conv_ln_g, conv_ln_b, conv_w_out, conv_b_out, attn_w_q, attn_w_k, attn_w_v, attn_w_o, attn_sinks, rel_bias, ffn_w_up, ffn_w_dw, ffn_b_dw, ffn_w_down):
    B, T, D = x_prompt.shape
    SB, SL, _ = x_sample.shape
    assert SB == MOD_ROWS_SAMPLE
    c_all = jnp.concatenate(
        [c_sample, jnp.pad(c_prompt[:, None, :], ((0, 0), (0, SUBLANES - 1), (0, 0))).reshape(B * SUBLANES, D)], axis=0)
    mod_head = _modulation_head(c_all, w_mod, b_mod)
    gn = g_norm.reshape(DEPTH * 4, 1, D)

    geo_p = dict(S=1, L=STREAM_ROWS, tm=STREAM_ROWS)
    geo_p_small = dict(S=1, L=RESIDENT_ROWS, tm=RESIDENT_ROWS)
    geo_s = dict(S=SB, L=SL, tm=SB * SL)

    xp = x_prompt
    xs = x_sample.reshape(1, SB * SL, D)

    yp, tail_p, mod = _conv_mix(xp, mod_head, gn, conv_w_in, conv_b_in, conv_w_dw, conv_b_dw, c_all, w_mod, b_mod,
                                layer=0, tm=STREAM_ROWS, R=MIX_CONV_ROWS, nb=MIX_ROW_BLOCKS)
    xp = _conv_proj(yp, xp, mod, gn, conv_ln_g, conv_ln_b, conv_w_out, conv_b_out, layer=0, tm=RESIDENT_ROWS)
    us = _conv_in(xs, mod_head, gn, conv_w_in, conv_b_in, layer=0, **geo_s)
    conv_args = (conv_w_dw, conv_b_dw, conv_ln_g, conv_ln_b, conv_w_out, conv_b_out)
    xs = _conv_out(us, cache_conv[0], xs, mod, gn, *conv_args, layer=0, R=SL, **geo_s)
    conv_state_p = tail_p[:, -1, CONV_HIST - (CONV_K - 1):, :][None]
    conv_state_s = us.reshape(SB, SL, D)[:, SL - (CONV_K - 1):, :][None]

    ffn_p, ffn_s = [], []
    ffn_hist = jnp.pad(cache_ffn, ((0, 0), (0, 0), (SUBLANES - (FFN_K - 1), 0), (0, 0)))

    def run_ffn(i, xp, xs):
        w = (ffn_w_up, ffn_w_dw, ffn_b_dw, ffn_w_down)
        xp, st_p = _ffn(xp, None, mod, gn, *w, layer=i, tf=FFN_CHUNK, nb=FFN_ROW_BLOCKS, **geo_p)
        xs, st_s = _ffn(xs, ffn_hist[i], mod, gn, *w, layer=i, tf=FFN_CHUNK, nb=FFN_ROW_BLOCKS, **geo_s)
        ffn_p.append(st_p[:, -1, 0])
        ffn_s.append(st_s[0, 0])
        return xp, xs

    xp, xs = run_ffn(0, xp, xs)

    qp, kp, vp = _qkv(xp, mod, gn, attn_w_q, attn_w_k, attn_w_v, layer=1, **geo_p_small)
    qs, ks, vs = _qkv(xs, mod, gn, attn_w_q, attn_w_k, attn_w_v, layer=1, **geo_s)
    P = cache_k.shape[2]
    assert P == WINDOW and P + SL <= KEY_WIN
    bias_p = _bias_table(rel_bias, 2 * CHUNK,
                         lambda q: (q // CHUNK) * CHUNK, lambda q: (q // CHUNK) * CHUNK + WINDOW + CHUNK, 2)
    bias_s = _bias_table(rel_bias, SL, lambda q: 0 * q, lambda q: 0 * q + P + SL, 1)
    sink_p = jnp.repeat(attn_sinks[0], 2 * CHUNK).reshape(N_KV_HEADS, 1, GROUP * 2 * CHUNK)
    sink_s = jnp.repeat(attn_sinks[0], SL).reshape(N_KV_HEADS, 1, GROUP * SL)
    op = _attn_prompt(qp, kp, vp, bias_p, sink_p, tq=ATTN_ROWS)
    ck = cache_k[0].reshape(SB, P, KV_DIM)
    cv = cache_v[0].reshape(SB, P, KV_DIM)
    os_ = _attn_sample(qs[0], ks[0], vs[0], ck, cv, bias_s, sink_s, S=SB, L=SL)[None]
    xp = _attn_out(op, xp, mod, gn, attn_w_o, layer=1, **geo_p_small)
    xs = _attn_out(os_, xs, mod, gn, attn_w_o, layer=1, **geo_s)
    keep = min(WINDOW, T)
    k_state_p = kp[:, T - keep:].reshape(B, keep, N_KV_HEADS, HEAD_DIM)[None]
    v_state_p = vp[:, T - keep:].reshape(B, keep, N_KV_HEADS, HEAD_DIM)[None]
    k_state_s = jnp.concatenate([ck, ks.reshape(SB, SL, KV_DIM)], axis=1)[:, SL:]
    v_state_s = jnp.concatenate([cv, vs.reshape(SB, SL, KV_DIM)], axis=1)[:, SL:]
    k_state_s = k_state_s.reshape(SB, P, N_KV_HEADS, HEAD_DIM)[None]
    v_state_s = v_state_s.reshape(SB, P, N_KV_HEADS, HEAD_DIM)[None]

    xp, xs = run_ffn(1, xp, xs)

    return (xp, xs.reshape(SB, SL, D), conv_state_p, conv_state_s,
            k_state_p, v_state_p, k_state_s, v_state_s,
            jnp.stack(ffn_p), jnp.stack(ffn_s))
```
